```python
import jax, jax.numpy as jnp
from jax import lax
import numpy as np

D_MODEL = 1024
BATCH = 1
SEQ = 16384
DEPTH = 2
DEC_BATCH = 32
DEC_SEQ = 8
PAST_LEN = 16384
PAGE_SIZE = 128

HEAD_DIM = 64
RW_HEADS = 4
RW_WIDTH = RW_HEADS * HEAD_DIM
RW_DECAY_LORA = 64
RW_AAA_LORA = 64
RW_GATE_LORA = 128
RW_COLS = 3 * RW_WIDTH + RW_DECAY_LORA + RW_AAA_LORA + RW_GATE_LORA
RW_GN_EPS = 64e-5
SA_HEADS = 8
SA_WIDTH = SA_HEADS * HEAD_DIM
IDX_HEADS = 8
IDX_DIM = 64
SA_COLS = 3 * SA_WIDTH + IDX_HEADS * IDX_DIM + IDX_DIM + IDX_HEADS
IDX_SCALE = (IDX_DIM * IDX_HEADS) ** -0.5
TOPK_MAX = 256
Q_BLOCK = 128
LRU_WIDTH = 256
LRU_BLOCKS = 4
LRU_BLOCK_DIM = LRU_WIDTH // LRU_BLOCKS
LRU_CONV = 4
LRU_C = 8.0
LRU_COLS = 2 * LRU_WIDTH
MIX_WIDTH = RW_WIDTH + SA_WIDTH + LRU_WIDTH
IN_COLS = RW_COLS + SA_COLS + LRU_COLS
MEM_TOKENS = 256
MEM_HEADS = 4
MEM_HEAD_DIM = D_MODEL // MEM_HEADS
D_FF = ((-(-8 * D_MODEL // 3)) + 255) // 256 * 256
ROPE_THETA = 10000.0
NORM_EPS = 1e-6

kernel_name = 'hybrid_rwkv7_dsa_rglru_decoder_step'


def rmsnorm(x, g):
    xf = x.astype(jnp.float32)
    y = xf * lax.rsqrt(jnp.mean(xf * xf, axis=-1, keepdims=True) + NORM_EPS)
    return (y * g.astype(jnp.float32)).astype(x.dtype)


def rope(x, pos):
    half = x.shape[-1] // 2
    inv = ROPE_THETA ** (-jnp.arange(half, dtype=jnp.float32) / half)
    ang = pos.astype(jnp.float32)[:, None] * inv[None, :]
    cos, sin = jnp.cos(ang)[:, None, :], jnp.sin(ang)[:, None, :]
    xf = x.astype(jnp.float32)
    x1, x2 = xf[..., :half], xf[..., half:]
    return jnp.concatenate([x1 * cos - x2 * sin, x1 * sin + x2 * cos], axis=-1).astype(x.dtype)


def swiglu(h, wg, wu, wd):
    return (jax.nn.silu(h @ wg) * (h @ wu)) @ wd


def mixer_inputs(h, w_in_l, pos):
    B, T, _ = h.shape
    u = h @ w_in_l
    u_rw = u[..., :RW_COLS]
    u_sa = u[..., RW_COLS:RW_COLS + SA_COLS]
    u_lru = u[..., RW_COLS + SA_COLS:]
    q = rope(u_sa[..., 0:SA_WIDTH].reshape(B, T, SA_HEADS, HEAD_DIM), pos)
    k = rope(u_sa[..., SA_WIDTH:2 * SA_WIDTH].reshape(B, T, SA_HEADS, HEAD_DIM), pos)
    v = u_sa[..., 2 * SA_WIDTH:3 * SA_WIDTH].reshape(B, T, SA_HEADS, HEAD_DIM)
    o = 3 * SA_WIDTH
    qi = rope(u_sa[..., o:o + IDX_HEADS * IDX_DIM].reshape(B, T, IDX_HEADS, IDX_DIM), pos)
    o = o + IDX_HEADS * IDX_DIM
    ki = rope(u_sa[..., None, o:o + IDX_DIM], pos)[:, :, 0]
    o = o + IDX_DIM
    wi = u_sa[..., o:o + IDX_HEADS] * IDX_SCALE
    return u_rw, (q, k, v, qi, ki, wi), u_lru


def rwkv7_mix(u, prev, s0, mu, w0, w2, a0, a2, g2, k_k, k_a, r_k, ln_w, ln_b):
    f32 = jnp.float32
    B, T, _ = u.shape
    uf = u.astype(f32)
    u_prev = jnp.concatenate([prev.astype(f32)[:, None, :], uf[:, :-1]], axis=1)
    us = uf + mu.astype(f32) * (u_prev - uf)
    o1 = 3 * RW_WIDTH
    o2 = o1 + RW_DECAY_LORA
    o3 = o2 + RW_AAA_LORA
    r, k, v = us[..., :RW_WIDTH], us[..., RW_WIDTH:2 * RW_WIDTH], us[..., 2 * RW_WIDTH:o1]
    wd, ad, gd = us[..., o1:o2], us[..., o2:o3], us[..., o3:]
    logw = -jnp.exp(-jax.nn.softplus(-(w0 + jnp.tanh(wd) @ w2)) - 0.5)
    a = jax.nn.sigmoid(a0 + ad @ a2)
    g = jax.nn.sigmoid(gd) @ g2
    hd = lambda z: z.astype(f32).reshape(B, T, RW_HEADS, HEAD_DIM)
    r, k, v, a, decay = hd(r), hd(k), hd(v), hd(a), jnp.exp(hd(logw))
    kk = k * k_k.astype(f32).reshape(RW_HEADS, HEAD_DIM)
    kk = kk / jnp.maximum(jnp.sqrt(jnp.sum(kk * kk, axis=-1, keepdims=True)), 1e-12)
    k = k * (1.0 + (a - 1.0) * k_a.astype(f32).reshape(RW_HEADS, HEAD_DIM))

    def step(S, inp):
        r_t, w_t, k_t, v_t, kk_t, a_t = inp
        s_kk = jnp.einsum('bhvk,bhk->bhv', S, kk_t)
        S = (S * w_t[:, :, None, :] - s_kk[..., None] * (kk_t * a_t)[:, :, None, :]
             + v_t[..., None] * k_t[:, :, None, :])
        return S, jnp.einsum('bhvk,bhk->bhv', S, r_t)

    seq = tuple(jnp.moveaxis(z, 1, 0) for z in (r, decay, k, v, kk, a))
    s_final, y = lax.scan(step, s0.astype(f32), seq)
    y = jnp.moveaxis(y, 0, 1)
    mean = jnp.mean(y, axis=-1, keepdims=True)
    var = jnp.mean(jnp.square(y - mean), axis=-1, keepdims=True)
    y = ((y - mean) * lax.rsqrt(var + RW_GN_EPS)).reshape(B, T, RW_WIDTH) * ln_w + ln_b
    bonus = jnp.sum(r * k * r_k.astype(f32), axis=-1, keepdims=True) * v
    y = (y + bonus.reshape(B, T, RW_WIDTH)) * g
    return y.astype(u.dtype), u[:, -1], s_final


def rglru_mix(u, conv_buf, h0, conv_w, conv_b, wa, ba, wx, bx, lam, reset_first):
    f32 = jnp.float32
    B, T, _ = u.shape
    uf = u.astype(f32)
    xb, gate = uf[..., :LRU_WIDTH], uf[..., LRU_WIDTH:]
    xc = jnp.concatenate([conv_buf.astype(f32), xb], axis=1)
    y = conv_b.astype(f32) + conv_w[0] * xc[:, 0:T]
    for j in range(1, LRU_CONV):
        y = y + conv_w[j] * xc[:, j:j + T]
    yb = y.reshape(B, T, LRU_BLOCKS, LRU_BLOCK_DIM)
    gate_r = jax.nn.sigmoid(jnp.einsum('btgi,gij->btgj', yb, wa).reshape(B, T, LRU_WIDTH) + ba)
    gate_i = jax.nn.sigmoid(jnp.einsum('btgi,gij->btgj', yb, wx).reshape(B, T, LRU_WIDTH) + bx)
    log_a = -LRU_C * jax.nn.softplus(-lam.astype(f32)) * gate_r
    a = jnp.exp(log_a)
    mult = jnp.sqrt(-jnp.expm1(2.0 * log_a))
    if reset_first:
        mult = mult.at[:, 0].set(1.0)
    b = mult * gate_i * y
    b = b.at[:, 0].add(a[:, 0] * h0.astype(f32))
    _, h = lax.associative_scan(lambda l, r: (l[0] * r[0], r[0] * l[1] + r[1]), (a, b), axis=1)
    out = h * jax.nn.gelu(gate, approximate=True)
    return out.astype(u.dtype), xc[:, T:], h[:, -1]


def index_scores(qi, ki, wi):
    s = jnp.einsum('...thd,...sd->...ths', qi.astype(jnp.float32), ki.astype(jnp.float32))
    return jnp.einsum('...th,...ths->...ts', wi.astype(jnp.float32), jax.nn.relu(s))


def sparse_attend(q, kg, vg, valid):
    s = jnp.einsum('...thd,...tkhd->...thk', q.astype(jnp.float32), kg.astype(jnp.float32)) * HEAD_DIM ** -0.5
    s = jnp.where(valid[..., None, :], s, -jnp.inf)
    p = jax.nn.softmax(s, axis=-1)
    return jnp.einsum('...thk,...tkhd->...thd', p, vg.astype(jnp.float32)).astype(q.dtype)


def dsa_prompt(q, k, v, qi, ki, wi):
    S = q.shape[1]
    topk = min(TOPK_MAX, S // 4)
    nb = S // Q_BLOCK

    def one_seq(q, k, v, qi, ki, wi):
        kpos = jnp.arange(S)

        def blk(args):
            qb, qib, wib, t0 = args
            tpos = t0 + jnp.arange(Q_BLOCK)
            isc = index_scores(qib, ki, wib)
            isc = jnp.where(kpos[None, :] <= tpos[:, None], isc, -jnp.inf)
            _, idx = lax.top_k(isc, topk)
            return sparse_attend(qb, k[idx], v[idx], idx <= tpos[:, None])

        split = lambda z: z.reshape((nb, Q_BLOCK) + z.shape[1:])
        out = lax.map(blk, (split(q), split(qi), split(wi), jnp.arange(nb) * Q_BLOCK))
        return out.reshape(S, SA_HEADS, HEAD_DIM)

    return jax.vmap(one_seq)(q, k, v, qi, ki, wi)


def dsa_sample(q, k, v, qi, ki, wi, layer, cache_k, cache_v, cache_kidx, page_table):
    DB, T = q.shape[0], q.shape[1]
    L = PAST_LEN + T
    topk = min(TOPK_MAX, L // 4)
    ki_past = cache_kidx[layer, page_table].reshape(DB, PAST_LEN, IDX_DIM)
    ki_all = jnp.concatenate([ki_past.astype(ki.dtype), ki], axis=1)
    tpos = PAST_LEN + jnp.arange(T)
    isc = index_scores(qi, ki_all, wi)
    isc = jnp.where(jnp.arange(L)[None, None, :] <= tpos[None, :, None], isc, -jnp.inf)
    _, idx = lax.top_k(isc, topk)
    in_past = idx < PAST_LEN
    pidx = jnp.minimum(idx, PAST_LEN - 1)
    bsel = jnp.arange(DB)[:, None, None]
    phys = page_table[bsel, pidx // PAGE_SIZE]
    slot = pidx % PAGE_SIZE
    nidx = jnp.clip(idx - PAST_LEN, 0, T - 1)
    kg = jnp.where(in_past[..., None, None], cache_k[layer, phys, slot].astype(k.dtype), k[bsel, nidx])
    vg = jnp.where(in_past[..., None, None], cache_v[layer, phys, slot].astype(v.dtype), v[bsel, nidx])
    return sparse_attend(q, kg, vg, idx <= tpos[None, :, None])


def mem_attend(h, mk, mv, wq, wo):
    B, T, _ = h.shape
    q = (h @ wq).reshape(B, T, MEM_HEADS, MEM_HEAD_DIM)
    s = jnp.einsum('bthd,bmhd->bhtm', q.astype(jnp.float32), mk.astype(jnp.float32)) * MEM_HEAD_DIM ** -0.5
    p = jax.nn.softmax(s, axis=-1)
    o = jnp.einsum('bhtm,bmhd->bthd', p, mv.astype(jnp.float32)).reshape(B, T, D_MODEL).astype(h.dtype)
    return o @ wo


def setup_inputs(seed: int = 0) -> dict:
    key = jax.random.key(seed)
    ks = iter(jax.random.split(key, 64))
    f32 = jnp.float32
    nrm = lambda shape, scale: scale * jax.random.normal(next(ks), shape, f32)
    uni = lambda shape, lo, hi: jax.random.uniform(next(ks), shape, f32, lo, hi)
    n_pages = PAST_LEN // PAGE_SIZE
    n_used = DEC_BATCH * n_pages
    n_pool = n_used + (n_used + 3) // 4
    return {
        'x_prompt': nrm((BATCH, SEQ, D_MODEL), 1.0),
        'x_sample': nrm((DEC_BATCH, DEC_SEQ, D_MODEL), 1.0),
        'mem_prompt': nrm((BATCH, MEM_TOKENS, D_MODEL), 1.0),
        'cache_k': nrm((DEPTH, n_pool, PAGE_SIZE, SA_HEADS, HEAD_DIM), 1.0),
        'cache_v': nrm((DEPTH, n_pool, PAGE_SIZE, SA_HEADS, HEAD_DIM), 1.0),
        'cache_kidx': nrm((DEPTH, n_pool, PAGE_SIZE, IDX_DIM), 1.0),
        'cache_mem_k': nrm((DEPTH, DEC_BATCH, MEM_TOKENS, MEM_HEADS, MEM_HEAD_DIM), 1.0),
        'cache_mem_v': nrm((DEPTH, DEC_BATCH, MEM_TOKENS, MEM_HEADS, MEM_HEAD_DIM), 1.0),
        'state_rwkv': nrm((DEPTH, DEC_BATCH, RW_HEADS, HEAD_DIM, HEAD_DIM), 1.0),
        'state_rwkv_shift': nrm((DEPTH, DEC_BATCH, RW_COLS), 1.0),
        'state_lru_h': nrm((DEPTH, DEC_BATCH, LRU_WIDTH), 0.5),
        'state_lru_conv': nrm((DEPTH, DEC_BATCH, LRU_CONV - 1, LRU_WIDTH), 1.0),
        'page_table': jax.random.permutation(next(ks), n_pool)[:n_used].reshape(DEC_BATCH, n_pages).astype(jnp.int32),
        'g_mix': 1.0 + nrm((DEPTH, D_MODEL), 0.01),
        'w_in': nrm((DEPTH, D_MODEL, IN_COLS), D_MODEL ** -0.5),
        'w_out': nrm((DEPTH, MIX_WIDTH, D_MODEL), MIX_WIDTH ** -0.5),
        'rw_mu': uni((DEPTH, RW_COLS), 0.0, 1.0),
        'rw_w0': uni((DEPTH, RW_WIDTH), -6.0, 1.0),
        'rw_w2': nrm((DEPTH, RW_DECAY_LORA, RW_WIDTH), 0.1 * RW_DECAY_LORA ** -0.5),
        'rw_a0': nrm((DEPTH, RW_WIDTH), 0.1),
        'rw_a2': nrm((DEPTH, RW_AAA_LORA, RW_WIDTH), 0.5 * RW_AAA_LORA ** -0.5),
        'rw_g2': nrm((DEPTH, RW_GATE_LORA, RW_WIDTH), RW_GATE_LORA ** -0.5),
        'rw_kk': 0.85 + nrm((DEPTH, RW_WIDTH), 0.05),
        'rw_ka': 1.0 + nrm((DEPTH, RW_WIDTH), 0.05),
        'rw_rk': nrm((DEPTH, RW_HEADS, HEAD_DIM), 0.1),
        'rw_ln_w': 1.0 + nrm((DEPTH, RW_WIDTH), 0.01),
        'rw_ln_b': nrm((DEPTH, RW_WIDTH), 0.01),
        'lru_conv_w': nrm((DEPTH, LRU_CONV, LRU_WIDTH), 0.5),
        'lru_conv_b': nrm((DEPTH, LRU_WIDTH), 0.01),
        'lru_wa': nrm((DEPTH, LRU_BLOCKS, LRU_BLOCK_DIM, LRU_BLOCK_DIM), LRU_BLOCK_DIM ** -0.5),
        'lru_ba': nrm((DEPTH, LRU_WIDTH), 0.01),
        'lru_wx': nrm((DEPTH, LRU_BLOCKS, LRU_BLOCK_DIM, LRU_BLOCK_DIM), LRU_BLOCK_DIM ** -0.5),
        'lru_bx': nrm((DEPTH, LRU_WIDTH), 0.01),
        'lru_lambda': (lambda s: jnp.log(s) - jnp.log1p(-s))(uni((DEPTH, LRU_WIDTH), 0.9, 0.999) ** (1.0 / LRU_C)),
        'g_mem': 1.0 + nrm((DEPTH, D_MODEL), 0.01),
        'w_mq': nrm((DEPTH, D_MODEL, D_MODEL), D_MODEL ** -0.5),
        'w_mk': nrm((DEPTH, D_MODEL, D_MODEL), D_MODEL ** -0.5),
        'w_mv': nrm((DEPTH, D_MODEL, D_MODEL), D_MODEL ** -0.5),
        'w_mo': nrm((DEPTH, D_MODEL, D_MODEL), D_MODEL ** -0.5),
        'g_ffn': 1.0 + nrm((DEPTH, D_MODEL), 0.01),
        'w_gate': nrm((DEPTH, D_MODEL, D_FF), D_MODEL ** -0.5),
        'w_up': nrm((DEPTH, D_MODEL, D_FF), D_MODEL ** -0.5),
        'w_down': nrm((DEPTH, D_FF, D_MODEL), D_FF ** -0.5),
        'g_final': 1.0 + nrm((D_MODEL,), 0.01),
    }


def reference(x_prompt, x_sample, mem_prompt, cache_k, cache_v, cache_kidx, cache_mem_k, cache_mem_v,
              state_rwkv, state_rwkv_shift, state_lru_h, state_lru_conv, page_table,
              g_mix, w_in, w_out, rw_mu, rw_w0, rw_w2, rw_a0, rw_a2, rw_g2, rw_kk, rw_ka, rw_rk,
              rw_ln_w, rw_ln_b, lru_conv_w, lru_conv_b, lru_wa, lru_ba, lru_wx, lru_bx, lru_lambda,
              g_mem, w_mq, w_mk, w_mv, w_mo, g_ffn, w_gate, w_up, w_down, g_final):
    B, S, _ = x_prompt.shape
    DB, T, _ = x_sample.shape
    pos_p = jnp.arange(S)
    pos_s = PAST_LEN + jnp.arange(T)
    xp, xs = x_prompt, x_sample
    kp_l, vp_l, kip_l, mkp_l, mvp_l, rwp_l, shp_l, hp_l, cvp_l = [], [], [], [], [], [], [], [], []
    ks_l, vs_l, kis_l, rws_l, shs_l, hs_l, cvs_l = [], [], [], [], [], [], []
    for l in range(DEPTH):
        rw = (rw_mu[l], rw_w0[l], rw_w2[l], rw_a0[l], rw_a2[l], rw_g2[l], rw_kk[l], rw_ka[l], rw_rk[l],
              rw_ln_w[l], rw_ln_b[l])
        lru = (lru_conv_w[l], lru_conv_b[l], lru_wa[l], lru_ba[l], lru_wx[l], lru_bx[l], lru_lambda[l])
        h = rmsnorm(xp, g_mix[l])
        u_rw, sa, u_lru = mixer_inputs(h, w_in[l], pos_p)
        y_rw, sh, srw = rwkv7_mix(u_rw, jnp.zeros((B, RW_COLS), xp.dtype),
                                  jnp.zeros((B, RW_HEADS, HEAD_DIM, HEAD_DIM), jnp.float32), *rw)
        y_sa = dsa_prompt(*sa)
        y_lru, cv, hl = rglru_mix(u_lru, jnp.zeros((B, LRU_CONV - 1, LRU_WIDTH), xp.dtype),
                                  jnp.zeros((B, LRU_WIDTH), jnp.float32), *lru, reset_first=True)
        xp = xp + jnp.concatenate([y_rw, y_sa.reshape(B, S, SA_WIDTH), y_lru], axis=-1) @ w_out[l]
        mk = (mem_prompt @ w_mk[l]).reshape(B, MEM_TOKENS, MEM_HEADS, MEM_HEAD_DIM)
        mv = (mem_prompt @ w_mv[l]).reshape(B, MEM_TOKENS, MEM_HEADS, MEM_HEAD_DIM)
        xp = xp + mem_attend(rmsnorm(xp, g_mem[l]), mk, mv, w_mq[l], w_mo[l])
        xp = xp + swiglu(rmsnorm(xp, g_ffn[l]), w_gate[l], w_up[l], w_down[l])
        kp_l.append(sa[1]); vp_l.append(sa[2]); kip_l.append(sa[4]); mkp_l.append(mk); mvp_l.append(mv)
        rwp_l.append(srw); shp_l.append(sh); hp_l.append(hl); cvp_l.append(cv)
        h = rmsnorm(xs, g_mix[l])
        u_rw, sa, u_lru = mixer_inputs(h, w_in[l], pos_s)
        y_rw, sh, srw = rwkv7_mix(u_rw, state_rwkv_shift[l], state_rwkv[l], *rw)
        y_sa = dsa_sample(*sa, l, cache_k, cache_v, cache_kidx, page_table)
        y_lru, cv, hl = rglru_mix(u_lru, state_lru_conv[l], state_lru_h[l], *lru, reset_first=False)
        xs = xs + jnp.concatenate([y_rw, y_sa.reshape(DB, T, SA_WIDTH), y_lru], axis=-1) @ w_out[l]
        xs = xs + mem_attend(rmsnorm(xs, g_mem[l]), cache_mem_k[l], cache_mem_v[l], w_mq[l], w_mo[l])
        xs = xs + swiglu(rmsnorm(xs, g_ffn[l]), w_gate[l], w_up[l], w_down[l])
        ks_l.append(sa[1]); vs_l.append(sa[2]); kis_l.append(sa[4])
        rws_l.append(srw); shs_l.append(sh); hs_l.append(hl); cvs_l.append(cv)
    y_prompt = rmsnorm(xp, g_final)
    y_sample = rmsnorm(xs, g_final)
    new_k_prompt = jnp.stack(kp_l)
    new_v_prompt = jnp.stack(vp_l)
    new_kidx_prompt = jnp.stack(kip_l)
    new_mem_k_prompt = jnp.stack(mkp_l)
    new_mem_v_prompt = jnp.stack(mvp_l)
    new_rwkv_prompt = jnp.stack(rwp_l)
    new_rwkv_shift_prompt = jnp.stack(shp_l)
    new_lru_h_prompt = jnp.stack(hp_l)
    new_lru_conv_prompt = jnp.stack(cvp_l)
    new_k_sample = jnp.stack(ks_l)
    new_v_sample = jnp.stack(vs_l)
    new_kidx_sample = jnp.stack(kis_l)
    new_rwkv_sample = jnp.stack(rws_l)
    new_rwkv_shift_sample = jnp.stack(shs_l)
    new_lru_h_sample = jnp.stack(hs_l)
    new_lru_conv_sample = jnp.stack(cvs_l)
    return (y_prompt, y_sample, new_k_prompt, new_v_prompt, new_kidx_prompt, new_mem_k_prompt,
            new_mem_v_prompt, new_rwkv_prompt, new_rwkv_shift_prompt, new_lru_h_prompt, new_lru_conv_prompt,
            new_k_sample, new_v_sample, new_kidx_sample, new_rwkv_sample, new_rwkv_shift_sample,
            new_lru_h_sample, new_lru_conv_sample)
```

```python
import functools
import math

import jax
import jax.numpy as jnp
from jax import lax
from jax.experimental import pallas as pl
from jax.experimental.pallas import tpu as pltpu

F32 = jnp.float32
BF16 = jnp.bfloat16
I32 = jnp.int32
HIGHEST = lax.Precision.HIGHEST

D_MODEL = 1024
HEAD_DIM = 64
RW_HEADS = 4
RW_WIDTH = 256
RW_COLS = 1024
RW_GN_EPS = 64e-5
SA_HEADS = 8
SA_WIDTH = 512
IDX_SCALE = 512.0 ** -0.5
TOPK = 256
LRU_WIDTH = 256
LRU_C = 8.0
D_FF = 2816
MEM_HEADS = 4
MEM_HEAD_DIM = 256
ROPE_THETA = 10000.0
NORM_EPS = 1e-6
PAGE = 128
INT_MIN = -2 ** 31
NEG_BIG = -1e30
VMEM_LIMIT = 56 * 1024 * 1024
RW_CHUNK = 64


def _params(*sem):
    return pltpu.CompilerParams(dimension_semantics=sem, vmem_limit_bytes=VMEM_LIMIT)


def _full(shape):
    n = len(shape)
    return pl.BlockSpec(shape, lambda *_: (0,) * n)


def _dot(a, b):
    return jnp.dot(a, b, preferred_element_type=F32)


def _dot_nt(a, b):
    return lax.dot_general(a, b, (((1,), (1,)), ((), ())), preferred_element_type=F32)


def _dotH(a, b):
    return jnp.dot(a, b, preferred_element_type=F32, precision=HIGHEST)


def _dotH_nt(a, b):
    return lax.dot_general(a, b, (((1,), (1,)), ((), ())), preferred_element_type=F32, precision=HIGHEST)


def _dotH_tn(a, b):
    return lax.dot_general(a, b, (((0,), (0,)), ((), ())), preferred_element_type=F32, precision=HIGHEST)


def _rms(x, g):
    return x * lax.rsqrt(jnp.mean(x * x, axis=-1, keepdims=True) + NORM_EPS) * g


def _split_bf16(x):
    hi = x.astype(BF16)
    lo = (x - hi.astype(F32)).astype(BF16)
    return hi, lo


def _rope(u, cos, sin):
    w = u.shape[-1]
    lane = lax.broadcasted_iota(I32, u.shape, 1)
    rot = jnp.where((lane & 63) < 32, pltpu.roll(u, w - 32, 1), pltpu.roll(u, 32, 1))
    return u * cos + rot * sin


def _in_proj_kernel(x_ref, g_ref, wrw_ref, wq_ref, wk_ref, wv_ref, wqi_ref, wkw_ref, wlru_ref,
                    cos_ref, sin_ref, ckw_ref, skw_ref,
                    urw_ref, k_ref, v_ref, qi_ref, kw_ref, ulru_ref, qb_ref, kb_ref, vb_ref, ki4_ref):
    hb = _rms(x_ref[...], g_ref[...]).astype(BF16)
    urw_ref[...] = _dot(hb, wrw_ref[...])
    ulru_ref[...] = _dot(hb, wlru_ref[...])
    cos = jnp.concatenate([cos_ref[...]] * 4, axis=1)
    sin = jnp.concatenate([sin_ref[...]] * 4, axis=1)
    q = _rope(_dot(hb, wq_ref[...]), cos, sin)
    k = _rope(_dot(hb, wk_ref[...]), cos, sin)
    v = _dot(hb, wv_ref[...])
    qi_ref[...] = _rope(_dot(hb, wqi_ref[...]), cos, sin)
    kw = _rope(_dot(hb, wkw_ref[...]), ckw_ref[...], skw_ref[...])
    k_ref[...] = k
    v_ref[...] = v
    kw_ref[...] = kw
    qb_ref[...] = (q * (HEAD_DIM ** -0.5)).astype(BF16)
    kb_ref[...] = k.astype(BF16)
    vb_ref[...] = v.astype(BF16)
    hi, lo = _split_bf16(kw[:, :64])
    ki4_ref[...] = jnp.concatenate([hi, lo, hi, lo], axis=1)


def _in_proj(x, g, ws, tabs, tm):
    m = x.shape[0]
    wrw, wq, wk, wv, wqi, wkw, wlru = ws
    row = lambda n: pl.BlockSpec((tm, n), lambda i: (i, 0))
    out_shape = [
        jax.ShapeDtypeStruct((m, 1024), F32), jax.ShapeDtypeStruct((m, 512), F32),
        jax.ShapeDtypeStruct((m, 512), F32), jax.ShapeDtypeStruct((m, 512), F32),
        jax.ShapeDtypeStruct((m, 128), F32), jax.ShapeDtypeStruct((m, 512), F32),
        jax.ShapeDtypeStruct((m, 512), BF16), jax.ShapeDtypeStruct((m, 512), BF16),
        jax.ShapeDtypeStruct((m, 512), BF16), jax.ShapeDtypeStruct((m, 256), BF16),
    ]
    return pl.pallas_call(
        _in_proj_kernel,
        grid=(m // tm,),
        in_specs=[row(1024), _full((1, 1024))] + [_full(w.shape) for w in ws] + [row(128)] * 4,
        out_specs=[row(1024), row(512), row(512), row(512), row(128), row(512), row(512), row(512), row(512),
                   row(256)],
        out_shape=out_shape,
        compiler_params=_params("parallel"),
        name="in_proj",
    )(x, g, wrw, wq, wk, wv, wqi, wkw, wlru, *tabs)


def _rope_tables(pos):
    half = HEAD_DIM // 2
    inv = ROPE_THETA ** (-jnp.arange(half, dtype=F32) / half)
    ang = pos.astype(F32)[:, None] * inv[None, :]
    c, s = jnp.cos(ang), jnp.sin(ang)
    m = pos.shape[0]
    cos = jnp.concatenate([c, c, c, c], axis=1)
    sin = jnp.concatenate([-s, s, -s, s], axis=1)
    ckw = jnp.concatenate([c, c, jnp.full((m, SA_HEADS), IDX_SCALE, F32), jnp.zeros((m, 56), F32)], axis=1)
    skw = jnp.concatenate([-s, s, jnp.zeros((m, 64), F32)], axis=1)
    return cos, sin, ckw, skw


def _rwkv_kernel(u_ref, prev_ref, s0_ref, mu_ref, w0_ref, w2_ref, a0_ref, a2_ref, g2_ref, kk_ref, ka_ref,
                 rk_ref, lnw_ref, lnb_ref, y_ref, sout_ref,
                 sbd, carry, r_s, kn_s, a_s, k2_s, v_s, lw_s, y_s, *, tb_rows, n_valid):
    c_len = RW_CHUNK
    tb = pl.program_id(1)

    @pl.when(tb == 0)
    def _():
        sbd[...] = s0_ref[0]
        carry[...] = prev_ref[0]

    u = u_ref[0]
    row = lax.broadcasted_iota(I32, u.shape, 0)
    u_prev = jnp.where(row == 0, carry[...], pltpu.roll(u, 1, 0))
    last = min(n_valid, tb_rows) - 1
    carry[...] = u[last:last + 1, :]
    us = u + mu_ref[...] * (u_prev - u)
    r, k, v = us[:, 0:256], us[:, 256:512], us[:, 512:768]
    wa, gd = us[:, 768:896], us[:, 896:1024]

    li = lax.broadcasted_iota(I32, (256, 256), 0) // 64
    lj = lax.broadcasted_iota(I32, (256, 256), 1) // 64
    blockdiag = (li == lj).astype(F32)

    xw = w0_ref[...] + _dotH(jnp.tanh(wa), w2_ref[...])
    logw = -math.exp(-0.5) * jax.nn.sigmoid(xw)
    a = jax.nn.sigmoid(a0_ref[...] + _dotH(wa, a2_ref[...]))
    g = _dotH(jax.nn.sigmoid(gd), g2_ref[...])
    kkv = k * kk_ref[...]
    kn = kkv / jnp.maximum(jnp.sqrt(_dotH(kkv * kkv, blockdiag)), 1e-12)
    k2 = k * (1.0 + (a - 1.0) * ka_ref[...])
    bonus = _dotH(r * k2 * rk_ref[...], blockdiag) * v
    if n_valid < tb_rows:
        ok = lax.broadcasted_iota(I32, (tb_rows, 256), 0) < n_valid
        logw = jnp.where(ok, logw, 0.0)
        kn = jnp.where(ok, kn, 0.0)
        k2 = jnp.where(ok, k2, 0.0)
        v = jnp.where(ok, v, 0.0)
    r_s[...] = r
    kn_s[...] = kn
    a_s[...] = a
    k2_s[...] = k2
    v_s[...] = v
    lw_s[...] = logw

    lane = lax.broadcasted_iota(I32, (1, 256), 1) // 64
    hmask = [(lane == h).astype(F32) for h in range(RW_HEADS)]
    ci = lax.broadcasted_iota(I32, (c_len, c_len), 0)
    cj = lax.broadcasted_iota(I32, (c_len, c_len), 1)
    tri_incl = (ci >= cj).astype(F32)
    eye = (ci == cj).astype(F32)
    mi = lax.broadcasted_iota(I32, (8 * c_len, 2 * c_len), 0)
    mj = lax.broadcasted_iota(I32, (8 * c_len, 2 * c_len), 1)
    mt, ms = mi & (c_len - 1), mj & (c_len - 1)
    keep = ms < mt + jnp.where(mi < 4 * c_len, 0, 1)
    pick = lambda x: sum(x[h * c_len:(h + 1) * c_len] * hmask[h] for h in range(RW_HEADS))
    n_sq = int(math.log2(c_len)) - 1

    def chunk(c, _):
        sl = pl.ds(pl.multiple_of(c * c_len, c_len), c_len)
        lw, rc, knc, ac, k2c, vc = lw_s[sl, :], r_s[sl, :], kn_s[sl, :], a_s[sl, :], k2_s[sl, :], v_s[sl, :]
        lcum = _dotH(tri_incl, lw)
        gam, gex, gin = jnp.exp(lcum), jnp.exp(lcum - lw), jnp.exp(-lcum)
        rt, at, bt, kt = rc * gam, knc * gex, -(knc * ac) * gin, k2c * gin
        g_end = gam[c_len - 1:c_len, :]
        lhs = jnp.concatenate([at * hm for hm in hmask] + [rt * hm for hm in hmask], axis=0)
        m = jnp.where(keep, _dotH_nt(lhs, jnp.concatenate([bt, kt], axis=0)), 0.0)
        s = sbd[...]
        asrs = _dotH_nt(jnp.concatenate([at, rt], axis=0), s)
        w1 = _dotH(m[:4 * c_len], jnp.concatenate([jnp.zeros_like(vc), vc], axis=0))
        wm = asrs[:c_len] + pick(w1)
        inv = []
        for h in range(RW_HEADS):
            q = m[h * c_len:(h + 1) * c_len, 0:c_len]
            x = eye + q
            for _ in range(n_sq):
                q = _dotH(q, q)
                x = x + _dotH(x, q)
            inv.append(x)
        p = pick(_dotH(jnp.concatenate(inv, axis=0), wm))
        pv = jnp.concatenate([p, vc], axis=0)
        y_s[sl, :] = asrs[c_len:] + pick(_dotH(m[4 * c_len:], pv))
        upd = _dotH_tn(pv, jnp.concatenate([bt * g_end, kt * g_end], axis=0))
        sbd[...] = s * g_end + upd * blockdiag
        return 0

    lax.fori_loop(0, tb_rows // c_len, chunk, 0)

    y = y_s[...]
    mean = _dotH(y, blockdiag) * (1.0 / HEAD_DIM)
    yc = y - mean
    var = _dotH(yc * yc, blockdiag) * (1.0 / HEAD_DIM)
    yn = yc * lax.rsqrt(var + RW_GN_EPS) * lnw_ref[...] + lnb_ref[...]
    y_ref[0] = (yn + bonus) * g
    sout_ref[0] = sbd[...]


def _rwkv(u, prev, s0bd, prm, tb_rows, n_valid):
    b, t, _ = u.shape
    vec = lambda n: _full((1, n))
    kern = functools.partial(_rwkv_kernel, tb_rows=tb_rows, n_valid=n_valid)
    return pl.pallas_call(
        kern,
        grid=(b, t // tb_rows),
        in_specs=[pl.BlockSpec((1, tb_rows, 1024), lambda i, j: (i, j, 0)),
                  pl.BlockSpec((1, 1, 1024), lambda i, j: (i, 0, 0)),
                  pl.BlockSpec((1, 256, 256), lambda i, j: (i, 0, 0)),
                  vec(1024), vec(256), _full((128, 256)), vec(256), _full((128, 256)), _full((128, 256)),
                  vec(256), vec(256), vec(256), vec(256), vec(256)],
        out_specs=[pl.BlockSpec((1, tb_rows, 256), lambda i, j: (i, j, 0)),
                   pl.BlockSpec((1, 256, 256), lambda i, j: (i, 0, 0))],
        out_shape=[jax.ShapeDtypeStruct((b, t, 256), F32), jax.ShapeDtypeStruct((b, 256, 256), F32)],
        scratch_shapes=[pltpu.VMEM((256, 256), F32), pltpu.VMEM((1, 1024), F32)]
        + [pltpu.VMEM((tb_rows, 256), F32)] * 7,
        compiler_params=_params("parallel", "arbitrary"),
        name="rwkv7",
    )(u, prev, s0bd, *prm)


def _lru_kernel(u_ref, cb_ref, h0_ref, cw_ref, cbias_ref, wa_ref, ba_ref, wx_ref, bx_ref, lam_ref,
                out_ref, hl_ref, ext, hc, *, tb_rows, reset_first):
    tb = pl.program_id(1)

    @pl.when(tb == 0)
    def _():
        ext[0:8, :] = jnp.zeros((8, 256), F32)
        ext[5:8, :] = cb_ref[0]
        hc[...] = h0_ref[0]

    u = u_ref[0]
    xb, gate = u[:, :256], u[:, 256:]
    ext[8:8 + tb_rows, :] = xb
    cw = cw_ref[...]
    y = (cbias_ref[...] + cw[0:1] * ext[5:5 + tb_rows, :] + cw[1:2] * ext[6:6 + tb_rows, :]
         + cw[2:3] * ext[7:7 + tb_rows, :] + cw[3:4] * xb)
    ext[0:8, :] = ext[tb_rows:tb_rows + 8, :]
    gate_r = jax.nn.sigmoid(_dotH(y, wa_ref[...]) + ba_ref[...])
    gate_i = jax.nn.sigmoid(_dotH(y, wx_ref[...]) + bx_ref[...])
    lam = lam_ref[...]
    softplus_neg = jnp.maximum(-lam, 0.0) + jnp.log(1.0 + jnp.exp(-jnp.abs(lam)))
    log_a = (-LRU_C * softplus_neg) * gate_r
    a = jnp.exp(log_a)
    mult = jnp.sqrt(1.0 - jnp.exp(2.0 * log_a))
    row = lax.broadcasted_iota(I32, (tb_rows, 256), 0)
    if reset_first:
        mult = jnp.where((row == 0) & (tb == 0), 1.0, mult)
    bv = mult * gate_i * y
    d = 1
    while d < tb_rows:
        live = row >= d
        a_sh = jnp.where(live, pltpu.roll(a, d, 0), 1.0)
        b_sh = jnp.where(live, pltpu.roll(bv, d, 0), 0.0)
        bv = a * b_sh + bv
        a = a * a_sh
        d *= 2
    h = a * hc[...] + bv
    hc[...] = h[tb_rows - 1:tb_rows, :]
    gelu = 0.5 * gate * (1.0 + jnp.tanh(math.sqrt(2.0 / math.pi) * (gate + 0.044715 * gate * gate * gate)))
    out_ref[0] = h * gelu
    hl_ref[0] = h[tb_rows - 1:tb_rows, :]


def _lru(u, conv_buf, h0, prm, tb_rows, reset_first):
    b, t, _ = u.shape
    vec = _full((1, 256))
    kern = functools.partial(_lru_kernel, tb_rows=tb_rows, reset_first=reset_first)
    return pl.pallas_call(
        kern,
        grid=(b, t // tb_rows),
        in_specs=[pl.BlockSpec((1, tb_rows, 512), lambda i, j: (i, j, 0)),
                  pl.BlockSpec((1, 3, 256), lambda i, j: (i, 0, 0)),
                  pl.BlockSpec((1, 1, 256), lambda i, j: (i, 0, 0)),
                  _full((4, 256)), vec, _full((256, 256)), vec, _full((256, 256)), vec, vec],
        out_specs=[pl.BlockSpec((1, tb_rows, 256), lambda i, j: (i, j, 0)),
                   pl.BlockSpec((1, 1, 256), lambda i, j: (i, 0, 0))],
        out_shape=[jax.ShapeDtypeStruct((b, t, 256), F32), jax.ShapeDtypeStruct((b, 1, 256), F32)],
        scratch_shapes=[pltpu.VMEM((tb_rows + 8, 256), F32), pltpu.VMEM((1, 256), F32)],
        compiler_params=_params("parallel", "arbitrary"),
        name="rglru",
    )(u, conv_buf, h0, *prm)


def _sortable(x):
    bits = pltpu.bitcast(x, I32)
    return jnp.where(bits < 0, bits ^ 0x7FFFFFFF, bits)


def _bisect_threshold(count_gt, shape, topk):
    def body(it, t):
        cand = t + lax.shift_left(jnp.int32(1), 31 - it)
        return jnp.where(count_gt(cand) >= topk, cand, t)

    return lax.fori_loop(0, 32, body, jnp.full(shape, INT_MIN, I32))


def _dsa_prompt_kernel(qb_ref, qi_ref, kw_ref, ki4_ref, kb_ref, vb_ref, o_ref,
                       lhs_s, wcol_s, key_s, thr_s, qm_s, m_s, l_s, acc_s, *, qb_rows, kb_rows, sk, topk):
    i, ph, j = pl.program_id(0), pl.program_id(1), pl.program_id(2)
    nv = ((i + 1) * qb_rows + kb_rows - 1) // kb_rows
    valid = j < nv

    @pl.when((ph == 0) & (j == 0))
    def _init():
        qi, kw, qb = qi_ref[...], kw_ref[...], qb_ref[...]
        lane = lax.broadcasted_iota(I32, (qb_rows, 128), 1)
        for h in range(SA_HEADS):
            hi, lo = _split_bf16(qi[:, h * 64:(h + 1) * 64])
            lhs_s[h * qb_rows:(h + 1) * qb_rows, :] = jnp.concatenate([hi, hi, lo, lo], axis=1)
            wcol_s[h] = kw[:, 64 + h:65 + h]
            own = (lane < 64) if h % 2 == 0 else (lane >= 64)
            qp = qb[:, (h // 2) * 128:(h // 2 + 1) * 128]
            qm_s[h] = jnp.where(own, qp, jnp.zeros_like(qp))
        m_s[...] = jnp.full(m_s.shape, NEG_BIG, F32)
        l_s[...] = jnp.zeros(l_s.shape, F32)
        acc_s[...] = jnp.zeros(acc_s.shape, F32)

    @pl.when((ph == 0) & valid)
    def _index():
        qpos = i * qb_rows + lax.broadcasted_iota(I32, (qb_rows, 256), 0)
        for sub in range(kb_rows // 256):
            z = _dot_nt(lhs_s[...], ki4_ref[sub * 256:(sub + 1) * 256, :])
            sc = wcol_s[0] * jnp.maximum(z[0:qb_rows], 0.0)
            for h in range(1, SA_HEADS):
                sc = sc + wcol_s[h] * jnp.maximum(z[h * qb_rows:(h + 1) * qb_rows], 0.0)
            kpos = j * kb_rows + sub * 256 + lax.broadcasted_iota(I32, (qb_rows, 256), 1)
            key_s[j, :, sub * 256:(sub + 1) * 256] = jnp.where(kpos <= qpos, _sortable(sc), INT_MIN)

    @pl.when((ph == 1) & (j == 0))
    def _threshold():
        def count_gt(t):
            def body(jb, acc):
                x = jnp.where(key_s[jb] > t, 1, 0)
                part = x[:, 0:128]
                for c in range(1, kb_rows // 128):
                    part = part + x[:, c * 128:(c + 1) * 128]
                return acc + part
            acc = lax.fori_loop(0, nv, body, jnp.zeros((qb_rows, 128), I32))
            return jnp.sum(acc, axis=1, keepdims=True)

        thr_s[...] = _bisect_threshold(count_gt, (qb_rows, 1), topk)

    @pl.when((ph == 1) & valid)
    def _attend():
        t = thr_s[...]
        for sub in range(kb_rows // sk):
            bias = jnp.where(key_s[j, :, sub * sk:(sub + 1) * sk] > t, 0.0, NEG_BIG)
            for p in range(SA_HEADS // 2):
                kp = kb_ref[sub * sk:(sub + 1) * sk, p * 128:(p + 1) * 128]
                vp = vb_ref[sub * sk:(sub + 1) * sk, p * 128:(p + 1) * 128]
                for h in (2 * p, 2 * p + 1):
                    s = _dot_nt(qm_s[h], kp) + bias
                    m_old = m_s[h]
                    m_new = jnp.maximum(m_old, jnp.max(s, axis=1, keepdims=True))
                    alpha = jnp.exp(m_old - m_new)
                    pr = jnp.exp(s - m_new)
                    l_s[h] = alpha * l_s[h] + jnp.sum(pr, axis=1, keepdims=True)
                    acc_s[h] = alpha * acc_s[h] + _dot(pr.astype(BF16), vp)
                    m_s[h] = m_new

    @pl.when((ph == 1) & (j == nv - 1))
    def _finish():
        lane = lax.broadcasted_iota(I32, (qb_rows, 128), 1)
        for p in range(SA_HEADS // 2):
            o0 = acc_s[2 * p] / l_s[2 * p]
            o1 = acc_s[2 * p + 1] / l_s[2 * p + 1]
            o_ref[:, p * 128:(p + 1) * 128] = jnp.where(lane < 64, o0, o1)


def _dsa_prompt(qb, qi, kw, ki4, kb, vb, qb_rows, kb_rows):
    s = qb.shape[0]
    nq, nk = s // qb_rows, s // kb_rows
    sk = min(512, kb_rows)

    def kblock(i, ph, j):
        nv = ((i + 1) * qb_rows + kb_rows - 1) // kb_rows
        return jnp.minimum(j, nv - 1)

    qrow = lambda n: pl.BlockSpec((qb_rows, n), lambda i, ph, j: (i, 0))
    kern = functools.partial(_dsa_prompt_kernel, qb_rows=qb_rows, kb_rows=kb_rows, sk=sk, topk=min(TOPK, s // 4))
    return pl.pallas_call(
        kern,
        grid=(nq, 2, nk),
        in_specs=[qrow(512), qrow(512), qrow(128),
                  pl.BlockSpec((kb_rows, 256), lambda i, ph, j: (jnp.where(ph == 0, kblock(i, ph, j), 0), 0)),
                  pl.BlockSpec((kb_rows, 512), lambda i, ph, j: (jnp.where(ph == 1, kblock(i, ph, j), 0), 0)),
                  pl.BlockSpec((kb_rows, 512), lambda i, ph, j: (jnp.where(ph == 1, kblock(i, ph, j), 0), 0))],
        out_specs=qrow(512),
        out_shape=jax.ShapeDtypeStruct((s, 512), F32),
        scratch_shapes=[pltpu.VMEM((SA_HEADS * qb_rows, 256), BF16),
                        pltpu.VMEM((SA_HEADS, qb_rows, 1), F32),
                        pltpu.VMEM((nk, qb_rows, kb_rows), I32),
                        pltpu.VMEM((qb_rows, 1), I32),
                        pltpu.VMEM((SA_HEADS, qb_rows, 128), BF16),
                        pltpu.VMEM((SA_HEADS, qb_rows, 1), F32),
                        pltpu.VMEM((SA_HEADS, qb_rows, 1), F32),
                        pltpu.VMEM((SA_HEADS, qb_rows, 128), F32)],
        compiler_params=_params("arbitrary", "arbitrary", "arbitrary"),
        name="dsa_prompt",
    )(qb, qi, kw, ki4, kb, vb)


PAGES_PER_STEP = 8


def _dsa_sample_index_kernel(pt_ref, qi_ref, kw_ref, *rest, n_pages, t_len, topk):
    pages = rest[:PAGES_PER_STEP]
    key_ref, thr_ref, lhs_s, wcol_s = rest[PAGES_PER_STEP:]
    j = pl.program_id(1)
    n_steps = n_pages // PAGES_PER_STEP

    @pl.when(j == 0)
    def _init():
        qi, kw = qi_ref[0], kw_ref[0]
        for h in range(SA_HEADS):
            lhs_s[h * t_len:(h + 1) * t_len, :] = qi[:, h * 64:(h + 1) * 64]
            wcol_s[h] = kw[:, 64 + h:65 + h]

    def scores(ki):
        z = _dotH_nt(lhs_s[...], ki)
        sc = wcol_s[0] * jnp.maximum(z[0:t_len], 0.0)
        for h in range(1, SA_HEADS):
            sc = sc + wcol_s[h] * jnp.maximum(z[h * t_len:(h + 1) * t_len], 0.0)
        return sc

    for r in range(PAGES_PER_STEP):
        key_ref[0, j * PAGES_PER_STEP + r] = _sortable(scores(pages[r][...]))

    @pl.when(j == n_steps - 1)
    def _finish():
        ki_new = jnp.concatenate([kw_ref[0][:, :64], jnp.zeros((128 - t_len, 64), F32)], axis=0)
        sc = scores(ki_new)
        tq = lax.broadcasted_iota(I32, (t_len, 128), 0)
        ts = lax.broadcasted_iota(I32, (t_len, 128), 1)
        key_ref[0, n_pages] = jnp.where(ts <= tq, _sortable(sc), INT_MIN)

        def count_gt(t):
            x = jnp.where(key_ref[0] > t[None], 1, 0)
            return jnp.sum(jnp.sum(x, axis=0), axis=1, keepdims=True)

        thr = _bisect_threshold(count_gt, (t_len, 1), topk)
        thr_ref[0] = jnp.broadcast_to(thr, (t_len, 128))


def _dsa_sample_index(page_table, qi, kw, cache_kidx_l, t_len):
    b = qi.shape[0]
    n_pages = page_table.shape[1]
    n_steps = n_pages // PAGES_PER_STEP

    def page_spec(r):
        return pl.BlockSpec((None, PAGE, 64),
                            lambda i, j, pt: (pt[i * n_pages + j * PAGES_PER_STEP + r], 0, 0))

    kern = functools.partial(_dsa_sample_index_kernel, n_pages=n_pages, t_len=t_len,
                             topk=min(TOPK, (n_pages * PAGE + t_len) // 4))
    grid_spec = pltpu.PrefetchScalarGridSpec(
        num_scalar_prefetch=1,
        grid=(b, n_steps),
        in_specs=[pl.BlockSpec((1, t_len, 512), lambda i, j, pt: (i, 0, 0)),
                  pl.BlockSpec((1, t_len, 128), lambda i, j, pt: (i, 0, 0))]
        + [page_spec(r) for r in range(PAGES_PER_STEP)],
        out_specs=[pl.BlockSpec((1, n_pages + 1, t_len, 128), lambda i, j, pt: (i, 0, 0, 0)),
                   pl.BlockSpec((1, t_len, 128), lambda i, j, pt: (i, 0, 0))],
        scratch_shapes=[pltpu.VMEM((SA_HEADS * t_len, 64), F32), pltpu.VMEM((SA_HEADS, t_len, 1), F32)],
    )
    return pl.pallas_call(
        kern,
        grid_spec=grid_spec,
        out_shape=[jax.ShapeDtypeStruct((b, n_pages + 1, t_len, 128), I32),
                   jax.ShapeDtypeStruct((b, t_len, 128), I32)],
        compiler_params=_params("parallel", "arbitrary"),
        name="dsa_sample_index",
    )(page_table.reshape(-1), qi, kw, *([cache_kidx_l] * PAGES_PER_STEP))


def _dsa_sample_attend_kernel(pt_ref, qb_ref, key_ref, keyn_ref, thr_ref, kn_ref, vn_ref, *rest, n_pages, t_len):
    kpages = rest[:PAGES_PER_STEP]
    vpages = rest[PAGES_PER_STEP:2 * PAGES_PER_STEP]
    o_ref, qbd_s, m_s, l_s, acc_s = rest[2 * PAGES_PER_STEP:]
    j = pl.program_id(1)
    n_steps = n_pages // PAGES_PER_STEP
    rows = SA_HEADS * t_len

    @pl.when(j == 0)
    def _init():
        qb = qb_ref[0]
        lane = lax.broadcasted_iota(I32, (t_len, 512), 1) // 64
        for h in range(SA_HEADS):
            qbd_s[h * t_len:(h + 1) * t_len, :] = jnp.where(lane == h, qb, jnp.zeros_like(qb))
        m_s[...] = jnp.full(m_s.shape, NEG_BIG, F32)
        l_s[...] = jnp.zeros(l_s.shape, F32)
        acc_s[...] = jnp.zeros(acc_s.shape, F32)

    thr = thr_ref[0]

    def update(keys, kpg, vpg):
        bias = jnp.where(keys > thr, 0.0, NEG_BIG)
        s = _dot_nt(qbd_s[...], kpg.astype(BF16)) + jnp.concatenate([bias] * SA_HEADS, axis=0)
        m_old = m_s[...]
        m_new = jnp.maximum(m_old, jnp.max(s, axis=1, keepdims=True))
        alpha = jnp.exp(m_old - m_new)
        pr = jnp.exp(s - m_new)
        l_s[...] = alpha * l_s[...] + jnp.sum(pr, axis=1, keepdims=True)
        acc_s[...] = alpha * acc_s[...] + _dot(pr.astype(BF16), vpg.astype(BF16))
        m_s[...] = m_new

    for r in range(PAGES_PER_STEP):
        update(key_ref[0, r], kpages[r][...], vpages[r][...])

    @pl.when(j == n_steps - 1)
    def _finish():
        update(keyn_ref[0, 0], kn_ref[0], vn_ref[0])
        o = acc_s[...] / l_s[...]
        lane = lax.broadcasted_iota(I32, (t_len, 512), 1) // 64
        out = jnp.zeros((t_len, 512), F32)
        for h in range(SA_HEADS):
            out = out + jnp.where(lane == h, o[h * t_len:(h + 1) * t_len], 0.0)
        o_ref[0] = out


def _dsa_sample_attend(page_table, qb, keys, thr, k_new, v_new, cache_k_l, cache_v_l, t_len):
    b = qb.shape[0]
    n_pages = page_table.shape[1]
    n_steps = n_pages // PAGES_PER_STEP
    rows = SA_HEADS * t_len

    def page_spec(r):
        return pl.BlockSpec((None, PAGE, 512),
                            lambda i, j, pt: (pt[i * n_pages + j * PAGES_PER_STEP + r], 0, 0))

    per_b = lambda n: pl.BlockSpec((1, t_len, n), lambda i, j, pt: (i, 0, 0))
    kern = functools.partial(_dsa_sample_attend_kernel, n_pages=n_pages, t_len=t_len)
    grid_spec = pltpu.PrefetchScalarGridSpec(
        num_scalar_prefetch=1,
        grid=(b, n_steps),
        in_specs=[per_b(512),
                  pl.BlockSpec((1, PAGES_PER_STEP, t_len, 128), lambda i, j, pt: (i, j, 0, 0)),
                  pl.BlockSpec((1, 1, t_len, 128), lambda i, j, pt: (i, n_pages, 0, 0)),
                  per_b(128),
                  pl.BlockSpec((1, PAGE, 512), lambda i, j, pt: (i, 0, 0)),
                  pl.BlockSpec((1, PAGE, 512), lambda i, j, pt: (i, 0, 0))]
        + [page_spec(r) for r in range(PAGES_PER_STEP)] * 2,
        out_specs=per_b(512),
        scratch_shapes=[pltpu.VMEM((rows, 512), BF16), pltpu.VMEM((rows, 1), F32), pltpu.VMEM((rows, 1), F32),
                        pltpu.VMEM((rows, 512), F32)],
    )
    return pl.pallas_call(
        kern,
        grid_spec=grid_spec,
        out_shape=jax.ShapeDtypeStruct((b, t_len, 512), F32),
        compiler_params=_params("parallel", "arbitrary"),
        name="dsa_sample_attend",
    )(page_table.reshape(-1), qb, keys, keys, thr, k_new, v_new,
      *([cache_k_l] * PAGES_PER_STEP), *([cache_v_l] * PAGES_PER_STEP))


def _matmul_kernel(x_ref, w_ref, o_ref):
    o_ref[...] = _dot(x_ref[...].astype(BF16), w_ref[...])


def _matmul(x, w):
    m, n = x.shape[0], w.shape[1]
    return pl.pallas_call(
        _matmul_kernel,
        grid=(1,),
        in_specs=[_full(x.shape), _full(w.shape)],
        out_specs=_full((m, n)),
        out_shape=jax.ShapeDtypeStruct((m, n), F32),
        compiler_params=_params("arbitrary"),
        name="mem_kv_proj",
    )(x, w)


def _post1_kernel(x_ref, yrw_ref, ysa_ref, ylru_ref, wo1_ref, wo2_ref, wo3_ref, g_ref, wq_ref, x1_ref, qm_ref):
    x1 = (x_ref[...] + _dot(yrw_ref[...].astype(BF16), wo1_ref[...])
          + _dot(ysa_ref[...].astype(BF16), wo2_ref[...]) + _dot(ylru_ref[...].astype(BF16), wo3_ref[...]))
    x1_ref[...] = x1
    qm = _dot(_rms(x1, g_ref[...]).astype(BF16), wq_ref[...])
    qm_ref[...] = (qm * (MEM_HEAD_DIM ** -0.5)).astype(BF16)


def _post1(x, yrw, ysa, ylru, wo1, wo2, wo3, g, wq, tm):
    m = x.shape[0]
    row = lambda n: pl.BlockSpec((tm, n), lambda i: (i, 0))
    return pl.pallas_call(
        _post1_kernel,
        grid=(m // tm,),
        in_specs=[row(1024), row(256), row(512), row(256), _full(wo1.shape), _full(wo2.shape), _full(wo3.shape),
                  _full((1, 1024)), _full(wq.shape)],
        out_specs=[row(1024), row(1024)],
        out_shape=[jax.ShapeDtypeStruct((m, 1024), F32), jax.ShapeDtypeStruct((m, 1024), BF16)],
        compiler_params=_params("parallel"),
        name="out_proj_memq",
    )(x, yrw, ysa, ylru, wo1, wo2, wo3, g, wq)


def _mem_attn_kernel(q_ref, mk_ref, mv_ref, o_ref):
    q = q_ref[0]
    mk = mk_ref[0].astype(BF16)
    mv = mv_ref[0].astype(BF16)
    for h in range(MEM_HEADS):
        sl = slice(h * MEM_HEAD_DIM, (h + 1) * MEM_HEAD_DIM)
        s = _dot_nt(q[:, sl], mk[:, sl])
        p = jnp.exp(s - jnp.max(s, axis=1, keepdims=True))
        o = _dot(p.astype(BF16), mv[:, sl]) / jnp.sum(p, axis=1, keepdims=True)
        o_ref[0, :, sl] = o.astype(BF16)


def _mem_attn(q, mk, mv, tm):
    b, t, _ = q.shape
    return pl.pallas_call(
        _mem_attn_kernel,
        grid=(b, t // tm),
        in_specs=[pl.BlockSpec((1, tm, 1024), lambda i, j: (i, j, 0)),
                  pl.BlockSpec((1, 256, 1024), lambda i, j: (i, 0, 0)),
                  pl.BlockSpec((1, 256, 1024), lambda i, j: (i, 0, 0))],
        out_specs=pl.BlockSpec((1, tm, 1024), lambda i, j: (i, j, 0)),
        out_shape=jax.ShapeDtypeStruct((b, t, 1024), BF16),
        compiler_params=_params("parallel", "parallel"),
        name="mem_attn",
    )(q, mk, mv)


def _post2_kernel(x1_ref, o_ref, wo_ref, g_ref, wg_ref, wu_ref, wd_ref, gf_ref, x3_ref, yf_ref):
    x2 = x1_ref[...] + _dot(o_ref[...], wo_ref[...])
    hb = _rms(x2, g_ref[...]).astype(BF16)
    gt = _dot(hb, wg_ref[...])
    up = _dot(hb, wu_ref[...])
    act = (gt * jax.nn.sigmoid(gt) * up).astype(BF16)
    x3 = x2 + _dot(act, wd_ref[...])
    x3_ref[...] = x3
    yf_ref[...] = _rms(x3, gf_ref[...])


def _post2(x1, o, wo, g, wg, wu, wd, gf, tm):
    m = x1.shape[0]
    row = lambda n: pl.BlockSpec((tm, n), lambda i: (i, 0))
    once = lambda w: pl.BlockSpec(w.shape, lambda i: (0, 0), pipeline_mode=pl.Buffered(1))
    return pl.pallas_call(
        _post2_kernel,
        grid=(m // tm,),
        in_specs=[row(1024), row(1024), once(wo), _full((1, 1024)), once(wg), once(wu), once(wd), _full((1, 1024))],
        out_specs=[row(1024), row(1024)],
        out_shape=[jax.ShapeDtypeStruct((m, 1024), F32), jax.ShapeDtypeStruct((m, 1024), F32)],
        compiler_params=_params("parallel"),
        name="memo_swiglu",
    )(x1, o, wo, g, wg, wu, wd, gf)


def _blockdiag4(w):
    n = w.shape[-1]
    return jnp.einsum("gij,gh->gihj", w, jnp.eye(4, dtype=w.dtype)).reshape(4 * n, 4 * n)


def _state_to_blockdiag(s):
    b = s.shape[0]
    return jnp.einsum("bhvk,hg->bhvgk", s, jnp.eye(4, dtype=s.dtype)).reshape(b, 256, 256)


def _blockdiag_to_state(sbd):
    b = sbd.shape[0]
    return jnp.einsum("bhvgk,hg->bhvk", sbd.reshape(b, 4, 64, 4, 64), jnp.eye(4, dtype=sbd.dtype))


def _layer_weights(l, w_in, w_out, rw, lru, g_mix, g_mem, w_mq, w_mk, w_mv, w_mo, g_ffn, w_gate, w_up, w_down):
    wi = w_in[l]
    o = RW_COLS
    cut = lambda a, n: wi[:, a:a + n].astype(BF16)
    wkw = jnp.pad(wi[:, o + 2048:o + 2120], ((0, 0), (0, 56))).astype(BF16)
    in_ws = (cut(0, 1024), cut(o, 512), cut(o + 512, 512), cut(o + 1024, 512), cut(o + 1536, 512), wkw,
             cut(o + 2120, 512))
    (rw_mu, rw_w0, rw_w2, rw_a0, rw_a2, rw_g2, rw_kk, rw_ka, rw_rk, rw_ln_w, rw_ln_b) = rw
    z = jnp.zeros((64, 256), F32)
    v256 = lambda a: a[l].reshape(1, 256)
    rw_prm = (rw_mu[l].reshape(1, 1024), v256(rw_w0), jnp.concatenate([rw_w2[l], z], axis=0), v256(rw_a0),
              jnp.concatenate([z, rw_a2[l]], axis=0), rw_g2[l], v256(rw_kk), v256(rw_ka), v256(rw_rk),
              v256(rw_ln_w), v256(rw_ln_b))
    (lru_conv_w, lru_conv_b, lru_wa, lru_ba, lru_wx, lru_bx, lru_lambda) = lru
    lru_prm = (lru_conv_w[l], v256(lru_conv_b), _blockdiag4(lru_wa[l]), v256(lru_ba), _blockdiag4(lru_wx[l]),
               v256(lru_bx), v256(lru_lambda))
    wo = w_out[l].astype(BF16)
    return dict(
        in_ws=in_ws, rw=rw_prm, lru=lru_prm, g_mix=g_mix[l].reshape(1, 1024),
        wo=(wo[:256], wo[256:768], wo[768:]), g_mem=g_mem[l].reshape(1, 1024), wq=w_mq[l].astype(BF16),
        wmk=w_mk[l].astype(BF16), wmv=w_mv[l].astype(BF16), wmo=w_mo[l].astype(BF16),
        g_ffn=g_ffn[l].reshape(1, 1024), wg=w_gate[l].astype(BF16), wu=w_up[l].astype(BF16),
        wd=w_down[l].astype(BF16))


def _pick_tile(n, pref):
    t = min(n, pref)
    while n % t:
        t //= 2
    return t


def _prompt_layer(x, mem, w, tabs, g_final):
    s = x.shape[0]
    tm = _pick_tile(s, 256)
    urw, k, v, qi, kw, ulru, qb, kb, vb, ki4 = _in_proj(x, w["g_mix"], w["in_ws"], tabs, tm)
    tb = _pick_tile(s, 512)
    y_rw, sbd = _rwkv(urw[None], jnp.zeros((1, 1, 1024), F32), jnp.zeros((1, 256, 256), F32), w["rw"], tb, tb)
    y_lru, h_last = _lru(ulru[None], jnp.zeros((1, 3, 256), F32), jnp.zeros((1, 1, 256), F32), w["lru"], tb, True)
    y_sa = _dsa_prompt(qb, qi, kw, ki4, kb, vb, _pick_tile(s, 256), _pick_tile(s, 1024))
    mk = _matmul(mem, w["wmk"])
    mv = _matmul(mem, w["wmv"])
    x1, qm = _post1(x, y_rw[0], y_sa, y_lru[0], *w["wo"], w["g_mem"], w["wq"], tm)
    o = _mem_attn(qm[None], mk[None], mv[None], tm)
    x3, yf = _post2(x1, o[0], w["wmo"], w["g_ffn"], w["wg"], w["wu"], w["wd"], g_final, tm)
    new = dict(k=k.reshape(1, s, 8, 64), v=v.reshape(1, s, 8, 64), kidx=kw[None, :, :64],
               mk=mk.reshape(1, 256, 4, 256), mv=mv.reshape(1, 256, 4, 256), rwkv=_blockdiag_to_state(sbd),
               shift=urw[None, s - 1], h=h_last[:, 0], conv=ulru[None, s - 3:, :256])
    return x3, yf, new


def _sample_layer(x, l, w, tabs, g_final, cache_k, cache_v, cache_kidx, cache_mem_k, cache_mem_v, state_rwkv,
                  state_rwkv_shift, state_lru_h, state_lru_conv, page_table, db, t):
    m = db * t
    urw, k, v, qi, kw, ulru, qb, kb, vb, ki4 = _in_proj(x, w["g_mix"], w["in_ws"], tabs, m)
    urw3 = urw.reshape(db, t, 1024)
    u_pad = jnp.pad(urw3, ((0, 0), (0, RW_CHUNK - t), (0, 0)))
    y_rw, sbd = _rwkv(u_pad, state_rwkv_shift[l][:, None], _state_to_blockdiag(state_rwkv[l]), w["rw"], RW_CHUNK, t)
    ulru3 = ulru.reshape(db, t, 512)
    y_lru, h_last = _lru(ulru3, state_lru_conv[l], state_lru_h[l][:, None], w["lru"], t, False)
    n_pool = cache_k.shape[1]
    keys, thr = _dsa_sample_index(page_table, qi.reshape(db, t, 512), kw.reshape(db, t, 128), cache_kidx[l], t)
    pad_new = lambda a: jnp.pad(a.reshape(db, t, 512), ((0, 0), (0, PAGE - t), (0, 0)))
    y_sa = _dsa_sample_attend(page_table, qb.reshape(db, t, 512), keys, thr, pad_new(k), pad_new(v),
                              cache_k[l].reshape(n_pool, PAGE, 512), cache_v[l].reshape(n_pool, PAGE, 512), t)
    x1, qm = _post1(x, y_rw[:, :t].reshape(m, 256), y_sa.reshape(m, 512), y_lru.reshape(m, 256), *w["wo"],
                    w["g_mem"], w["wq"], m)
    o = _mem_attn(qm.reshape(db, t, 1024), cache_mem_k[l].reshape(db, 256, 1024),
                  cache_mem_v[l].reshape(db, 256, 1024), t)
    x3, yf = _post2(x1, o.reshape(m, 1024), w["wmo"], w["g_ffn"], w["wg"], w["wu"], w["wd"], g_final, m)
    conv = jnp.concatenate([state_lru_conv[l], ulru3[:, :, :256]], axis=1)[:, t:]
    new = dict(k=k.reshape(db, t, 8, 64), v=v.reshape(db, t, 8, 64), kidx=kw.reshape(db, t, 128)[:, :, :64],
               rwkv=_blockdiag_to_state(sbd), shift=urw3[:, t - 1], h=h_last[:, 0], conv=conv)
    return x3, yf, new


def kernel(x_prompt, x_sample, mem_prompt, cache_k, cache_v, cache_kidx, cache_mem_k, cache_mem_v, state_rwkv, state_rwkv_shift, state_lru_h, state_lru_conv, page_table, g_mix, w_in, w_out, rw_mu, rw_w0, rw_w2, rw_a0, rw_a2, rw_g2, rw_kk, rw_ka, rw_rk, rw_ln_w, rw_ln_b, lru_conv_w, lru_conv_b, lru_wa, lru_ba, lru_wx, lru_bx, lru_lambda, g_mem, w_mq, w_mk, w_mv, w_mo, g_ffn, w_gate, w_up, w_down, g_final):
    depth = w_in.shape[0]
    _, s, _ = x_prompt.shape
    db, t, _ = x_sample.shape
    past_len = page_table.shape[1] * PAGE
    rw = (rw_mu, rw_w0, rw_w2, rw_a0, rw_a2, rw_g2, rw_kk, rw_ka, rw_rk, rw_ln_w, rw_ln_b)
    lru = (lru_conv_w, lru_conv_b, lru_wa, lru_ba, lru_wx, lru_bx, lru_lambda)
    tabs_p = _rope_tables(jnp.arange(s))
    tabs_s = _rope_tables(jnp.tile(past_len + jnp.arange(t), db))
    gf = g_final.reshape(1, 1024)
    xp, xs = x_prompt[0], x_sample.reshape(db * t, 1024)
    mem = mem_prompt[0]
    news_p, news_s = [], []
    yp = ys = None
    for l in range(depth):
        w = _layer_weights(l, w_in, w_out, rw, lru, g_mix, g_mem, w_mq, w_mk, w_mv, w_mo, g_ffn, w_gate, w_up,
                           w_down)
        xp, yp, new_p = _prompt_layer(xp, mem, w, tabs_p, gf)
        xs, ys, new_s = _sample_layer(xs, l, w, tabs_s, gf, cache_k, cache_v, cache_kidx, cache_mem_k, cache_mem_v,
                                      state_rwkv, state_rwkv_shift, state_lru_h, state_lru_conv, page_table, db, t)
        news_p.append(new_p)
        news_s.append(new_s)
    stk = lambda news, name: jnp.stack([n[name] for n in news])
    return (yp[None], ys.reshape(db, t, 1024),
            stk(news_p, "k"), stk(news_p, "v"), stk(news_p, "kidx"), stk(news_p, "mk"), stk(news_p, "mv"),
            stk(news_p, "rwkv"), stk(news_p, "shift"), stk(news_p, "h"), stk(news_p, "conv"),
            stk(news_s, "k"), stk(news_s, "v"), stk(news_s, "kidx"), stk(news_s, "rwkv"), stk(news_s, "shift"),
            stk(news_s, "h"), stk(news_s, "conv"))
```

```python
import functools
import math

import jax
import jax.numpy as jnp
from jax import lax
from jax.experimental import pallas as pl
from jax.experimental.pallas import tpu as pltpu

F32 = jnp.float32
BF16 = jnp.bfloat16
I32 = jnp.int32

D_MODEL = 1024
HEAD_DIM = 64
RW_HEADS = 4
RW_WIDTH = 256
RW_COLS = 1024
RW_GN_EPS = 64e-5
SA_HEADS = 8
SA_WIDTH = 512
IDX_SCALE = 512.0 ** -0.5
TOPK = 256
LRU_WIDTH = 256
LRU_C = 8.0
D_FF = 2816
MEM_HEADS = 4
MEM_HEAD_DIM = 256
ROPE_THETA = 10000.0
NORM_EPS = 1e-6
PAGE = 128
INT_MIN = -2 ** 31
NEG_BIG = -1e30
VMEM_LIMIT = 56 * 1024 * 1024
RW_CHUNK = 64


def _params(*sem):
    return pltpu.CompilerParams(dimension_semantics=sem, vmem_limit_bytes=VMEM_LIMIT)


def _full(shape):
    n = len(shape)
    return pl.BlockSpec(shape, lambda *_: (0,) * n)


def _dot(a, b):
    return jnp.dot(a, b, preferred_element_type=F32)


def _dot_nt(a, b):
    return lax.dot_general(a, b, (((1,), (1,)), ((), ())), preferred_element_type=F32)


def _rms(x, g):
    return x * lax.rsqrt(jnp.mean(x * x, axis=-1, keepdims=True) + NORM_EPS) * g


def _split_bf16(x):
    hi = x.astype(BF16)
    lo = (x - hi.astype(F32)).astype(BF16)
    return hi, lo


_NN = (((1,), (0,)), ((), ()))
_NT = (((1,), (1,)), ((), ()))
_TN = (((0,), (0,)), ((), ()))


def _dot3(a, b, dims=_NN):
    ah, al = _split_bf16(a)
    bh, bl = _split_bf16(b)
    d = lambda x, y: lax.dot_general(x, y, dims, preferred_element_type=F32)
    return d(ah, bh) + (d(ah, bl) + d(al, bh))


def _dot_exact_rhs(a, m):
    hi = a.astype(BF16)
    r = a - hi.astype(F32)
    mid = r.astype(BF16)
    lo = (r - mid.astype(F32)).astype(BF16)
    return _dot(hi, m) + (_dot(mid, m) + _dot(lo, m))


def _dot_exact_lhs(m, b):
    hi = b.astype(BF16)
    r = b - hi.astype(F32)
    mid = r.astype(BF16)
    lo = (r - mid.astype(F32)).astype(BF16)
    return _dot(m, hi) + (_dot(m, mid) + _dot(m, lo))


def _rope(u, cos, sin):
    w = u.shape[-1]
    lane = lax.broadcasted_iota(I32, u.shape, 1)
    rot = jnp.where((lane & 63) < 32, pltpu.roll(u, w - 32, 1), pltpu.roll(u, 32, 1))
    return u * cos + rot * sin


def _in_proj_kernel(x_ref, g_ref, wrw_ref, wq_ref, wk_ref, wv_ref, wqi_ref, wkw_ref, wlru_ref,
                    cos_ref, sin_ref, ckw_ref, skw_ref,
                    urw_ref, k_ref, v_ref, qi_ref, kw_ref, ulru_ref, qb_ref, kb_ref, vb_ref, ki4_ref):
    hb = _rms(x_ref[...], g_ref[...]).astype(BF16)
    urw_ref[...] = _dot(hb, wrw_ref[...])
    ulru_ref[...] = _dot(hb, wlru_ref[...])
    cos = jnp.concatenate([cos_ref[...]] * 4, axis=1)
    sin = jnp.concatenate([sin_ref[...]] * 4, axis=1)
    q = _rope(_dot(hb, wq_ref[...]), cos, sin)
    k = _rope(_dot(hb, wk_ref[...]), cos, sin)
    v = _dot(hb, wv_ref[...])
    qi_ref[...] = _rope(_dot(hb, wqi_ref[...]), cos, sin)
    kw = _rope(_dot(hb, wkw_ref[...]), ckw_ref[...], skw_ref[...])
    k_ref[...] = k
    v_ref[...] = v
    kw_ref[...] = kw
    qb_ref[...] = (q * (HEAD_DIM ** -0.5)).astype(BF16)
    kb_ref[...] = k.astype(BF16)
    vb_ref[...] = v.astype(BF16)
    hi, lo = _split_bf16(kw[:, :64])
    ki4_ref[...] = jnp.concatenate([hi, lo, hi, lo], axis=1)


def _in_proj(x, g, ws, tabs, tm):
    m = x.shape[0]
    wrw, wq, wk, wv, wqi, wkw, wlru = ws
    row = lambda n: pl.BlockSpec((tm, n), lambda i: (i, 0))
    out_shape = [
        jax.ShapeDtypeStruct((m, 1024), F32), jax.ShapeDtypeStruct((m, 512), F32),
        jax.ShapeDtypeStruct((m, 512), F32), jax.ShapeDtypeStruct((m, 512), F32),
        jax.ShapeDtypeStruct((m, 128), F32), jax.ShapeDtypeStruct((m, 512), F32),
        jax.ShapeDtypeStruct((m, 512), BF16), jax.ShapeDtypeStruct((m, 512), BF16),
        jax.ShapeDtypeStruct((m, 512), BF16), jax.ShapeDtypeStruct((m, 256), BF16),
    ]
    return pl.pallas_call(
        _in_proj_kernel,
        grid=(m // tm,),
        in_specs=[row(1024), _full((1, 1024))] + [_full(w.shape) for w in ws] + [row(128)] * 4,
        out_specs=[row(1024), row(512), row(512), row(512), row(128), row(512), row(512), row(512), row(512),
                   row(256)],
        out_shape=out_shape,
        compiler_params=_params("parallel"),
        name="in_proj",
    )(x, g, wrw, wq, wk, wv, wqi, wkw, wlru, *tabs)


def _rope_tables(pos):
    half = HEAD_DIM // 2
    inv = ROPE_THETA ** (-jnp.arange(half, dtype=F32) / half)
    ang = pos.astype(F32)[:, None] * inv[None, :]
    c, s = jnp.cos(ang), jnp.sin(ang)
    m = pos.shape[0]
    cos = jnp.concatenate([c, c, c, c], axis=1)
    sin = jnp.concatenate([-s, s, -s, s], axis=1)
    ckw = jnp.concatenate([c, c, jnp.full((m, SA_HEADS), IDX_SCALE, F32), jnp.zeros((m, 56), F32)], axis=1)
    skw = jnp.concatenate([-s, s, jnp.zeros((m, 64), F32)], axis=1)
    return cos, sin, ckw, skw


def _rwkv_kernel(u_ref, prev_ref, s0_ref, mu_ref, w0_ref, w2_ref, a0_ref, a2_ref, g2_ref, kk_ref, ka_ref,
                 rk_ref, lnw_ref, lnb_ref, y_ref, sout_ref,
                 sbd, carry, r_s, kn_s, a_s, k2_s, v_s, lw_s, y_s, *, tb_rows, n_valid):
    c_len = RW_CHUNK
    tb = pl.program_id(1)

    @pl.when(tb == 0)
    def _():
        sbd[...] = s0_ref[0]
        carry[...] = prev_ref[0]

    u = u_ref[0]
    row = lax.broadcasted_iota(I32, u.shape, 0)
    u_prev = jnp.where(row == 0, carry[...], pltpu.roll(u, 1, 0))
    last = min(n_valid, tb_rows) - 1
    carry[...] = u[last:last + 1, :]
    us = u + mu_ref[...] * (u_prev - u)
    r, k, v = us[:, 0:256], us[:, 256:512], us[:, 512:768]
    wa, gd = us[:, 768:896], us[:, 896:1024]

    li = lax.broadcasted_iota(I32, (256, 256), 0) // 64
    lj = lax.broadcasted_iota(I32, (256, 256), 1) // 64
    blockdiag = (li == lj).astype(F32)
    head_sum = functools.partial(_dot_exact_rhs, m=blockdiag.astype(BF16))

    xw = w0_ref[...] + _dot3(jnp.tanh(wa), w2_ref[...])
    logw = -math.exp(-0.5) * jax.nn.sigmoid(xw)
    a = jax.nn.sigmoid(a0_ref[...] + _dot3(wa, a2_ref[...]))
    g = _dot3(jax.nn.sigmoid(gd), g2_ref[...])
    kkv = k * kk_ref[...]
    kn = kkv / jnp.maximum(jnp.sqrt(head_sum(kkv * kkv)), 1e-12)
    k2 = k * (1.0 + (a - 1.0) * ka_ref[...])
    bonus = head_sum(r * k2 * rk_ref[...]) * v
    if n_valid < tb_rows:
        ok = lax.broadcasted_iota(I32, (tb_rows, 256), 0) < n_valid
        logw = jnp.where(ok, logw, 0.0)
        kn = jnp.where(ok, kn, 0.0)
        k2 = jnp.where(ok, k2, 0.0)
        v = jnp.where(ok, v, 0.0)
    r_s[...] = r
    kn_s[...] = kn
    a_s[...] = a
    k2_s[...] = k2
    v_s[...] = v
    lw_s[...] = logw

    lane = lax.broadcasted_iota(I32, (1, 256), 1) // 64
    hmask = [(lane == h).astype(F32) for h in range(RW_HEADS)]
    ci = lax.broadcasted_iota(I32, (c_len, c_len), 0)
    cj = lax.broadcasted_iota(I32, (c_len, c_len), 1)
    tri_incl = (ci >= cj).astype(BF16)
    eye = (ci == cj).astype(F32)
    mi = lax.broadcasted_iota(I32, (8 * c_len, 2 * c_len), 0)
    mj = lax.broadcasted_iota(I32, (8 * c_len, 2 * c_len), 1)
    mt, ms = mi & (c_len - 1), mj & (c_len - 1)
    keep = ms < mt + jnp.where(mi < 4 * c_len, 0, 1)
    pick = lambda x: sum(x[h * c_len:(h + 1) * c_len] * hmask[h] for h in range(RW_HEADS))
    n_sq = int(math.log2(c_len)) - 1

    def chunk(c, _):
        sl = pl.ds(pl.multiple_of(c * c_len, c_len), c_len)
        lw, rc, knc, ac, k2c, vc = lw_s[sl, :], r_s[sl, :], kn_s[sl, :], a_s[sl, :], k2_s[sl, :], v_s[sl, :]
        lcum = _dot_exact_lhs(tri_incl, lw)
        gam, gex, gin = jnp.exp(lcum), jnp.exp(lcum - lw), jnp.exp(-lcum)
        rt, at, bt, kt = rc * gam, knc * gex, -(knc * ac) * gin, k2c * gin
        g_end = gam[c_len - 1:c_len, :]
        lhs = jnp.concatenate([at * hm for hm in hmask] + [rt * hm for hm in hmask], axis=0)
        m = jnp.where(keep, _dot3(lhs, jnp.concatenate([bt, kt], axis=0), _NT), 0.0)
        s = sbd[...]
        asrs = _dot3(jnp.concatenate([at, rt], axis=0), s, _NT)
        w1 = _dot3(m[:4 * c_len], jnp.concatenate([jnp.zeros_like(vc), vc], axis=0))
        wm = asrs[:c_len] + pick(w1)
        inv = []
        for h in range(RW_HEADS):
            q = m[h * c_len:(h + 1) * c_len, 0:c_len]
            x = eye + q
            for _ in range(n_sq):
                q = _dot3(q, q)
                x = x + _dot3(x, q)
            inv.append(x)
        p = pick(_dot3(jnp.concatenate(inv, axis=0), wm))
        pv = jnp.concatenate([p, vc], axis=0)
        y_s[sl, :] = asrs[c_len:] + pick(_dot3(m[4 * c_len:], pv))
        upd = _dot3(pv, jnp.concatenate([bt * g_end, kt * g_end], axis=0), _TN)
        sbd[...] = s * g_end + upd * blockdiag
        return 0

    lax.fori_loop(0, tb_rows // c_len, chunk, 0)

    y = y_s[...]
    mean = head_sum(y) * (1.0 / HEAD_DIM)
    yc = y - mean
    var = head_sum(yc * yc) * (1.0 / HEAD_DIM)
    yn = yc * lax.rsqrt(var + RW_GN_EPS) * lnw_ref[...] + lnb_ref[...]
    y_ref[0] = (yn + bonus) * g
    sout_ref[0] = sbd[...]


def _rwkv(u, prev, s0bd, prm, tb_rows, n_valid):
    b, t, _ = u.shape
    vec = lambda n: _full((1, n))
    kern = functools.partial(_rwkv_kernel, tb_rows=tb_rows, n_valid=n_valid)
    return pl.pallas_call(
        kern,
        grid=(b, t // tb_rows),
        in_specs=[pl.BlockSpec((1, tb_rows, 1024), lambda i, j: (i, j, 0)),
                  pl.BlockSpec((1, 1, 1024), lambda i, j: (i, 0, 0)),
                  pl.BlockSpec((1, 256, 256), lambda i, j: (i, 0, 0)),
                  vec(1024), vec(256), _full((128, 256)), vec(256), _full((128, 256)), _full((128, 256)),
                  vec(256), vec(256), vec(256), vec(256), vec(256)],
        out_specs=[pl.BlockSpec((1, tb_rows, 256), lambda i, j: (i, j, 0)),
                   pl.BlockSpec((1, 256, 256), lambda i, j: (i, 0, 0))],
        out_shape=[jax.ShapeDtypeStruct((b, t, 256), F32), jax.ShapeDtypeStruct((b, 256, 256), F32)],
        scratch_shapes=[pltpu.VMEM((256, 256), F32), pltpu.VMEM((1, 1024), F32)]
        + [pltpu.VMEM((tb_rows, 256), F32)] * 7,
        compiler_params=_params("parallel", "arbitrary"),
        name="rwkv7",
    )(u, prev, s0bd, *prm)


def _lru_kernel(u_ref, cb_ref, h0_ref, cw_ref, cbias_ref, wa_ref, ba_ref, wx_ref, bx_ref, lam_ref,
                out_ref, hl_ref, ext, hc, *, tb_rows, reset_first):
    tb = pl.program_id(1)

    @pl.when(tb == 0)
    def _():
        ext[0:8, :] = jnp.zeros((8, 256), F32)
        ext[5:8, :] = cb_ref[0]
        hc[...] = h0_ref[0]

    u = u_ref[0]
    xb, gate = u[:, :256], u[:, 256:]
    ext[8:8 + tb_rows, :] = xb
    cw = cw_ref[...]
    y = (cbias_ref[...] + cw[0:1] * ext[5:5 + tb_rows, :] + cw[1:2] * ext[6:6 + tb_rows, :]
         + cw[2:3] * ext[7:7 + tb_rows, :] + cw[3:4] * xb)
    ext[0:8, :] = ext[tb_rows:tb_rows + 8, :]
    gate_r = jax.nn.sigmoid(_dot3(y, wa_ref[...]) + ba_ref[...])
    gate_i = jax.nn.sigmoid(_dot3(y, wx_ref[...]) + bx_ref[...])
    lam = lam_ref[...]
    softplus_neg = jnp.maximum(-lam, 0.0) + jnp.log(1.0 + jnp.exp(-jnp.abs(lam)))
    log_a = (-LRU_C * softplus_neg) * gate_r
    a = jnp.exp(log_a)
    mult = jnp.sqrt(1.0 - jnp.exp(2.0 * log_a))
    row = lax.broadcasted_iota(I32, (tb_rows, 256), 0)
    if reset_first:
        mult = jnp.where((row == 0) & (tb == 0), 1.0, mult)
    bv = mult * gate_i * y
    d = 1
    while d < tb_rows:
        live = row >= d
        a_sh = jnp.where(live, pltpu.roll(a, d, 0), 1.0)
        b_sh = jnp.where(live, pltpu.roll(bv, d, 0), 0.0)
        bv = a * b_sh + bv
        a = a * a_sh
        d *= 2
    h = a * hc[...] + bv
    hc[...] = h[tb_rows - 1:tb_rows, :]
    gelu = 0.5 * gate * (1.0 + jnp.tanh(math.sqrt(2.0 / math.pi) * (gate + 0.044715 * gate * gate * gate)))
    out_ref[0] = h * gelu
    hl_ref[0] = h[tb_rows - 1:tb_rows, :]


def _lru(u, conv_buf, h0, prm, tb_rows, reset_first):
    b, t, _ = u.shape
    vec = _full((1, 256))
    kern = functools.partial(_lru_kernel, tb_rows=tb_rows, reset_first=reset_first)
    return pl.pallas_call(
        kern,
        grid=(b, t // tb_rows),
        in_specs=[pl.BlockSpec((1, tb_rows, 512), lambda i, j: (i, j, 0)),
                  pl.BlockSpec((1, 3, 256), lambda i, j: (i, 0, 0)),
                  pl.BlockSpec((1, 1, 256), lambda i, j: (i, 0, 0)),
                  _full((4, 256)), vec, _full((256, 256)), vec, _full((256, 256)), vec, vec],
        out_specs=[pl.BlockSpec((1, tb_rows, 256), lambda i, j: (i, j, 0)),
                   pl.BlockSpec((1, 1, 256), lambda i, j: (i, 0, 0))],
        out_shape=[jax.ShapeDtypeStruct((b, t, 256), F32), jax.ShapeDtypeStruct((b, 1, 256), F32)],
        scratch_shapes=[pltpu.VMEM((tb_rows + 8, 256), F32), pltpu.VMEM((1, 256), F32)],
        compiler_params=_params("parallel", "arbitrary"),
        name="rglru",
    )(u, conv_buf, h0, *prm)


def _sortable(x):
    bits = pltpu.bitcast(x, I32)
    return jnp.where(bits < 0, bits ^ 0x7FFFFFFF, bits)


def _bisect_threshold(count_gt, shape, topk):
    def body(it, t):
        cand = t + lax.shift_left(jnp.int32(1), 31 - it)
        return jnp.where(count_gt(cand) >= topk, cand, t)

    return lax.fori_loop(0, 32, body, jnp.full(shape, INT_MIN, I32))


def _dsa_prompt_kernel(qb_ref, qi_ref, kw_ref, ki4_ref, kb_ref, vb_ref, o_ref,
                       lhs_s, wcol_s, key_s, thr_s, qm_s, m_s, l_s, acc_s, *, qb_rows, kb_rows, sk, topk):
    i, ph, j = pl.program_id(0), pl.program_id(1), pl.program_id(2)
    nv = ((i + 1) * qb_rows + kb_rows - 1) // kb_rows
    valid = j < nv

    @pl.when((ph == 0) & (j == 0))
    def _init():
        qi, kw, qb = qi_ref[...], kw_ref[...], qb_ref[...]
        lane = lax.broadcasted_iota(I32, (qb_rows, 128), 1)
        for h in range(SA_HEADS):
            hi, lo = _split_bf16(qi[:, h * 64:(h + 1) * 64])
            lhs_s[h * qb_rows:(h + 1) * qb_rows, :] = jnp.concatenate([hi, hi, lo, lo], axis=1)
            wcol_s[h] = kw[:, 64 + h:65 + h]
            own = (lane < 64) if h % 2 == 0 else (lane >= 64)
            qp = qb[:, (h // 2) * 128:(h // 2 + 1) * 128]
            qm_s[h] = jnp.where(own, qp, jnp.zeros_like(qp))
        m_s[...] = jnp.full(m_s.shape, NEG_BIG, F32)
        l_s[...] = jnp.zeros(l_s.shape, F32)
        acc_s[...] = jnp.zeros(acc_s.shape, F32)

    @pl.when((ph == 0) & valid)
    def _index():
        qpos = i * qb_rows + lax.broadcasted_iota(I32, (qb_rows, 256), 0)
        for sub in range(kb_rows // 256):
            z = _dot_nt(lhs_s[...], ki4_ref[sub * 256:(sub + 1) * 256, :])
            sc = wcol_s[0] * jnp.maximum(z[0:qb_rows], 0.0)
            for h in range(1, SA_HEADS):
                sc = sc + wcol_s[h] * jnp.maximum(z[h * qb_rows:(h + 1) * qb_rows], 0.0)
            kpos = j * kb_rows + sub * 256 + lax.broadcasted_iota(I32, (qb_rows, 256), 1)
            key_s[j, :, sub * 256:(sub + 1) * 256] = jnp.where(kpos <= qpos, _sortable(sc), INT_MIN)

    @pl.when((ph == 1) & (j == 0))
    def _threshold():
        def count_gt(t):
            def body(jb, acc):
                x = jnp.where(key_s[jb] > t, 1, 0)
                part = x[:, 0:128]
                for c in range(1, kb_rows // 128):
                    part = part + x[:, c * 128:(c + 1) * 128]
                return acc + part
            acc = lax.fori_loop(0, nv, body, jnp.zeros((qb_rows, 128), I32))
            return jnp.sum(acc, axis=1, keepdims=True)

        thr_s[...] = _bisect_threshold(count_gt, (qb_rows, 1), topk)

    @pl.when((ph == 1) & valid)
    def _attend():
        t = thr_s[...]
        for sub in range(kb_rows // sk):
            bias = jnp.where(key_s[j, :, sub * sk:(sub + 1) * sk] > t, 0.0, NEG_BIG)
            for p in range(SA_HEADS // 2):
                kp = kb_ref[sub * sk:(sub + 1) * sk, p * 128:(p + 1) * 128]
                vp = vb_ref[sub * sk:(sub + 1) * sk, p * 128:(p + 1) * 128]
                for h in (2 * p, 2 * p + 1):
                    s = _dot_nt(qm_s[h], kp) + bias
                    m_old = m_s[h]
                    m_new = jnp.maximum(m_old, jnp.max(s, axis=1, keepdims=True))
                    alpha = jnp.exp(m_old - m_new)
                    pr = jnp.exp(s - m_new)
                    l_s[h] = alpha * l_s[h] + jnp.sum(pr, axis=1, keepdims=True)
                    acc_s[h] = alpha * acc_s[h] + _dot(pr.astype(BF16), vp)
                    m_s[h] = m_new

    @pl.when((ph == 1) & (j == nv - 1))
    def _finish():
        lane = lax.broadcasted_iota(I32, (qb_rows, 128), 1)
        for p in range(SA_HEADS // 2):
            o0 = acc_s[2 * p] / l_s[2 * p]
            o1 = acc_s[2 * p + 1] / l_s[2 * p + 1]
            o_ref[:, p * 128:(p + 1) * 128] = jnp.where(lane < 64, o0, o1)


def _dsa_prompt(qb, qi, kw, ki4, kb, vb, qb_rows, kb_rows):
    s = qb.shape[0]
    nq, nk = s // qb_rows, s // kb_rows
    sk = min(512, kb_rows)

    def kblock(i, ph, j):
        nv = ((i + 1) * qb_rows + kb_rows - 1) // kb_rows
        return jnp.minimum(j, nv - 1)

    qrow = lambda n: pl.BlockSpec((qb_rows, n), lambda i, ph, j: (i, 0))
    kern = functools.partial(_dsa_prompt_kernel, qb_rows=qb_rows, kb_rows=kb_rows, sk=sk, topk=min(TOPK, s // 4))
    return pl.pallas_call(
        kern,
        grid=(nq, 2, nk),
        in_specs=[qrow(512), qrow(512), qrow(128),
                  pl.BlockSpec((kb_rows, 256), lambda i, ph, j: (jnp.where(ph == 0, kblock(i, ph, j), 0), 0)),
                  pl.BlockSpec((kb_rows, 512), lambda i, ph, j: (jnp.where(ph == 1, kblock(i, ph, j), 0), 0)),
                  pl.BlockSpec((kb_rows, 512), lambda i, ph, j: (jnp.where(ph == 1, kblock(i, ph, j), 0), 0))],
        out_specs=qrow(512),
        out_shape=jax.ShapeDtypeStruct((s, 512), F32),
        scratch_shapes=[pltpu.VMEM((SA_HEADS * qb_rows, 256), BF16),
                        pltpu.VMEM((SA_HEADS, qb_rows, 1), F32),
                        pltpu.VMEM((nk, qb_rows, kb_rows), I32),
                        pltpu.VMEM((qb_rows, 1), I32),
                        pltpu.VMEM((SA_HEADS, qb_rows, 128), BF16),
                        pltpu.VMEM((SA_HEADS, qb_rows, 1), F32),
                        pltpu.VMEM((SA_HEADS, qb_rows, 1), F32),
                        pltpu.VMEM((SA_HEADS, qb_rows, 128), F32)],
        compiler_params=_params("arbitrary", "arbitrary", "arbitrary"),
        name="dsa_prompt",
    )(qb, qi, kw, ki4, kb, vb)


PAGES_PER_STEP = 8


def _dsa_sample_index_kernel(pt_ref, qi_ref, kw_ref, ki4n_ref, *rest, n_pages, t_len, topk):
    pages = rest[:PAGES_PER_STEP]
    key_ref, thr_ref, lhs_s, wcol_s = rest[PAGES_PER_STEP:]
    j = pl.program_id(1)
    n_steps = n_pages // PAGES_PER_STEP

    @pl.when(j == 0)
    def _init():
        qi, kw = qi_ref[0], kw_ref[0]
        hi, lo = _split_bf16(jnp.concatenate([qi[:, h * 64:(h + 1) * 64] for h in range(SA_HEADS)], axis=0))
        lhs_s[...] = jnp.concatenate([hi, hi, lo, lo], axis=1)
        for h in range(SA_HEADS):
            wcol_s[h] = kw[:, 64 + h:65 + h]

    def head_sum(z):
        sc = wcol_s[0] * jnp.maximum(z[0:t_len], 0.0)
        for h in range(1, SA_HEADS):
            sc = sc + wcol_s[h] * jnp.maximum(z[h * t_len:(h + 1) * t_len], 0.0)
        return sc

    hi, lo = _split_bf16(jnp.concatenate([p[...] for p in pages], axis=1))
    keys = _sortable(head_sum(_dot(lhs_s[...], jnp.concatenate([hi, lo, hi, lo], axis=0))))
    for r in range(PAGES_PER_STEP):
        key_ref[0, j * PAGES_PER_STEP + r] = keys[:, r * PAGE:(r + 1) * PAGE]

    @pl.when(j == n_steps - 1)
    def _finish():
        sc = head_sum(_dot_nt(lhs_s[...], ki4n_ref[0]))
        tq = lax.broadcasted_iota(I32, (t_len, 128), 0)
        ts = lax.broadcasted_iota(I32, (t_len, 128), 1)
        key_ref[0, n_pages] = jnp.where(ts <= tq, _sortable(sc), INT_MIN)

        def count_gt(t):
            x = jnp.where(key_ref[0] > t[None], 1, 0)
            return jnp.sum(jnp.sum(x, axis=0), axis=1, keepdims=True)

        thr = _bisect_threshold(count_gt, (t_len, 1), topk)
        thr_ref[0] = jnp.broadcast_to(thr, (t_len, 128))


def _dsa_sample_index(page_table, qi, kw, ki4_new, cache_kidx_t, layer, t_len):
    b = qi.shape[0]
    n_pages = page_table.shape[1]
    n_steps = n_pages // PAGES_PER_STEP

    def page_spec(r):
        return pl.BlockSpec((None, None, 64, PAGE),
                            lambda i, j, pt: (layer, pt[i * n_pages + j * PAGES_PER_STEP + r], 0, 0))

    kern = functools.partial(_dsa_sample_index_kernel, n_pages=n_pages, t_len=t_len,
                             topk=min(TOPK, (n_pages * PAGE + t_len) // 4))
    grid_spec = pltpu.PrefetchScalarGridSpec(
        num_scalar_prefetch=1,
        grid=(b, n_steps),
        in_specs=[pl.BlockSpec((1, t_len, 512), lambda i, j, pt: (i, 0, 0)),
                  pl.BlockSpec((1, t_len, 128), lambda i, j, pt: (i, 0, 0)),
                  pl.BlockSpec((1, PAGE, 256), lambda i, j, pt: (i, 0, 0))]
        + [page_spec(r) for r in range(PAGES_PER_STEP)],
        out_specs=[pl.BlockSpec((1, n_pages + 1, t_len, 128), lambda i, j, pt: (i, 0, 0, 0)),
                   pl.BlockSpec((1, t_len, 128), lambda i, j, pt: (i, 0, 0))],
        scratch_shapes=[pltpu.VMEM((SA_HEADS * t_len, 256), BF16), pltpu.VMEM((SA_HEADS, t_len, 1), F32)],
    )
    return pl.pallas_call(
        kern,
        grid_spec=grid_spec,
        out_shape=[jax.ShapeDtypeStruct((b, n_pages + 1, t_len, 128), I32),
                   jax.ShapeDtypeStruct((b, t_len, 128), I32)],
        compiler_params=_params("parallel", "arbitrary"),
        name="dsa_sample_index",
    )(page_table.reshape(-1), qi, kw, ki4_new, *([cache_kidx_t] * PAGES_PER_STEP))


def _dsa_sample_attend_kernel(pt_ref, qb_ref, key_ref, keyn_ref, thr_ref, kn_ref, vn_ref, *rest, n_pages, t_len):
    kpages = rest[:PAGES_PER_STEP]
    vpages = rest[PAGES_PER_STEP:2 * PAGES_PER_STEP]
    o_ref, qbd_s, m_s, l_s, acc_s = rest[2 * PAGES_PER_STEP:]
    j = pl.program_id(1)
    n_steps = n_pages // PAGES_PER_STEP
    rows = SA_HEADS * t_len

    @pl.when(j == 0)
    def _init():
        qb = qb_ref[0].astype(F32)
        lane = lax.broadcasted_iota(I32, (t_len, 512), 1) // 64
        qbd = jnp.concatenate([jnp.where(lane == h, qb, 0.0) for h in range(SA_HEADS)], axis=0)
        qbd_s[...] = qbd.astype(BF16)
        m_s[...] = jnp.full(m_s.shape, NEG_BIG, F32)
        l_s[...] = jnp.zeros(l_s.shape, F32)
        acc_s[...] = jnp.zeros(acc_s.shape, F32)

    thr = thr_ref[0][:, 0:1]

    def update(keys, s, pv):
        bias = jnp.where(keys > thr, 0.0, NEG_BIG)
        s = s + jnp.concatenate([bias] * SA_HEADS, axis=0)
        m_old = m_s[...]
        m_new = jnp.maximum(m_old, jnp.max(s, axis=1, keepdims=True))
        alpha = jnp.exp(m_old - m_new)
        pr = jnp.exp(s - m_new)
        l_s[...] = alpha * l_s[...] + jnp.sum(pr, axis=1, keepdims=True)
        acc_s[...] = alpha * acc_s[...] + pv(pr.astype(BF16))
        m_s[...] = m_new

    kcat = jnp.concatenate([p[...].astype(BF16) for p in kpages], axis=1)
    vcat = jnp.concatenate([p[...].astype(BF16) for p in vpages], axis=1)
    keys = jnp.concatenate([key_ref[0, r] for r in range(PAGES_PER_STEP)], axis=1)
    update(keys, _dot(qbd_s[...], kcat), lambda pr: _dot_nt(pr, vcat))

    @pl.when(j == n_steps - 1)
    def _finish():
        kn, vn = kn_ref[0].astype(BF16), vn_ref[0].astype(BF16)
        update(keyn_ref[0, 0], _dot_nt(qbd_s[...], kn), lambda pr: _dot(pr, vn))
        o = acc_s[...] / l_s[...]
        lane = lax.broadcasted_iota(I32, (t_len, 512), 1) // 64
        out = jnp.zeros((t_len, 512), F32)
        for h in range(SA_HEADS):
            out = out + jnp.where(lane == h, o[h * t_len:(h + 1) * t_len], 0.0)
        o_ref[0] = out


def _dsa_sample_attend(page_table, qb, keys, thr, k_new, v_new, cache_k_t, cache_v_t, layer, t_len):
    b = qb.shape[0]
    n_pages = page_table.shape[1]
    n_steps = n_pages // PAGES_PER_STEP
    rows = SA_HEADS * t_len

    def page_spec(r):
        return pl.BlockSpec((None, None, 512, PAGE),
                            lambda i, j, pt: (layer, pt[i * n_pages + j * PAGES_PER_STEP + r], 0, 0))

    per_b = lambda n: pl.BlockSpec((1, t_len, n), lambda i, j, pt: (i, 0, 0))
    kern = functools.partial(_dsa_sample_attend_kernel, n_pages=n_pages, t_len=t_len)
    grid_spec = pltpu.PrefetchScalarGridSpec(
        num_scalar_prefetch=1,
        grid=(b, n_steps),
        in_specs=[per_b(512),
                  pl.BlockSpec((1, PAGES_PER_STEP, t_len, 128), lambda i, j, pt: (i, j, 0, 0)),
                  pl.BlockSpec((1, 1, t_len, 128), lambda i, j, pt: (i, n_pages, 0, 0)),
                  per_b(128),
                  pl.BlockSpec((1, PAGE, 512), lambda i, j, pt: (i, 0, 0)),
                  pl.BlockSpec((1, PAGE, 512), lambda i, j, pt: (i, 0, 0))]
        + [page_spec(r) for r in range(PAGES_PER_STEP)] * 2,
        out_specs=per_b(512),
        scratch_shapes=[pltpu.VMEM((rows, 512), BF16), pltpu.VMEM((rows, 1), F32), pltpu.VMEM((rows, 1), F32),
                        pltpu.VMEM((rows, 512), F32)],
    )
    return pl.pallas_call(
        kern,
        grid_spec=grid_spec,
        out_shape=jax.ShapeDtypeStruct((b, t_len, 512), F32),
        compiler_params=_params("parallel", "arbitrary"),
        name="dsa_sample_attend",
    )(page_table.reshape(-1), qb, keys, keys, thr, k_new, v_new,
      *([cache_k_t] * PAGES_PER_STEP), *([cache_v_t] * PAGES_PER_STEP))


def _matmul_kernel(x_ref, w_ref, o_ref):
    o_ref[...] = _dot(x_ref[...].astype(BF16), w_ref[...])


def _matmul(x, w):
    m, n = x.shape[0], w.shape[1]
    return pl.pallas_call(
        _matmul_kernel,
        grid=(1,),
        in_specs=[_full(x.shape), _full(w.shape)],
        out_specs=_full((m, n)),
        out_shape=jax.ShapeDtypeStruct((m, n), F32),
        compiler_params=_params("arbitrary"),
        name="mem_kv_proj",
    )(x, w)


def _post1_kernel(x_ref, yrw_ref, ysa_ref, ylru_ref, wo1_ref, wo2_ref, wo3_ref, g_ref, wq_ref, x1_ref, qm_ref):
    x1 = (x_ref[...] + _dot(yrw_ref[...].astype(BF16), wo1_ref[...])
          + _dot(ysa_ref[...].astype(BF16), wo2_ref[...]) + _dot(ylru_ref[...].astype(BF16), wo3_ref[...]))
    x1_ref[...] = x1
    qm = _dot(_rms(x1, g_ref[...]).astype(BF16), wq_ref[...])
    qm_ref[...] = (qm * (MEM_HEAD_DIM ** -0.5)).astype(BF16)


def _post1(x, yrw, ysa, ylru, wo1, wo2, wo3, g, wq, tm):
    m = x.shape[0]
    row = lambda n: pl.BlockSpec((tm, n), lambda i: (i, 0))
    return pl.pallas_call(
        _post1_kernel,
        grid=(m // tm,),
        in_specs=[row(1024), row(256), row(512), row(256), _full(wo1.shape), _full(wo2.shape), _full(wo3.shape),
                  _full((1, 1024)), _full(wq.shape)],
        out_specs=[row(1024), row(1024)],
        out_shape=[jax.ShapeDtypeStruct((m, 1024), F32), jax.ShapeDtypeStruct((m, 1024), BF16)],
        compiler_params=_params("parallel"),
        name="out_proj_memq",
    )(x, yrw, ysa, ylru, wo1, wo2, wo3, g, wq)


def _mem_attn_kernel(q_ref, mk_ref, mv_ref, o_ref):
    q = q_ref[0]
    mk = mk_ref[0].astype(BF16)
    mv = mv_ref[0].astype(BF16)
    for h in range(MEM_HEADS):
        sl = slice(h * MEM_HEAD_DIM, (h + 1) * MEM_HEAD_DIM)
        s = _dot_nt(q[:, sl], mk[:, sl])
        p = jnp.exp(s - jnp.max(s, axis=1, keepdims=True))
        o = _dot(p.astype(BF16), mv[:, sl]) / jnp.sum(p, axis=1, keepdims=True)
        o_ref[0, :, sl] = o.astype(BF16)


def _mem_attn(q, mk, mv, tm):
    b, t, _ = q.shape
    return pl.pallas_call(
        _mem_attn_kernel,
        grid=(b, t // tm),
        in_specs=[pl.BlockSpec((1, tm, 1024), lambda i, j: (i, j, 0)),
                  pl.BlockSpec((1, 256, 1024), lambda i, j: (i, 0, 0)),
                  pl.BlockSpec((1, 256, 1024), lambda i, j: (i, 0, 0))],
        out_specs=pl.BlockSpec((1, tm, 1024), lambda i, j: (i, j, 0)),
        out_shape=jax.ShapeDtypeStruct((b, t, 1024), BF16),
        compiler_params=_params("parallel", "parallel"),
        name="mem_attn",
    )(q, mk, mv)


def _post2_kernel(x1_ref, o_ref, wo_ref, g_ref, wg_ref, wu_ref, wd_ref, gf_ref, out_ref, *, last):
    x2 = x1_ref[...] + _dot(o_ref[...], wo_ref[...])
    hb = _rms(x2, g_ref[...]).astype(BF16)
    gt = _dot(hb, wg_ref[...])
    up = _dot(hb, wu_ref[...])
    act = (gt * jax.nn.sigmoid(gt) * up).astype(BF16)
    x3 = x2 + _dot(act, wd_ref[...])
    out_ref[...] = _rms(x3, gf_ref[...]) if last else x3


def _post2(x1, o, wo, g, wg, wu, wd, gf, tm, last):
    m = x1.shape[0]
    row = lambda n: pl.BlockSpec((tm, n), lambda i: (i, 0))
    once = lambda w: pl.BlockSpec(w.shape, lambda i: (0, 0), pipeline_mode=pl.Buffered(1))
    return pl.pallas_call(
        functools.partial(_post2_kernel, last=last),
        grid=(m // tm,),
        in_specs=[row(1024), row(1024), once(wo), _full((1, 1024)), once(wg), once(wu), once(wd), _full((1, 1024))],
        out_specs=row(1024),
        out_shape=jax.ShapeDtypeStruct((m, 1024), F32),
        compiler_params=_params("parallel"),
        name="memo_swiglu",
    )(x1, o, wo, g, wg, wu, wd, gf)


def _blockdiag4(w):
    n = w.shape[-1]
    return jnp.einsum("gij,gh->gihj", w, jnp.eye(4, dtype=w.dtype)).reshape(4 * n, 4 * n)


def _state_to_blockdiag(s):
    b = s.shape[0]
    return jnp.einsum("bhvk,hg->bhvgk", s, jnp.eye(4, dtype=s.dtype)).reshape(b, 256, 256)


def _blockdiag_to_state(sbd):
    b = sbd.shape[0]
    return jnp.einsum("bhvgk,hg->bhvk", sbd.reshape(b, 4, 64, 4, 64), jnp.eye(4, dtype=sbd.dtype))


def _layer_weights(l, w_in, w_out, rw, lru, g_mix, g_mem, w_mq, w_mk, w_mv, w_mo, g_ffn, w_gate, w_up, w_down):
    wi = w_in[l]
    o = RW_COLS
    cut = lambda a, n: wi[:, a:a + n].astype(BF16)
    wkw = jnp.pad(wi[:, o + 2048:o + 2120], ((0, 0), (0, 56))).astype(BF16)
    in_ws = (cut(0, 1024), cut(o, 512), cut(o + 512, 512), cut(o + 1024, 512), cut(o + 1536, 512), wkw,
             cut(o + 2120, 512))
    (rw_mu, rw_w0, rw_w2, rw_a0, rw_a2, rw_g2, rw_kk, rw_ka, rw_rk, rw_ln_w, rw_ln_b) = rw
    z = jnp.zeros((64, 256), F32)
    v256 = lambda a: a[l].reshape(1, 256)
    rw_prm = (rw_mu[l].reshape(1, 1024), v256(rw_w0), jnp.concatenate([rw_w2[l], z], axis=0), v256(rw_a0),
              jnp.concatenate([z, rw_a2[l]], axis=0), rw_g2[l], v256(rw_kk), v256(rw_ka), v256(rw_rk),
              v256(rw_ln_w), v256(rw_ln_b))
    (lru_conv_w, lru_conv_b, lru_wa, lru_ba, lru_wx, lru_bx, lru_lambda) = lru
    lru_prm = (lru_conv_w[l], v256(lru_conv_b), _blockdiag4(lru_wa[l]), v256(lru_ba), _blockdiag4(lru_wx[l]),
               v256(lru_bx), v256(lru_lambda))
    wo = w_out[l].astype(BF16)
    return dict(
        in_ws=in_ws, rw=rw_prm, lru=lru_prm, g_mix=g_mix[l].reshape(1, 1024),
        wo=(wo[:256], wo[256:768], wo[768:]), g_mem=g_mem[l].reshape(1, 1024), wq=w_mq[l].astype(BF16),
        wmk=w_mk[l].astype(BF16), wmv=w_mv[l].astype(BF16), wmo=w_mo[l].astype(BF16),
        g_ffn=g_ffn[l].reshape(1, 1024), wg=w_gate[l].astype(BF16), wu=w_up[l].astype(BF16),
        wd=w_down[l].astype(BF16))


def _pick_tile(n, pref):
    t = min(n, pref)
    while n % t:
        t //= 2
    return t


def _prompt_layer(x, mem, w, tabs, g_final, last):
    s = x.shape[0]
    tm = _pick_tile(s, 256)
    urw, k, v, qi, kw, ulru, qb, kb, vb, ki4 = _in_proj(x, w["g_mix"], w["in_ws"], tabs, tm)
    tb = _pick_tile(s, 512)
    y_rw, sbd = _rwkv(urw[None], jnp.zeros((1, 1, 1024), F32), jnp.zeros((1, 256, 256), F32), w["rw"], tb, tb)
    y_lru, h_last = _lru(ulru[None], jnp.zeros((1, 3, 256), F32), jnp.zeros((1, 1, 256), F32), w["lru"], tb, True)
    y_sa = _dsa_prompt(qb, qi, kw, ki4, kb, vb, _pick_tile(s, 256), _pick_tile(s, 1024))
    mk = _matmul(mem, w["wmk"])
    mv = _matmul(mem, w["wmv"])
    x1, qm = _post1(x, y_rw[0], y_sa, y_lru[0], *w["wo"], w["g_mem"], w["wq"], tm)
    o = _mem_attn(qm[None], mk[None], mv[None], tm)
    x3 = _post2(x1, o[0], w["wmo"], w["g_ffn"], w["wg"], w["wu"], w["wd"], g_final, tm, last)
    new = dict(k=k.reshape(1, s, 8, 64), v=v.reshape(1, s, 8, 64), kidx=kw[None, :, :64],
               mk=mk.reshape(1, 256, 4, 256), mv=mv.reshape(1, 256, 4, 256), rwkv=_blockdiag_to_state(sbd),
               shift=urw[None, s - 1], h=h_last[:, 0], conv=ulru[None, s - 3:, :256])
    return x3, new


def _sample_layer(x, l, w, tabs, g_final, cache_k, cache_v, cache_kidx, cache_mem_k, cache_mem_v, state_rwkv,
                  state_rwkv_shift, state_lru_h, state_lru_conv, page_table, db, t, last):
    m = db * t
    urw, k, v, qi, kw, ulru, qb, kb, vb, ki4 = _in_proj(x, w["g_mix"], w["in_ws"], tabs, m)
    urw3 = urw.reshape(db, t, 1024)
    u_pad = jnp.pad(urw3, ((0, 0), (0, RW_CHUNK - t), (0, 0)))
    y_rw, sbd = _rwkv(u_pad, state_rwkv_shift[l][:, None], _state_to_blockdiag(state_rwkv[l]), w["rw"], RW_CHUNK, t)
    ulru3 = ulru.reshape(db, t, 512)
    y_lru, h_last = _lru(ulru3, state_lru_conv[l], state_lru_h[l][:, None], w["lru"], t, False)
    pad_new = lambda a: jnp.pad(a.reshape(db, t, -1), ((0, 0), (0, PAGE - t), (0, 0)))
    keys, thr = _dsa_sample_index(page_table, qi.reshape(db, t, 512), kw.reshape(db, t, 128), pad_new(ki4),
                                  cache_kidx, l, t)
    y_sa = _dsa_sample_attend(page_table, qb.reshape(db, t, 512), keys, thr, pad_new(k), pad_new(v),
                              cache_k, cache_v, l, t)
    x1, qm = _post1(x, y_rw[:, :t].reshape(m, 256), y_sa.reshape(m, 512), y_lru.reshape(m, 256), *w["wo"],
                    w["g_mem"], w["wq"], m)
    o = _mem_attn(qm.reshape(db, t, 1024), cache_mem_k[l].reshape(db, 256, 1024),
                  cache_mem_v[l].reshape(db, 256, 1024), t)
    x3 = _post2(x1, o.reshape(m, 1024), w["wmo"], w["g_ffn"], w["wg"], w["wu"], w["wd"], g_final, m, last)
    conv = jnp.concatenate([state_lru_conv[l], ulru3[:, :, :256]], axis=1)[:, t:]
    new = dict(k=k.reshape(db, t, 8, 64), v=v.reshape(db, t, 8, 64), kidx=kw.reshape(db, t, 128)[:, :, :64],
               rwkv=_blockdiag_to_state(sbd), shift=urw3[:, t - 1], h=h_last[:, 0], conv=conv)
    return x3, new


def kernel(x_prompt, x_sample, mem_prompt, cache_k, cache_v, cache_kidx, cache_mem_k, cache_mem_v, state_rwkv, state_rwkv_shift, state_lru_h, state_lru_conv, page_table, g_mix, w_in, w_out, rw_mu, rw_w0, rw_w2, rw_a0, rw_a2, rw_g2, rw_kk, rw_ka, rw_rk, rw_ln_w, rw_ln_b, lru_conv_w, lru_conv_b, lru_wa, lru_ba, lru_wx, lru_bx, lru_lambda, g_mem, w_mq, w_mk, w_mv, w_mo, g_ffn, w_gate, w_up, w_down, g_final):
    depth = w_in.shape[0]
    _, s, _ = x_prompt.shape
    db, t, _ = x_sample.shape
    past_len = page_table.shape[1] * PAGE
    rw = (rw_mu, rw_w0, rw_w2, rw_a0, rw_a2, rw_g2, rw_kk, rw_ka, rw_rk, rw_ln_w, rw_ln_b)
    lru = (lru_conv_w, lru_conv_b, lru_wa, lru_ba, lru_wx, lru_bx, lru_lambda)
    tabs_p = _rope_tables(jnp.arange(s))
    tabs_s = _rope_tables(jnp.tile(past_len + jnp.arange(t), db))
    gf = g_final.reshape(1, 1024)
    n_pool = cache_k.shape[1]
    cache_kidx = jnp.transpose(cache_kidx, (0, 1, 3, 2))
    cache_k = jnp.transpose(cache_k, (0, 1, 3, 4, 2)).reshape(depth, n_pool, SA_WIDTH, PAGE)
    cache_v = jnp.transpose(cache_v, (0, 1, 3, 4, 2)).reshape(depth, n_pool, SA_WIDTH, PAGE)
    xp, xs = x_prompt[0], x_sample.reshape(db * t, 1024)
    mem = mem_prompt[0]
    news_p, news_s = [], []
    for l in range(depth):
        w = _layer_weights(l, w_in, w_out, rw, lru, g_mix, g_mem, w_mq, w_mk, w_mv, w_mo, g_ffn, w_gate, w_up,
                           w_down)
        last = l == depth - 1
        xp, new_p = _prompt_layer(xp, mem, w, tabs_p, gf, last)
        xs, new_s = _sample_layer(xs, l, w, tabs_s, gf, cache_k, cache_v, cache_kidx, cache_mem_k, cache_mem_v,
                                  state_rwkv, state_rwkv_shift, state_lru_h, state_lru_conv, page_table, db, t, last)
        news_p.append(new_p)
        news_s.append(new_s)
    stk = lambda news, name: jnp.stack([n[name] for n in news])
    return (xp[None], xs.reshape(db, t, 1024),
            stk(news_p, "k"), stk(news_p, "v"), stk(news_p, "kidx"), stk(news_p, "mk"), stk(news_p, "mv"),
            stk(news_p, "rwkv"), stk(news_p, "shift"), stk(news_p, "h"), stk(news_p, "conv"),
            stk(news_s, "k"), stk(news_s, "v"), stk(news_s, "kidx"), stk(news_s, "rwkv"), stk(news_s, "shift"),
            stk(news_s, "h"), stk(news_s, "conv"))
```

```python
import functools
import math

import jax
import jax.numpy as jnp
from jax import lax
from jax.experimental import pallas as pl
from jax.experimental.pallas import tpu as pltpu

F32 = jnp.float32
BF16 = jnp.bfloat16
I32 = jnp.int32

D_MODEL = 1024
HEAD_DIM = 64
RW_HEADS = 4
RW_WIDTH = 256
RW_COLS = 1024
RW_GN_EPS = 64e-5
SA_HEADS = 8
SA_WIDTH = 512
IDX_SCALE = 512.0 ** -0.5
TOPK = 256
LRU_WIDTH = 256
LRU_C = 8.0
D_FF = 2816
MEM_HEADS = 4
MEM_HEAD_DIM = 256
ROPE_THETA = 10000.0
NORM_EPS = 1e-6
PAGE = 128
LOG2E = 1.4426950408889634
INT_MIN = -2 ** 31
NEG_BIG = -1e30
VMEM_LIMIT = 56 * 1024 * 1024
RW_CHUNK = 64


def _params(*sem):
    return pltpu.CompilerParams(dimension_semantics=sem, vmem_limit_bytes=VMEM_LIMIT)


def _full(shape):
    n = len(shape)
    return pl.BlockSpec(shape, lambda *_: (0,) * n)


def _dot(a, b):
    return jnp.dot(a, b, preferred_element_type=F32)


def _dot_nt(a, b):
    return lax.dot_general(a, b, (((1,), (1,)), ((), ())), preferred_element_type=F32)


def _rms(x, g):
    return x * lax.rsqrt(jnp.mean(x * x, axis=-1, keepdims=True) + NORM_EPS) * g


def _split_bf16(x):
    hi = x.astype(BF16)
    lo = (x - hi.astype(F32)).astype(BF16)
    return hi, lo


_NN = (((1,), (0,)), ((), ()))
_NT = (((1,), (1,)), ((), ()))
_TN = (((0,), (0,)), ((), ()))


def _dot3(a, b, dims=_NN):
    ah, al = _split_bf16(a)
    bh, bl = _split_bf16(b)
    d = lambda x, y: lax.dot_general(x, y, dims, preferred_element_type=F32)
    return d(ah, bh) + (d(ah, bl) + d(al, bh))


def _dot_exact_rhs(a, m):
    hi = a.astype(BF16)
    r = a - hi.astype(F32)
    mid = r.astype(BF16)
    lo = (r - mid.astype(F32)).astype(BF16)
    return _dot(hi, m) + (_dot(mid, m) + _dot(lo, m))


def _dot_exact_lhs(m, b):
    hi = b.astype(BF16)
    r = b - hi.astype(F32)
    mid = r.astype(BF16)
    lo = (r - mid.astype(F32)).astype(BF16)
    return _dot(m, hi) + (_dot(m, mid) + _dot(m, lo))


def _rope(u, cos, sin):
    w = u.shape[-1]
    lane = lax.broadcasted_iota(I32, u.shape, 1)
    rot = jnp.where((lane & 63) < 32, pltpu.roll(u, w - 32, 1), pltpu.roll(u, 32, 1))
    return u * cos + rot * sin


def _in_proj_kernel(x_ref, g_ref, wrw_ref, wq_ref, wk_ref, wv_ref, wqi_ref, wkw_ref, wlru_ref,
                    cos_ref, sin_ref, ckw_ref, skw_ref,
                    urw_ref, k_ref, v_ref, qi_ref, kw_ref, ulru_ref, qb_ref, kb_ref, vt_ref, ki4_ref):
    hb = _rms(x_ref[...], g_ref[...]).astype(BF16)
    urw_ref[...] = _dot(hb, wrw_ref[...])
    ulru_ref[...] = _dot(hb, wlru_ref[...])
    cos = jnp.concatenate([cos_ref[...]] * 4, axis=1)
    sin = jnp.concatenate([sin_ref[...]] * 4, axis=1)
    q = _rope(_dot(hb, wq_ref[...]), cos, sin)
    k = _rope(_dot(hb, wk_ref[...]), cos, sin)
    v = _dot(hb, wv_ref[...])
    qi_ref[...] = _rope(_dot(hb, wqi_ref[...]), cos, sin)
    kw = _rope(_dot(hb, wkw_ref[...]), ckw_ref[...], skw_ref[...])
    k_ref[...] = k
    v_ref[...] = v
    kw_ref[...] = kw
    qb_ref[...] = (q * (HEAD_DIM ** -0.5 * LOG2E)).astype(BF16)
    kb_ref[...] = k.astype(BF16)
    vt_ref[...] = v.T.astype(BF16)
    hi, lo = _split_bf16(kw[:, :64])
    ki4_ref[...] = jnp.concatenate([hi, lo, hi, lo], axis=1)


def _in_proj(x, g, ws, tabs, tm):
    m = x.shape[0]
    wrw, wq, wk, wv, wqi, wkw, wlru = ws
    row = lambda n: pl.BlockSpec((tm, n), lambda i: (i, 0))
    out_shape = [
        jax.ShapeDtypeStruct((m, 1024), F32), jax.ShapeDtypeStruct((m, 512), F32),
        jax.ShapeDtypeStruct((m, 512), F32), jax.ShapeDtypeStruct((m, 512), F32),
        jax.ShapeDtypeStruct((m, 128), F32), jax.ShapeDtypeStruct((m, 512), F32),
        jax.ShapeDtypeStruct((m, 512), BF16), jax.ShapeDtypeStruct((m, 512), BF16),
        jax.ShapeDtypeStruct((512, m), BF16), jax.ShapeDtypeStruct((m, 256), BF16),
    ]
    return pl.pallas_call(
        _in_proj_kernel,
        grid=(m // tm,),
        in_specs=[row(1024), _full((1, 1024))] + [_full(w.shape) for w in ws] + [row(128)] * 4,
        out_specs=[row(1024), row(512), row(512), row(512), row(128), row(512), row(512), row(512),
                   pl.BlockSpec((512, tm), lambda i: (0, i)), row(256)],
        out_shape=out_shape,
        compiler_params=_params("parallel"),
        name="in_proj",
    )(x, g, wrw, wq, wk, wv, wqi, wkw, wlru, *tabs)


def _rope_tables(pos):
    half = HEAD_DIM // 2
    inv = ROPE_THETA ** (-jnp.arange(half, dtype=F32) / half)
    ang = pos.astype(F32)[:, None] * inv[None, :]
    c, s = jnp.cos(ang), jnp.sin(ang)
    m = pos.shape[0]
    cos = jnp.concatenate([c, c, c, c], axis=1)
    sin = jnp.concatenate([-s, s, -s, s], axis=1)
    ckw = jnp.concatenate([c, c, jnp.full((m, SA_HEADS), IDX_SCALE, F32), jnp.zeros((m, 56), F32)], axis=1)
    skw = jnp.concatenate([-s, s, jnp.zeros((m, 64), F32)], axis=1)
    return cos, sin, ckw, skw


def _rwkv_kernel(u_ref, prev_ref, s0_ref, mu_ref, w0_ref, w2_ref, a0_ref, a2_ref, g2_ref, kk_ref, ka_ref,
                 rk_ref, lnw_ref, lnb_ref, y_ref, sout_ref,
                 sbd, carry, r_s, kn_s, a_s, k2_s, v_s, lw_s, y_s, *, tb_rows, n_valid):
    c_len = RW_CHUNK
    tb = pl.program_id(1)

    @pl.when(tb == 0)
    def _():
        sbd[...] = s0_ref[0]
        carry[...] = prev_ref[0]

    u = u_ref[0]
    row = lax.broadcasted_iota(I32, u.shape, 0)
    u_prev = jnp.where(row == 0, carry[...], pltpu.roll(u, 1, 0))
    last = min(n_valid, tb_rows) - 1
    carry[...] = u[last:last + 1, :]
    us = u + mu_ref[...] * (u_prev - u)
    r, k, v = us[:, 0:256], us[:, 256:512], us[:, 512:768]
    wa, gd = us[:, 768:896], us[:, 896:1024]

    li = lax.broadcasted_iota(I32, (256, 256), 0) // 64
    lj = lax.broadcasted_iota(I32, (256, 256), 1) // 64
    blockdiag = (li == lj).astype(F32)
    head_sum = functools.partial(_dot_exact_rhs, m=blockdiag.astype(BF16))

    xw = w0_ref[...] + _dot3(jnp.tanh(wa), w2_ref[...])
    logw = -math.exp(-0.5) * jax.nn.sigmoid(xw)
    a = jax.nn.sigmoid(a0_ref[...] + _dot3(wa, a2_ref[...]))
    g = _dot3(jax.nn.sigmoid(gd), g2_ref[...])
    kkv = k * kk_ref[...]
    kn = kkv / jnp.maximum(jnp.sqrt(head_sum(kkv * kkv)), 1e-12)
    k2 = k * (1.0 + (a - 1.0) * ka_ref[...])
    bonus = head_sum(r * k2 * rk_ref[...]) * v
    if n_valid < tb_rows:
        ok = lax.broadcasted_iota(I32, (tb_rows, 256), 0) < n_valid
        logw = jnp.where(ok, logw, 0.0)
        kn = jnp.where(ok, kn, 0.0)
        k2 = jnp.where(ok, k2, 0.0)
        v = jnp.where(ok, v, 0.0)
    r_s[...] = r
    kn_s[...] = kn
    a_s[...] = a
    k2_s[...] = k2
    v_s[...] = v
    lw_s[...] = logw

    lane = lax.broadcasted_iota(I32, (1, 256), 1) // 64
    hmask = [(lane == h).astype(F32) for h in range(RW_HEADS)]
    ci = lax.broadcasted_iota(I32, (c_len, c_len), 0)
    cj = lax.broadcasted_iota(I32, (c_len, c_len), 1)
    tri_incl = (ci >= cj).astype(BF16)
    eye = (ci == cj).astype(F32)
    mi = lax.broadcasted_iota(I32, (8 * c_len, 2 * c_len), 0)
    mj = lax.broadcasted_iota(I32, (8 * c_len, 2 * c_len), 1)
    mt, ms = mi & (c_len - 1), mj & (c_len - 1)
    keep = ms < mt + jnp.where(mi < 4 * c_len, 0, 1)
    pick = lambda x: sum(x[h * c_len:(h + 1) * c_len] * hmask[h] for h in range(RW_HEADS))
    n_sq = int(math.log2(c_len)) - 1

    def chunk(c, _):
        sl = pl.ds(pl.multiple_of(c * c_len, c_len), c_len)
        lw, rc, knc, ac, k2c, vc = lw_s[sl, :], r_s[sl, :], kn_s[sl, :], a_s[sl, :], k2_s[sl, :], v_s[sl, :]
        lcum = _dot_exact_lhs(tri_incl, lw)
        gam, gex, gin = jnp.exp(lcum), jnp.exp(lcum - lw), jnp.exp(-lcum)
        rt, at, bt, kt = rc * gam, knc * gex, -(knc * ac) * gin, k2c * gin
        g_end = gam[c_len - 1:c_len, :]
        lhs = jnp.concatenate([at * hm for hm in hmask] + [rt * hm for hm in hmask], axis=0)
        m = jnp.where(keep, _dot3(lhs, jnp.concatenate([bt, kt], axis=0), _NT), 0.0)
        s = sbd[...]
        asrs = _dot3(jnp.concatenate([at, rt], axis=0), s, _NT)
        w1 = _dot3(m[:4 * c_len], jnp.concatenate([jnp.zeros_like(vc), vc], axis=0))
        wm = asrs[:c_len] + pick(w1)
        inv = []
        for h in range(RW_HEADS):
            q = m[h * c_len:(h + 1) * c_len, 0:c_len]
            x = eye + q
            for _ in range(n_sq):
                q = _dot3(q, q)
                x = x + _dot3(x, q)
            inv.append(x)
        p = pick(_dot3(jnp.concatenate(inv, axis=0), wm))
        pv = jnp.concatenate([p, vc], axis=0)
        y_s[sl, :] = asrs[c_len:] + pick(_dot3(m[4 * c_len:], pv))
        upd = _dot3(pv, jnp.concatenate([bt * g_end, kt * g_end], axis=0), _TN)
        sbd[...] = s * g_end + upd * blockdiag
        return 0

    lax.fori_loop(0, tb_rows // c_len, chunk, 0)

    y = y_s[...]
    mean = head_sum(y) * (1.0 / HEAD_DIM)
    yc = y - mean
    var = head_sum(yc * yc) * (1.0 / HEAD_DIM)
    yn = yc * lax.rsqrt(var + RW_GN_EPS) * lnw_ref[...] + lnb_ref[...]
    y_ref[0] = (yn + bonus) * g
    sout_ref[0] = sbd[...]


def _rwkv(u, prev, s0bd, prm, tb_rows, n_valid):
    b, t, _ = u.shape
    vec = lambda n: _full((1, n))
    kern = functools.partial(_rwkv_kernel, tb_rows=tb_rows, n_valid=n_valid)
    return pl.pallas_call(
        kern,
        grid=(b, t // tb_rows),
        in_specs=[pl.BlockSpec((1, tb_rows, 1024), lambda i, j: (i, j, 0)),
                  pl.BlockSpec((1, 1, 1024), lambda i, j: (i, 0, 0)),
                  pl.BlockSpec((1, 256, 256), lambda i, j: (i, 0, 0)),
                  vec(1024), vec(256), _full((128, 256)), vec(256), _full((128, 256)), _full((128, 256)),
                  vec(256), vec(256), vec(256), vec(256), vec(256)],
        out_specs=[pl.BlockSpec((1, tb_rows, 256), lambda i, j: (i, j, 0)),
                   pl.BlockSpec((1, 256, 256), lambda i, j: (i, 0, 0))],
        out_shape=[jax.ShapeDtypeStruct((b, t, 256), F32), jax.ShapeDtypeStruct((b, 256, 256), F32)],
        scratch_shapes=[pltpu.VMEM((256, 256), F32), pltpu.VMEM((1, 1024), F32)]
        + [pltpu.VMEM((tb_rows, 256), F32)] * 7,
        compiler_params=_params("parallel", "arbitrary"),
        name="rwkv7",
    )(u, prev, s0bd, *prm)


def _lru_kernel(u_ref, cb_ref, h0_ref, cw_ref, cbias_ref, wa_ref, ba_ref, wx_ref, bx_ref, lam_ref,
                out_ref, hl_ref, ext, hc, *, tb_rows, reset_first):
    tb = pl.program_id(1)

    @pl.when(tb == 0)
    def _():
        ext[0:8, :] = jnp.zeros((8, 256), F32)
        ext[5:8, :] = cb_ref[0]
        hc[...] = h0_ref[0]

    u = u_ref[0]
    xb, gate = u[:, :256], u[:, 256:]
    ext[8:8 + tb_rows, :] = xb
    cw = cw_ref[...]
    y = (cbias_ref[...] + cw[0:1] * ext[5:5 + tb_rows, :] + cw[1:2] * ext[6:6 + tb_rows, :]
         + cw[2:3] * ext[7:7 + tb_rows, :] + cw[3:4] * xb)
    ext[0:8, :] = ext[tb_rows:tb_rows + 8, :]
    gate_r = jax.nn.sigmoid(_dot3(y, wa_ref[...]) + ba_ref[...])
    gate_i = jax.nn.sigmoid(_dot3(y, wx_ref[...]) + bx_ref[...])
    lam = lam_ref[...]
    softplus_neg = jnp.maximum(-lam, 0.0) + jnp.log(1.0 + jnp.exp(-jnp.abs(lam)))
    log_a = (-LRU_C * softplus_neg) * gate_r
    a = jnp.exp(log_a)
    mult = jnp.sqrt(1.0 - jnp.exp(2.0 * log_a))
    row = lax.broadcasted_iota(I32, (tb_rows, 256), 0)
    if reset_first:
        mult = jnp.where((row == 0) & (tb == 0), 1.0, mult)
    bv = mult * gate_i * y
    d = 1
    while d < tb_rows:
        live = row >= d
        a_sh = jnp.where(live, pltpu.roll(a, d, 0), 1.0)
        b_sh = jnp.where(live, pltpu.roll(bv, d, 0), 0.0)
        bv = a * b_sh + bv
        a = a * a_sh
        d *= 2
    h = a * hc[...] + bv
    hc[...] = h[tb_rows - 1:tb_rows, :]
    gelu = 0.5 * gate * (1.0 + jnp.tanh(math.sqrt(2.0 / math.pi) * (gate + 0.044715 * gate * gate * gate)))
    out_ref[0] = h * gelu
    hl_ref[0] = h[tb_rows - 1:tb_rows, :]


def _lru(u, conv_buf, h0, prm, tb_rows, reset_first):
    b, t, _ = u.shape
    vec = _full((1, 256))
    kern = functools.partial(_lru_kernel, tb_rows=tb_rows, reset_first=reset_first)
    return pl.pallas_call(
        kern,
        grid=(b, t // tb_rows),
        in_specs=[pl.BlockSpec((1, tb_rows, 512), lambda i, j: (i, j, 0)),
                  pl.BlockSpec((1, 3, 256), lambda i, j: (i, 0, 0)),
                  pl.BlockSpec((1, 1, 256), lambda i, j: (i, 0, 0)),
                  _full((4, 256)), vec, _full((256, 256)), vec, _full((256, 256)), vec, vec],
        out_specs=[pl.BlockSpec((1, tb_rows, 256), lambda i, j: (i, j, 0)),
                   pl.BlockSpec((1, 1, 256), lambda i, j: (i, 0, 0))],
        out_shape=[jax.ShapeDtypeStruct((b, t, 256), F32), jax.ShapeDtypeStruct((b, 1, 256), F32)],
        scratch_shapes=[pltpu.VMEM((tb_rows + 8, 256), F32), pltpu.VMEM((1, 256), F32)],
        compiler_params=_params("parallel", "arbitrary"),
        name="rglru",
    )(u, conv_buf, h0, *prm)


def _sortable(x):
    bits = pltpu.bitcast(x, I32)
    return jnp.where(bits < 0, bits ^ 0x7FFFFFFF, bits)


def _bisect_threshold(count_gt, shape, topk):
    t0 = jnp.full(shape, INT_MIN, I32)

    def cond(c):
        it, _, cnt = c
        return (it < 32) & (jnp.max(cnt) > topk)

    def body(c):
        it, t, cnt = c
        cand = t + lax.shift_left(jnp.int32(1), 31 - it)
        cnt_cand = count_gt(cand)
        ok = cnt_cand >= topk
        return it + 1, jnp.where(ok, cand, t), jnp.where(ok, cnt_cand, cnt)

    return lax.while_loop(cond, body, (jnp.int32(0), t0, count_gt(t0)))[1]


def _dsa_prompt_steps(nq, qb_rows, kb_rows):
    steps = []
    for i in range(nq):
        nv = ((i + 1) * qb_rows + kb_rows - 1) // kb_rows
        steps += [(i, 0, j, nv, j, 0) for j in range(nv)] + [(i, 1, j, nv, nv - 1, j) for j in range(nv)]
    return list(zip(*steps))


def _dsa_prompt_kernel(tab_ref, qb_ref, qi_ref, wt_ref, ki4_ref, kb_ref, vt_ref, o_ref,
                       lhs_s, key_s, thr_s, qm_s, m_s, l_s, acc_s, *, qb_rows, kb_rows, sk, topk, n_steps):
    step = pl.program_id(0)
    i, ph, j, nv = (tab_ref[r * n_steps + step] for r in range(4))
    n_pairs = SA_HEADS // 2

    @pl.when((ph == 0) & (j == 0))
    def _init():
        qi, qb = qi_ref[...], qb_ref[...]
        lane = lax.broadcasted_iota(I32, (qb_rows, 128), 1)
        for h in range(SA_HEADS):
            hi, lo = _split_bf16(qi[:, h * 64:(h + 1) * 64])
            lhs_s[h * qb_rows:(h + 1) * qb_rows, :] = jnp.concatenate([hi, hi, lo, lo], axis=1)
        for p in range(n_pairs):
            qp = qb[:, p * 128:(p + 1) * 128]
            zero = jnp.zeros_like(qp)
            qm_s[p] = jnp.concatenate([jnp.where(lane < 64, qp, zero), jnp.where(lane < 64, zero, qp)], axis=0)
        m_s[...] = jnp.full(m_s.shape, NEG_BIG, F32)
        l_s[...] = jnp.zeros(l_s.shape, F32)
        acc_s[...] = jnp.zeros(acc_s.shape, F32)

    @pl.when(ph == 0)
    def _index():
        wt = wt_ref[...]
        qpos = i * qb_rows + lax.broadcasted_iota(I32, (sk, qb_rows), 1)
        for sub in range(kb_rows // sk):
            z = _dot_nt(ki4_ref[sub * sk:(sub + 1) * sk, :], lhs_s[...])
            sc = wt[0:1] * jnp.maximum(z[:, 0:qb_rows], 0.0)
            for h in range(1, SA_HEADS):
                sc = sc + wt[h:h + 1] * jnp.maximum(z[:, h * qb_rows:(h + 1) * qb_rows], 0.0)
            kpos = j * kb_rows + sub * sk + lax.broadcasted_iota(I32, (sk, qb_rows), 0)
            key_s[j, sub * sk:(sub + 1) * sk, :] = jnp.where(kpos <= qpos, _sortable(sc), INT_MIN)

    @pl.when((ph == 1) & (j == 0))
    def _threshold():
        def count_gt(t):
            def body(jb, acc):
                x = jnp.where(key_s[jb] > t, 1, 0)
                return acc + jnp.sum(x.reshape(kb_rows // 8, 8, qb_rows), axis=0)
            acc = lax.fori_loop(0, nv, body, jnp.zeros((8, qb_rows), I32))
            return jnp.sum(acc, axis=0, keepdims=True)

        thr_s[...] = _bisect_threshold(count_gt, (1, qb_rows), topk)

    @pl.when(ph == 1)
    def _attend():
        t = thr_s[...]
        for sub in range(kb_rows // sk):
            rows = slice(sub * sk, (sub + 1) * sk)
            bias = jnp.where(key_s[j, rows, :] > t, 0.0, NEG_BIG)
            bias = jnp.concatenate([bias, bias], axis=1)
            scores = [_dot_nt(kb_ref[rows, p * 128:(p + 1) * 128], qm_s[p]) for p in range(n_pairs)]
            for p in range(n_pairs):
                s = scores[p] + bias
                m_old = m_s[p]
                m_new = jnp.maximum(m_old, jnp.max(s, axis=0, keepdims=True))
                alpha = jnp.exp2(m_old - m_new)
                pr = jnp.exp2(s - m_new)
                l_s[p] = alpha * l_s[p] + jnp.sum(pr, axis=0, keepdims=True)
                m_s[p] = m_new
                prb = pr.astype(BF16)
                for hh in range(2):
                    h = 2 * p + hh
                    cols = slice(hh * qb_rows, (hh + 1) * qb_rows)
                    pv = _dot(vt_ref[h * 64:(h + 1) * 64, rows], prb[:, cols])
                    acc_s[p, hh * 64:(hh + 1) * 64, :] = alpha[:, cols] * acc_s[p, hh * 64:(hh + 1) * 64, :] + pv

    @pl.when((ph == 1) & (j == nv - 1))
    def _finish():
        for p in range(n_pairs):
            inv_l = 1.0 / l_s[p]
            a = acc_s[p]
            a = jnp.concatenate([a[:64] * inv_l[:, :qb_rows], a[64:] * inv_l[:, qb_rows:]], axis=0)
            o_ref[:, p * 128:(p + 1) * 128] = a.T


def _dsa_prompt(qb, qi, wt, ki4, kb, vt, qb_rows, kb_rows):
    s = qb.shape[0]
    nq, nk = s // qb_rows, s // kb_rows
    sk = min(512, kb_rows)
    tab = _dsa_prompt_steps(nq, qb_rows, kb_rows)
    n = len(tab[0])
    qrow = lambda w: pl.BlockSpec((qb_rows, w), lambda st, t: (t[st], 0))
    kern = functools.partial(_dsa_prompt_kernel, qb_rows=qb_rows, kb_rows=kb_rows, sk=sk, topk=min(TOPK, s // 4),
                             n_steps=n)
    grid_spec = pltpu.PrefetchScalarGridSpec(
        num_scalar_prefetch=1,
        grid=(n,),
        in_specs=[qrow(512), qrow(512),
                  pl.BlockSpec((SA_HEADS, qb_rows), lambda st, t: (0, t[st])),
                  pl.BlockSpec((kb_rows, 256), lambda st, t: (t[4 * n + st], 0)),
                  pl.BlockSpec((kb_rows, 512), lambda st, t: (t[5 * n + st], 0)),
                  pl.BlockSpec((512, kb_rows), lambda st, t: (0, t[5 * n + st]))],
        out_specs=qrow(512),
        scratch_shapes=[pltpu.VMEM((SA_HEADS * qb_rows, 256), BF16),
                        pltpu.VMEM((nk, kb_rows, qb_rows), I32),
                        pltpu.VMEM((1, qb_rows), I32),
                        pltpu.VMEM((SA_HEADS // 2, 2 * qb_rows, 128), BF16),
                        pltpu.VMEM((SA_HEADS // 2, 1, 2 * qb_rows), F32),
                        pltpu.VMEM((SA_HEADS // 2, 1, 2 * qb_rows), F32),
                        pltpu.VMEM((SA_HEADS // 2, 128, qb_rows), F32)],
    )
    return pl.pallas_call(
        kern,
        grid_spec=grid_spec,
        out_shape=jax.ShapeDtypeStruct((s, 512), F32),
        compiler_params=_params("arbitrary"),
        name="dsa_prompt",
    )(jnp.asarray(tab, I32).reshape(-1), qb, qi, wt, ki4, kb, vt)


PAGES_PER_STEP = 16


def _dsa_sample_index_kernel(pt_ref, qi_ref, kw_ref, ki4n_ref, *rest, n_pages, t_len, topk):
    pages = rest[:PAGES_PER_STEP]
    key_ref, thr_ref, lhs_s, wcol_s = rest[PAGES_PER_STEP:]
    j = pl.program_id(1)
    n_steps = n_pages // PAGES_PER_STEP

    @pl.when(j == 0)
    def _init():
        qi, kw = qi_ref[0], kw_ref[0]
        hi, lo = _split_bf16(jnp.concatenate([qi[:, h * 64:(h + 1) * 64] for h in range(SA_HEADS)], axis=0))
        lhs_s[...] = jnp.concatenate([hi, hi, lo, lo], axis=1)
        for h in range(SA_HEADS):
            wcol_s[h] = kw[:, 64 + h:65 + h]

    def head_sum(z):
        sc = wcol_s[0] * jnp.maximum(z[0:t_len], 0.0)
        for h in range(1, SA_HEADS):
            sc = sc + wcol_s[h] * jnp.maximum(z[h * t_len:(h + 1) * t_len], 0.0)
        return sc

    hi, lo = _split_bf16(jnp.concatenate([p[...] for p in pages], axis=1))
    keys = _sortable(head_sum(_dot(lhs_s[...], jnp.concatenate([hi, lo, hi, lo], axis=0))))
    for r in range(PAGES_PER_STEP):
        key_ref[0, j * PAGES_PER_STEP + r] = keys[:, r * PAGE:(r + 1) * PAGE]

    @pl.when(j == n_steps - 1)
    def _finish():
        sc = head_sum(_dot_nt(lhs_s[...], ki4n_ref[0]))
        tq = lax.broadcasted_iota(I32, (t_len, 128), 0)
        ts = lax.broadcasted_iota(I32, (t_len, 128), 1)
        key_ref[0, n_pages] = jnp.where(ts <= tq, _sortable(sc), INT_MIN)

        def count_gt(t):
            x = jnp.where(key_ref[0] > t[None], 1, 0)
            return jnp.sum(jnp.sum(x, axis=0), axis=1, keepdims=True)

        thr = _bisect_threshold(count_gt, (t_len, 1), topk)
        thr_ref[0] = jnp.broadcast_to(thr, (t_len, 128))


def _dsa_sample_index(page_table, qi, kw, ki4_new, cache_kidx_t, layer, t_len):
    b = qi.shape[0]
    n_pages = page_table.shape[1]
    n_steps = n_pages // PAGES_PER_STEP

    def page_spec(r):
        return pl.BlockSpec((None, None, 64, PAGE),
                            lambda i, j, pt: (layer, pt[i * n_pages + j * PAGES_PER_STEP + r], 0, 0))

    kern = functools.partial(_dsa_sample_index_kernel, n_pages=n_pages, t_len=t_len,
                             topk=min(TOPK, (n_pages * PAGE + t_len) // 4))
    grid_spec = pltpu.PrefetchScalarGridSpec(
        num_scalar_prefetch=1,
        grid=(b, n_steps),
        in_specs=[pl.BlockSpec((1, t_len, 512), lambda i, j, pt: (i, 0, 0)),
                  pl.BlockSpec((1, t_len, 128), lambda i, j, pt: (i, 0, 0)),
                  pl.BlockSpec((1, PAGE, 256), lambda i, j, pt: (i, 0, 0))]
        + [page_spec(r) for r in range(PAGES_PER_STEP)],
        out_specs=[pl.BlockSpec((1, n_pages + 1, t_len, 128), lambda i, j, pt: (i, 0, 0, 0)),
                   pl.BlockSpec((1, t_len, 128), lambda i, j, pt: (i, 0, 0))],
        scratch_shapes=[pltpu.VMEM((SA_HEADS * t_len, 256), BF16), pltpu.VMEM((SA_HEADS, t_len, 1), F32)],
    )
    return pl.pallas_call(
        kern,
        grid_spec=grid_spec,
        out_shape=[jax.ShapeDtypeStruct((b, n_pages + 1, t_len, 128), I32),
                   jax.ShapeDtypeStruct((b, t_len, 128), I32)],
        compiler_params=_params("parallel", "arbitrary"),
        name="dsa_sample_index",
    )(page_table.reshape(-1), qi, kw, ki4_new, *([cache_kidx_t] * PAGES_PER_STEP))


def _dsa_sample_attend_kernel(pt_ref, qb_ref, key_ref, keyn_ref, thr_ref, kn_ref, vn_ref, *rest, n_pages, t_len):
    kpages = rest[:PAGES_PER_STEP]
    vpages = rest[PAGES_PER_STEP:2 * PAGES_PER_STEP]
    o_ref, qbd_s, m_s, l_s, acc_s = rest[2 * PAGES_PER_STEP:]
    j = pl.program_id(1)
    n_steps = n_pages // PAGES_PER_STEP
    rows = SA_HEADS * t_len

    @pl.when(j == 0)
    def _init():
        qb = qb_ref[0].astype(F32)
        lane = lax.broadcasted_iota(I32, (t_len, 512), 1) // 64
        qbd = jnp.concatenate([jnp.where(lane == h, qb, 0.0) for h in range(SA_HEADS)], axis=0)
        qbd_s[...] = qbd.astype(BF16)
        m_s[...] = jnp.full(m_s.shape, NEG_BIG, F32)
        l_s[...] = jnp.zeros(l_s.shape, F32)
        acc_s[...] = jnp.zeros(acc_s.shape, F32)

    thr = thr_ref[0][:, 0:1]

    def update(keys, s, pv):
        bias = jnp.where(keys > thr, 0.0, NEG_BIG)
        s = s + jnp.concatenate([bias] * SA_HEADS, axis=0)
        m_old = m_s[...]
        m_new = jnp.maximum(m_old, jnp.max(s, axis=1, keepdims=True))
        alpha = jnp.exp2(m_old - m_new)
        pr = jnp.exp2(s - m_new)
        l_s[...] = alpha * l_s[...] + jnp.sum(pr, axis=1, keepdims=True)
        acc_s[...] = alpha * acc_s[...] + pv(pr.astype(BF16))
        m_s[...] = m_new

    kcat = jnp.concatenate([p[...].astype(BF16) for p in kpages], axis=1)
    vcat = jnp.concatenate([p[...].astype(BF16) for p in vpages], axis=1)
    keys = jnp.concatenate([key_ref[0, r] for r in range(PAGES_PER_STEP)], axis=1)
    update(keys, _dot(qbd_s[...], kcat), lambda pr: _dot_nt(pr, vcat))

    @pl.when(j == n_steps - 1)
    def _finish():
        kn, vn = kn_ref[0].astype(BF16), vn_ref[0].astype(BF16)
        update(keyn_ref[0, 0], _dot_nt(qbd_s[...], kn), lambda pr: _dot(pr, vn))
        o = acc_s[...] / l_s[...]
        lane = lax.broadcasted_iota(I32, (t_len, 512), 1) // 64
        out = jnp.zeros((t_len, 512), F32)
        for h in range(SA_HEADS):
            out = out + jnp.where(lane == h, o[h * t_len:(h + 1) * t_len], 0.0)
        o_ref[0] = out


def _dsa_sample_attend(page_table, qb, keys, thr, k_new, v_new, cache_k_t, cache_v_t, layer, t_len):
    b = qb.shape[0]
    n_pages = page_table.shape[1]
    n_steps = n_pages // PAGES_PER_STEP
    rows = SA_HEADS * t_len

    def page_spec(r):
        return pl.BlockSpec((None, None, 512, PAGE),
                            lambda i, j, pt: (layer, pt[i * n_pages + j * PAGES_PER_STEP + r], 0, 0))

    per_b = lambda n: pl.BlockSpec((1, t_len, n), lambda i, j, pt: (i, 0, 0))
    kern = functools.partial(_dsa_sample_attend_kernel, n_pages=n_pages, t_len=t_len)
    grid_spec = pltpu.PrefetchScalarGridSpec(
        num_scalar_prefetch=1,
        grid=(b, n_steps),
        in_specs=[per_b(512),
                  pl.BlockSpec((1, PAGES_PER_STEP, t_len, 128), lambda i, j, pt: (i, j, 0, 0)),
                  pl.BlockSpec((1, 1, t_len, 128), lambda i, j, pt: (i, n_pages, 0, 0)),
                  per_b(128),
                  pl.BlockSpec((1, PAGE, 512), lambda i, j, pt: (i, 0, 0)),
                  pl.BlockSpec((1, PAGE, 512), lambda i, j, pt: (i, 0, 0))]
        + [page_spec(r) for r in range(PAGES_PER_STEP)] * 2,
        out_specs=per_b(512),
        scratch_shapes=[pltpu.VMEM((rows, 512), BF16), pltpu.VMEM((rows, 1), F32), pltpu.VMEM((rows, 1), F32),
                        pltpu.VMEM((rows, 512), F32)],
    )
    return pl.pallas_call(
        kern,
        grid_spec=grid_spec,
        out_shape=jax.ShapeDtypeStruct((b, t_len, 512), F32),
        compiler_params=_params("parallel", "arbitrary"),
        name="dsa_sample_attend",
    )(page_table.reshape(-1), qb, keys, keys, thr, k_new, v_new,
      *([cache_k_t] * PAGES_PER_STEP), *([cache_v_t] * PAGES_PER_STEP))


def _matmul_kernel(x_ref, w_ref, o_ref):
    o_ref[...] = _dot(x_ref[...].astype(BF16), w_ref[...])


def _matmul(x, w):
    m, n = x.shape[0], w.shape[1]
    return pl.pallas_call(
        _matmul_kernel,
        grid=(1,),
        in_specs=[_full(x.shape), _full(w.shape)],
        out_specs=_full((m, n)),
        out_shape=jax.ShapeDtypeStruct((m, n), F32),
        compiler_params=_params("arbitrary"),
        name="mem_kv_proj",
    )(x, w)


def _post1_kernel(x_ref, yrw_ref, ysa_ref, ylru_ref, wo1_ref, wo2_ref, wo3_ref, g_ref, wq_ref, x1_ref, qm_ref):
    x1 = (x_ref[...] + _dot(yrw_ref[...].astype(BF16), wo1_ref[...])
          + _dot(ysa_ref[...].astype(BF16), wo2_ref[...]) + _dot(ylru_ref[...].astype(BF16), wo3_ref[...]))
    x1_ref[...] = x1
    qm = _dot(_rms(x1, g_ref[...]).astype(BF16), wq_ref[...])
    qm_ref[...] = (qm * (MEM_HEAD_DIM ** -0.5)).astype(BF16)


def _post1(x, yrw, ysa, ylru, wo1, wo2, wo3, g, wq, tm):
    m = x.shape[0]
    row = lambda n: pl.BlockSpec((tm, n), lambda i: (i, 0))
    return pl.pallas_call(
        _post1_kernel,
        grid=(m // tm,),
        in_specs=[row(1024), row(256), row(512), row(256), _full(wo1.shape), _full(wo2.shape), _full(wo3.shape),
                  _full((1, 1024)), _full(wq.shape)],
        out_specs=[row(1024), row(1024)],
        out_shape=[jax.ShapeDtypeStruct((m, 1024), F32), jax.ShapeDtypeStruct((m, 1024), BF16)],
        compiler_params=_params("parallel"),
        name="out_proj_memq",
    )(x, yrw, ysa, ylru, wo1, wo2, wo3, g, wq)


def _mem_attn_kernel(q_ref, mk_ref, mv_ref, o_ref):
    q = q_ref[0]
    mk = mk_ref[0].astype(BF16)
    mv = mv_ref[0].astype(BF16)
    for h in range(MEM_HEADS):
        sl = slice(h * MEM_HEAD_DIM, (h + 1) * MEM_HEAD_DIM)
        s = _dot_nt(q[:, sl], mk[:, sl])
        p = jnp.exp(s - jnp.max(s, axis=1, keepdims=True))
        o = _dot(p.astype(BF16), mv[:, sl]) / jnp.sum(p, axis=1, keepdims=True)
        o_ref[0, :, sl] = o.astype(BF16)


def _mem_attn(q, mk, mv, tm):
    b, t, _ = q.shape
    return pl.pallas_call(
        _mem_attn_kernel,
        grid=(b, t // tm),
        in_specs=[pl.BlockSpec((1, tm, 1024), lambda i, j: (i, j, 0)),
                  pl.BlockSpec((1, 256, 1024), lambda i, j: (i, 0, 0)),
                  pl.BlockSpec((1, 256, 1024), lambda i, j: (i, 0, 0))],
        out_specs=pl.BlockSpec((1, tm, 1024), lambda i, j: (i, j, 0)),
        out_shape=jax.ShapeDtypeStruct((b, t, 1024), BF16),
        compiler_params=_params("parallel", "parallel"),
        name="mem_attn",
    )(q, mk, mv)


def _post2_kernel(x1_ref, o_ref, wo_ref, g_ref, wg_ref, wu_ref, wd_ref, gf_ref, out_ref, *, last):
    x2 = x1_ref[...] + _dot(o_ref[...], wo_ref[...])
    hb = _rms(x2, g_ref[...]).astype(BF16)
    gt = _dot(hb, wg_ref[...])
    up = _dot(hb, wu_ref[...])
    act = (gt * jax.nn.sigmoid(gt) * up).astype(BF16)
    x3 = x2 + _dot(act, wd_ref[...])
    out_ref[...] = _rms(x3, gf_ref[...]) if last else x3


def _post2(x1, o, wo, g, wg, wu, wd, gf, tm, last):
    m = x1.shape[0]
    row = lambda n: pl.BlockSpec((tm, n), lambda i: (i, 0))
    once = lambda w: pl.BlockSpec(w.shape, lambda i: (0, 0), pipeline_mode=pl.Buffered(1))
    return pl.pallas_call(
        functools.partial(_post2_kernel, last=last),
        grid=(m // tm,),
        in_specs=[row(1024), row(1024), once(wo), _full((1, 1024)), once(wg), once(wu), once(wd), _full((1, 1024))],
        out_specs=row(1024),
        out_shape=jax.ShapeDtypeStruct((m, 1024), F32),
        compiler_params=_params("parallel"),
        name="memo_swiglu",
    )(x1, o, wo, g, wg, wu, wd, gf)


def _blockdiag4(w):
    n = w.shape[-1]
    return jnp.einsum("gij,gh->gihj", w, jnp.eye(4, dtype=w.dtype)).reshape(4 * n, 4 * n)


def _state_to_blockdiag(s):
    b = s.shape[0]
    return jnp.einsum("bhvk,hg->bhvgk", s, jnp.eye(4, dtype=s.dtype)).reshape(b, 256, 256)


def _blockdiag_to_state(sbd):
    b = sbd.shape[0]
    return jnp.einsum("bhvgk,hg->bhvk", sbd.reshape(b, 4, 64, 4, 64), jnp.eye(4, dtype=sbd.dtype))


def _layer_weights(l, w_in, w_out, rw, lru, g_mix, g_mem, w_mq, w_mk, w_mv, w_mo, g_ffn, w_gate, w_up, w_down):
    wi = w_in[l]
    o = RW_COLS
    cut = lambda a, n: wi[:, a:a + n].astype(BF16)
    wkw = jnp.pad(wi[:, o + 2048:o + 2120], ((0, 0), (0, 56))).astype(BF16)
    in_ws = (cut(0, 1024), cut(o, 512), cut(o + 512, 512), cut(o + 1024, 512), cut(o + 1536, 512), wkw,
             cut(o + 2120, 512))
    (rw_mu, rw_w0, rw_w2, rw_a0, rw_a2, rw_g2, rw_kk, rw_ka, rw_rk, rw_ln_w, rw_ln_b) = rw
    z = jnp.zeros((64, 256), F32)
    v256 = lambda a: a[l].reshape(1, 256)
    rw_prm = (rw_mu[l].reshape(1, 1024), v256(rw_w0), jnp.concatenate([rw_w2[l], z], axis=0), v256(rw_a0),
              jnp.concatenate([z, rw_a2[l]], axis=0), rw_g2[l], v256(rw_kk), v256(rw_ka), v256(rw_rk),
              v256(rw_ln_w), v256(rw_ln_b))
    (lru_conv_w, lru_conv_b, lru_wa, lru_ba, lru_wx, lru_bx, lru_lambda) = lru
    lru_prm = (lru_conv_w[l], v256(lru_conv_b), _blockdiag4(lru_wa[l]), v256(lru_ba), _blockdiag4(lru_wx[l]),
               v256(lru_bx), v256(lru_lambda))
    wo = w_out[l].astype(BF16)
    return dict(
        in_ws=in_ws, rw=rw_prm, lru=lru_prm, g_mix=g_mix[l].reshape(1, 1024),
        wo=(wo[:256], wo[256:768], wo[768:]), g_mem=g_mem[l].reshape(1, 1024), wq=w_mq[l].astype(BF16),
        wmk=w_mk[l].astype(BF16), wmv=w_mv[l].astype(BF16), wmo=w_mo[l].astype(BF16),
        g_ffn=g_ffn[l].reshape(1, 1024), wg=w_gate[l].astype(BF16), wu=w_up[l].astype(BF16),
        wd=w_down[l].astype(BF16))


def _pick_tile(n, pref):
    t = min(n, pref)
    while n % t:
        t //= 2
    return t


def _prompt_layer(x, mem, w, tabs, g_final, last):
    s = x.shape[0]
    tm = _pick_tile(s, 256)
    urw, k, v, qi, kw, ulru, qb, kb, vt, ki4 = _in_proj(x, w["g_mix"], w["in_ws"], tabs, tm)
    tb = _pick_tile(s, 512)
    y_rw, sbd = _rwkv(urw[None], jnp.zeros((1, 1, 1024), F32), jnp.zeros((1, 256, 256), F32), w["rw"], tb, tb)
    y_lru, h_last = _lru(ulru[None], jnp.zeros((1, 3, 256), F32), jnp.zeros((1, 1, 256), F32), w["lru"], tb, True)
    y_sa = _dsa_prompt(qb, qi, kw[:, 64:64 + SA_HEADS].T, ki4, kb, vt, _pick_tile(s, 256), _pick_tile(s, 1024))
    mk = _matmul(mem, w["wmk"])
    mv = _matmul(mem, w["wmv"])
    x1, qm = _post1(x, y_rw[0], y_sa, y_lru[0], *w["wo"], w["g_mem"], w["wq"], tm)
    o = _mem_attn(qm[None], mk[None], mv[None], tm)
    x3 = _post2(x1, o[0], w["wmo"], w["g_ffn"], w["wg"], w["wu"], w["wd"], g_final, tm, last)
    new = dict(k=k.reshape(1, s, 8, 64), v=v.reshape(1, s, 8, 64), kidx=kw[None, :, :64],
               mk=mk.reshape(1, 256, 4, 256), mv=mv.reshape(1, 256, 4, 256), rwkv=_blockdiag_to_state(sbd),
               shift=urw[None, s - 1], h=h_last[:, 0], conv=ulru[None, s - 3:, :256])
    return x3, new


def _sample_layer(x, l, w, tabs, g_final, cache_k, cache_v, cache_kidx, cache_mem_k, cache_mem_v, state_rwkv,
                  state_rwkv_shift, state_lru_h, state_lru_conv, page_table, db, t, last):
    m = db * t
    urw, k, v, qi, kw, ulru, qb, _, _, ki4 = _in_proj(x, w["g_mix"], w["in_ws"], tabs, m)
    urw3 = urw.reshape(db, t, 1024)
    u_pad = jnp.pad(urw3, ((0, 0), (0, RW_CHUNK - t), (0, 0)))
    y_rw, sbd = _rwkv(u_pad, state_rwkv_shift[l][:, None], _state_to_blockdiag(state_rwkv[l]), w["rw"], RW_CHUNK, t)
    ulru3 = ulru.reshape(db, t, 512)
    y_lru, h_last = _lru(ulru3, state_lru_conv[l], state_lru_h[l][:, None], w["lru"], t, False)
    pad_new = lambda a: jnp.pad(a.reshape(db, t, -1), ((0, 0), (0, PAGE - t), (0, 0)))
    keys, thr = _dsa_sample_index(page_table, qi.reshape(db, t, 512), kw.reshape(db, t, 128), pad_new(ki4),
                                  cache_kidx, l, t)
    y_sa = _dsa_sample_attend(page_table, qb.reshape(db, t, 512), keys, thr, pad_new(k), pad_new(v),
                              cache_k, cache_v, l, t)
    x1, qm = _post1(x, y_rw[:, :t].reshape(m, 256), y_sa.reshape(m, 512), y_lru.reshape(m, 256), *w["wo"],
                    w["g_mem"], w["wq"], m)
    o = _mem_attn(qm.reshape(db, t, 1024), cache_mem_k[l].reshape(db, 256, 1024),
                  cache_mem_v[l].reshape(db, 256, 1024), t)
    x3 = _post2(x1, o.reshape(m, 1024), w["wmo"], w["g_ffn"], w["wg"], w["wu"], w["wd"], g_final, m, last)
    conv = jnp.concatenate([state_lru_conv[l], ulru3[:, :, :256]], axis=1)[:, t:]
    new = dict(k=k.reshape(db, t, 8, 64), v=v.reshape(db, t, 8, 64), kidx=kw.reshape(db, t, 128)[:, :, :64],
               rwkv=_blockdiag_to_state(sbd), shift=urw3[:, t - 1], h=h_last[:, 0], conv=conv)
    return x3, new


def kernel(x_prompt, x_sample, mem_prompt, cache_k, cache_v, cache_kidx, cache_mem_k, cache_mem_v, state_rwkv, state_rwkv_shift, state_lru_h, state_lru_conv, page_table, g_mix, w_in, w_out, rw_mu, rw_w0, rw_w2, rw_a0, rw_a2, rw_g2, rw_kk, rw_ka, rw_rk, rw_ln_w, rw_ln_b, lru_conv_w, lru_conv_b, lru_wa, lru_ba, lru_wx, lru_bx, lru_lambda, g_mem, w_mq, w_mk, w_mv, w_mo, g_ffn, w_gate, w_up, w_down, g_final):
    depth = w_in.shape[0]
    _, s, _ = x_prompt.shape
    db, t, _ = x_sample.shape
    past_len = page_table.shape[1] * PAGE
    rw = (rw_mu, rw_w0, rw_w2, rw_a0, rw_a2, rw_g2, rw_kk, rw_ka, rw_rk, rw_ln_w, rw_ln_b)
    lru = (lru_conv_w, lru_conv_b, lru_wa, lru_ba, lru_wx, lru_bx, lru_lambda)
    tabs_p = _rope_tables(jnp.arange(s))
    tabs_s = _rope_tables(jnp.tile(past_len + jnp.arange(t), db))
    gf = g_final.reshape(1, 1024)
    n_pool = cache_k.shape[1]
    cache_kidx = jnp.transpose(cache_kidx, (0, 1, 3, 2))
    cache_k = jnp.transpose(cache_k, (0, 1, 3, 4, 2)).reshape(depth, n_pool, SA_WIDTH, PAGE)
    cache_v = jnp.transpose(cache_v, (0, 1, 3, 4, 2)).reshape(depth, n_pool, SA_WIDTH, PAGE)
    xp, xs = x_prompt[0], x_sample.reshape(db * t, 1024)
    mem = mem_prompt[0]
    news_p, news_s = [], []
    for l in range(depth):
        w = _layer_weights(l, w_in, w_out, rw, lru, g_mix, g_mem, w_mq, w_mk, w_mv, w_mo, g_ffn, w_gate, w_up,
                           w_down)
        last = l == depth - 1
        xp, new_p = _prompt_layer(xp, mem, w, tabs_p, gf, last)
        xs, new_s = _sample_layer(xs, l, w, tabs_s, gf, cache_k, cache_v, cache_kidx, cache_mem_k, cache_mem_v,
                                  state_rwkv, state_rwkv_shift, state_lru_h, state_lru_conv, page_table, db, t, last)
        news_p.append(new_p)
        news_s.append(new_s)
    stk = lambda news, name: jnp.stack([n[name] for n in news])
    return (xp[None], xs.reshape(db, t, 1024),
            stk(news_p, "k"), stk(news_p, "v"), stk(news_p, "kidx"), stk(news_p, "mk"), stk(news_p, "mv"),
            stk(news_p, "rwkv"), stk(news_p, "shift"), stk(news_p, "h"), stk(news_p, "conv"),
            stk(news_s, "k"), stk(news_s, "v"), stk(news_s, "kidx"), stk(news_s, "rwkv"), stk(news_s, "shift"),
            stk(news_s, "h"), stk(news_s, "conv"))
```

```python
import functools
import math

import jax
import jax.numpy as jnp
from jax import lax
from jax.experimental import pallas as pl
from jax.experimental.pallas import tpu as pltpu

F32 = jnp.float32
BF16 = jnp.bfloat16
I32 = jnp.int32

D_MODEL = 1024
HEAD_DIM = 64
RW_HEADS = 4
RW_WIDTH = 256
RW_COLS = 1024
RW_GN_EPS = 64e-5
SA_HEADS = 8
SA_WIDTH = 512
IDX_SCALE = 512.0 ** -0.5
TOPK = 256
LRU_WIDTH = 256
LRU_C = 8.0
D_FF = 2816
MEM_HEADS = 4
MEM_HEAD_DIM = 256
ROPE_THETA = 10000.0
NORM_EPS = 1e-6
PAGE = 128
LOG2E = 1.4426950408889634
INT_MIN = -2 ** 31
F32_MIN_NORMAL = 2.0 ** -126
NEG_BIG = -1e30
VMEM_LIMIT = 56 * 1024 * 1024
RW_CHUNK = 64


def _params(*sem):
    return pltpu.CompilerParams(dimension_semantics=sem, vmem_limit_bytes=VMEM_LIMIT)


def _full(shape):
    n = len(shape)
    return pl.BlockSpec(shape, lambda *_: (0,) * n)


def _dot(a, b):
    return jnp.dot(a, b, preferred_element_type=F32)


def _dot_nt(a, b):
    return lax.dot_general(a, b, (((1,), (1,)), ((), ())), preferred_element_type=F32)


def _rms(x, g):
    return x * lax.rsqrt(jnp.mean(x * x, axis=-1, keepdims=True) + NORM_EPS) * g


def _split_bf16(x):
    hi = x.astype(BF16)
    lo = (x - hi.astype(F32)).astype(BF16)
    return hi, lo


_NN = (((1,), (0,)), ((), ()))
_NT = (((1,), (1,)), ((), ()))
_TN = (((0,), (0,)), ((), ()))


def _dot3(a, b, dims=_NN):
    ah, al = _split_bf16(a)
    bh, bl = _split_bf16(b)
    d = lambda x, y: lax.dot_general(x, y, dims, preferred_element_type=F32)
    return d(ah, bh) + (d(ah, bl) + d(al, bh))


def _dot_exact_rhs(a, m):
    hi = a.astype(BF16)
    r = a - hi.astype(F32)
    mid = r.astype(BF16)
    lo = (r - mid.astype(F32)).astype(BF16)
    return _dot(hi, m) + (_dot(mid, m) + _dot(lo, m))


def _dot_exact_lhs(m, b):
    hi = b.astype(BF16)
    r = b - hi.astype(F32)
    mid = r.astype(BF16)
    lo = (r - mid.astype(F32)).astype(BF16)
    return _dot(m, hi) + (_dot(m, mid) + _dot(m, lo))


def _rope(u, cos, sin):
    w = u.shape[-1]
    lane = lax.broadcasted_iota(I32, u.shape, 1)
    rot = jnp.where((lane & 63) < 32, pltpu.roll(u, w - 32, 1), pltpu.roll(u, 32, 1))
    return u * cos + rot * sin


def _in_proj_kernel(x_ref, g_ref, wrw_ref, wq_ref, wk_ref, wv_ref, wqi_ref, wkw_ref, wlru_ref,
                    cos_ref, sin_ref, ckw_ref, skw_ref,
                    urw_ref, kt_ref, vt_ref, qi_ref, kw_ref, kwt_ref, ulru_ref, qb_ref, kb_ref, vtb_ref, ki4_ref):
    hb = _rms(x_ref[...], g_ref[...]).astype(BF16)
    urw_ref[...] = _dot(hb, wrw_ref[...])
    ulru_ref[...] = _dot(hb, wlru_ref[...])
    cos = jnp.concatenate([cos_ref[...]] * 4, axis=1)
    sin = jnp.concatenate([sin_ref[...]] * 4, axis=1)
    q = _rope(_dot(hb, wq_ref[...]), cos, sin)
    k = _rope(_dot(hb, wk_ref[...]), cos, sin)
    v = _dot(hb, wv_ref[...])
    qi_ref[...] = _rope(_dot(hb, wqi_ref[...]), cos, sin)
    kw = _rope(_dot(hb, wkw_ref[...]), ckw_ref[...], skw_ref[...])
    vt = v.T
    kt_ref[...] = k.T
    vt_ref[...] = vt
    kw_ref[...] = kw
    kwt_ref[...] = kw.T
    qb_ref[...] = (q * (HEAD_DIM ** -0.5 * LOG2E)).astype(BF16)
    kb_ref[...] = k.astype(BF16)
    vtb_ref[...] = vt.astype(BF16)
    hi, lo = _split_bf16(kw[:, :64])
    ki4_ref[...] = jnp.concatenate([hi, lo, hi, lo], axis=1)


def _in_proj(x, g, ws, tabs, tm):
    m = x.shape[0]
    wrw, wq, wk, wv, wqi, wkw, wlru = ws
    row = lambda n: pl.BlockSpec((tm, n), lambda i: (i, 0))
    col = lambda n: pl.BlockSpec((n, tm), lambda i: (0, i))
    sds = jax.ShapeDtypeStruct
    out_shape = [sds((m, 1024), F32), sds((512, m), F32), sds((512, m), F32), sds((m, 512), F32),
                 sds((m, 128), F32), sds((128, m), F32), sds((m, 512), F32), sds((m, 512), BF16),
                 sds((m, 512), BF16), sds((512, m), BF16), sds((m, 256), BF16)]
    return pl.pallas_call(
        _in_proj_kernel,
        grid=(m // tm,),
        in_specs=[row(1024), _full((1, 1024))] + [_full(w.shape) for w in ws] + [row(128)] * 4,
        out_specs=[row(1024), col(512), col(512), row(512), row(128), col(128), row(512), row(512), row(512),
                   col(512), row(256)],
        out_shape=out_shape,
        compiler_params=_params("parallel"),
        name="in_proj",
    )(x, g, wrw, wq, wk, wv, wqi, wkw, wlru, *tabs)


def _rope_tables(pos):
    half = HEAD_DIM // 2
    inv = ROPE_THETA ** (-jnp.arange(half, dtype=F32) / half)
    ang = pos.astype(F32)[:, None] * inv[None, :]
    c, s = jnp.cos(ang), jnp.sin(ang)
    m = pos.shape[0]
    cos = jnp.concatenate([c, c, c, c], axis=1)
    sin = jnp.concatenate([-s, s, -s, s], axis=1)
    ckw = jnp.concatenate([c, c, jnp.full((m, SA_HEADS), IDX_SCALE, F32), jnp.zeros((m, 56), F32)], axis=1)
    skw = jnp.concatenate([-s, s, jnp.zeros((m, 64), F32)], axis=1)
    return cos, sin, ckw, skw


def _rwkv_kernel(u_ref, prev_ref, s0_ref, mu_ref, w0_ref, w2_ref, a0_ref, a2_ref, g2_ref, kk_ref, ka_ref,
                 rk_ref, lnw_ref, lnb_ref, y_ref, sout_ref,
                 sbd, carry, r_s, kn_s, a_s, k2_s, v_s, lw_s, y_s, *, tb_rows, n_valid):
    c_len = RW_CHUNK
    tb = pl.program_id(1)

    @pl.when(tb == 0)
    def _():
        sbd[...] = s0_ref[0]
        carry[...] = prev_ref[0]

    u = u_ref[0]
    row = lax.broadcasted_iota(I32, u.shape, 0)
    u_prev = jnp.where(row == 0, carry[...], pltpu.roll(u, 1, 0))
    last = min(n_valid, tb_rows) - 1
    carry[...] = u[last:last + 1, :]
    us = u + mu_ref[...] * (u_prev - u)
    r, k, v = us[:, 0:256], us[:, 256:512], us[:, 512:768]
    wa, gd = us[:, 768:896], us[:, 896:1024]

    li = lax.broadcasted_iota(I32, (256, 256), 0) // 64
    lj = lax.broadcasted_iota(I32, (256, 256), 1) // 64
    blockdiag = (li == lj).astype(F32)
    head_sum = functools.partial(_dot_exact_rhs, m=blockdiag.astype(BF16))

    xw = w0_ref[...] + _dot3(jnp.tanh(wa), w2_ref[...])
    logw = -math.exp(-0.5) * jax.nn.sigmoid(xw)
    a = jax.nn.sigmoid(a0_ref[...] + _dot3(wa, a2_ref[...]))
    g = _dot3(jax.nn.sigmoid(gd), g2_ref[...])
    kkv = k * kk_ref[...]
    kn = kkv / jnp.maximum(jnp.sqrt(head_sum(kkv * kkv)), 1e-12)
    k2 = k * (1.0 + (a - 1.0) * ka_ref[...])
    bonus = head_sum(r * k2 * rk_ref[...]) * v
    if n_valid < tb_rows:
        ok = lax.broadcasted_iota(I32, (tb_rows, 256), 0) < n_valid
        logw = jnp.where(ok, logw, 0.0)
        kn = jnp.where(ok, kn, 0.0)
        k2 = jnp.where(ok, k2, 0.0)
        v = jnp.where(ok, v, 0.0)
    r_s[...] = r
    kn_s[...] = kn
    a_s[...] = a
    k2_s[...] = k2
    v_s[...] = v
    lw_s[...] = logw

    lane = lax.broadcasted_iota(I32, (1, 256), 1) // 64
    hmask = [(lane == h).astype(F32) for h in range(RW_HEADS)]
    ci = lax.broadcasted_iota(I32, (c_len, c_len), 0)
    cj = lax.broadcasted_iota(I32, (c_len, c_len), 1)
    tri_incl = (ci >= cj).astype(BF16)
    eye = (ci == cj).astype(F32)
    mi = lax.broadcasted_iota(I32, (8 * c_len, 2 * c_len), 0)
    mj = lax.broadcasted_iota(I32, (8 * c_len, 2 * c_len), 1)
    mt, ms = mi & (c_len - 1), mj & (c_len - 1)
    keep = ms < mt + jnp.where(mi < 4 * c_len, 0, 1)
    pick = lambda x: sum(x[h * c_len:(h + 1) * c_len] * hmask[h] for h in range(RW_HEADS))
    n_sq = int(math.log2(c_len)) - 1

    def chunk(c, _):
        sl = pl.ds(pl.multiple_of(c * c_len, c_len), c_len)
        lw, rc, knc, ac, k2c, vc = lw_s[sl, :], r_s[sl, :], kn_s[sl, :], a_s[sl, :], k2_s[sl, :], v_s[sl, :]
        lcum = _dot_exact_lhs(tri_incl, lw)
        gam, gex, gin = jnp.exp(lcum), jnp.exp(lcum - lw), jnp.exp(-lcum)
        rt, at, bt, kt = rc * gam, knc * gex, -(knc * ac) * gin, k2c * gin
        g_end = gam[c_len - 1:c_len, :]
        lhs = jnp.concatenate([at * hm for hm in hmask] + [rt * hm for hm in hmask], axis=0)
        m = jnp.where(keep, _dot3(lhs, jnp.concatenate([bt, kt], axis=0), _NT), 0.0)
        s = sbd[...]
        asrs = _dot3(jnp.concatenate([at, rt], axis=0), s, _NT)
        w1 = _dot3(m[:4 * c_len], jnp.concatenate([jnp.zeros_like(vc), vc], axis=0))
        wm = asrs[:c_len] + pick(w1)
        inv = []
        for h in range(RW_HEADS):
            q = m[h * c_len:(h + 1) * c_len, 0:c_len]
            x = eye + q
            for _ in range(n_sq):
                q = _dot3(q, q)
                x = x + _dot3(x, q)
            inv.append(x)
        p = pick(_dot3(jnp.concatenate(inv, axis=0), wm))
        pv = jnp.concatenate([p, vc], axis=0)
        y_s[sl, :] = asrs[c_len:] + pick(_dot3(m[4 * c_len:], pv))
        upd = _dot3(pv, jnp.concatenate([bt * g_end, kt * g_end], axis=0), _TN)
        sbd[...] = s * g_end + upd * blockdiag
        return 0

    lax.fori_loop(0, tb_rows // c_len, chunk, 0)

    y = y_s[...]
    mean = head_sum(y) * (1.0 / HEAD_DIM)
    yc = y - mean
    var = head_sum(yc * yc) * (1.0 / HEAD_DIM)
    yn = yc * lax.rsqrt(var + RW_GN_EPS) * lnw_ref[...] + lnb_ref[...]
    y_ref[0] = (yn + bonus) * g
    sout_ref[0] = sbd[...]


def _rwkv(u, prev, s0bd, prm, tb_rows, n_valid):
    b, t, _ = u.shape
    vec = lambda n: _full((1, n))
    kern = functools.partial(_rwkv_kernel, tb_rows=tb_rows, n_valid=n_valid)
    return pl.pallas_call(
        kern,
        grid=(b, t // tb_rows),
        in_specs=[pl.BlockSpec((1, tb_rows, 1024), lambda i, j: (i, j, 0)),
                  pl.BlockSpec((1, 1, 1024), lambda i, j: (i, 0, 0)),
                  pl.BlockSpec((1, 256, 256), lambda i, j: (i, 0, 0)),
                  vec(1024), vec(256), _full((128, 256)), vec(256), _full((128, 256)), _full((128, 256)),
                  vec(256), vec(256), vec(256), vec(256), vec(256)],
        out_specs=[pl.BlockSpec((1, tb_rows, 256), lambda i, j: (i, j, 0)),
                   pl.BlockSpec((1, 256, 256), lambda i, j: (i, 0, 0))],
        out_shape=[jax.ShapeDtypeStruct((b, t, 256), F32), jax.ShapeDtypeStruct((b, 256, 256), F32)],
        scratch_shapes=[pltpu.VMEM((256, 256), F32), pltpu.VMEM((1, 1024), F32)]
        + [pltpu.VMEM((tb_rows, 256), F32)] * 7,
        compiler_params=_params("parallel", "arbitrary"),
        name="rwkv7",
    )(u, prev, s0bd, *prm)


def _lru_kernel(u_ref, cb_ref, h0_ref, cw_ref, cbias_ref, wa_ref, ba_ref, wx_ref, bx_ref, lam_ref,
                out_ref, hl_ref, ext, hc, *, tb_rows, reset_first):
    tb = pl.program_id(1)

    @pl.when(tb == 0)
    def _():
        ext[0:8, :] = jnp.zeros((8, 256), F32)
        ext[5:8, :] = cb_ref[0]
        hc[...] = h0_ref[0]

    u = u_ref[0]
    xb, gate = u[:, :256], u[:, 256:]
    ext[8:8 + tb_rows, :] = xb
    cw = cw_ref[...]
    y = (cbias_ref[...] + cw[0:1] * ext[5:5 + tb_rows, :] + cw[1:2] * ext[6:6 + tb_rows, :]
         + cw[2:3] * ext[7:7 + tb_rows, :] + cw[3:4] * xb)
    ext[0:8, :] = ext[tb_rows:tb_rows + 8, :]
    gate_r = jax.nn.sigmoid(_dot3(y, wa_ref[...]) + ba_ref[...])
    gate_i = jax.nn.sigmoid(_dot3(y, wx_ref[...]) + bx_ref[...])
    lam = lam_ref[...]
    softplus_neg = jnp.maximum(-lam, 0.0) + jnp.log(1.0 + jnp.exp(-jnp.abs(lam)))
    log_a = (-LRU_C * softplus_neg) * gate_r
    a = jnp.exp(log_a)
    mult = jnp.sqrt(1.0 - jnp.exp(2.0 * log_a))
    row = lax.broadcasted_iota(I32, (tb_rows, 256), 0)
    if reset_first:
        mult = jnp.where((row == 0) & (tb == 0), 1.0, mult)
    bv = mult * gate_i * y
    d = 1
    while d < tb_rows:
        live = row >= d
        a_sh = jnp.where(live, pltpu.roll(a, d, 0), 1.0)
        b_sh = jnp.where(live, pltpu.roll(bv, d, 0), 0.0)
        bv = a * b_sh + bv
        a = a * a_sh
        d *= 2
    h = a * hc[...] + bv
    hc[...] = h[tb_rows - 1:tb_rows, :]
    gelu = 0.5 * gate * (1.0 + jnp.tanh(math.sqrt(2.0 / math.pi) * (gate + 0.044715 * gate * gate * gate)))
    out_ref[0] = h * gelu
    hl_ref[0] = h[tb_rows - 1:tb_rows, :]


def _lru(u, conv_buf, h0, prm, tb_rows, reset_first):
    b, t, _ = u.shape
    vec = _full((1, 256))
    kern = functools.partial(_lru_kernel, tb_rows=tb_rows, reset_first=reset_first)
    return pl.pallas_call(
        kern,
        grid=(b, t // tb_rows),
        in_specs=[pl.BlockSpec((1, tb_rows, 512), lambda i, j: (i, j, 0)),
                  pl.BlockSpec((1, 3, 256), lambda i, j: (i, 0, 0)),
                  pl.BlockSpec((1, 1, 256), lambda i, j: (i, 0, 0)),
                  _full((4, 256)), vec, _full((256, 256)), vec, _full((256, 256)), vec, vec],
        out_specs=[pl.BlockSpec((1, tb_rows, 256), lambda i, j: (i, j, 0)),
                   pl.BlockSpec((1, 1, 256), lambda i, j: (i, 0, 0))],
        out_shape=[jax.ShapeDtypeStruct((b, t, 256), F32), jax.ShapeDtypeStruct((b, 1, 256), F32)],
        scratch_shapes=[pltpu.VMEM((tb_rows + 8, 256), F32), pltpu.VMEM((1, 256), F32)],
        compiler_params=_params("parallel", "arbitrary"),
        name="rglru",
    )(u, conv_buf, h0, *prm)


def _sortable(x):
    bits = pltpu.bitcast(jnp.where(jnp.abs(x) < F32_MIN_NORMAL, 0.0, x), I32)
    return jnp.where(bits < 0, bits ^ 0x7FFFFFFF, bits)


def _bisect_threshold(count_gt, t0, cnt0, topk, bits=32):
    def cond(c):
        it, _, cnt = c
        return (it < bits) & (jnp.max(cnt) > topk)

    def body(c):
        it, t, cnt = c
        cand = t + lax.shift_left(jnp.int32(1), bits - 1 - it)
        cnt_cand = count_gt(cand)
        ok = cnt_cand >= topk
        return it + 1, jnp.where(ok, cand, t), jnp.where(ok, cnt_cand, cnt)

    return lax.while_loop(cond, body, (jnp.int32(0), t0, cnt0))[1]


def _dsa_prompt_steps(nq, qb_rows, kb_rows):
    steps = []
    for i in range(nq):
        nv = ((i + 1) * qb_rows + kb_rows - 1) // kb_rows
        steps += [(i, 0, j, nv, j, 0) for j in range(nv)] + [(i, 1, j, nv, nv - 1, j) for j in range(nv)]
    return list(zip(*steps))


def _dsa_prompt_kernel(tab_ref, qb_ref, qi_ref, wt_ref, ki4_ref, kb_ref, vt_ref, o_ref,
                       lhs_s, key_s, top_s, thr_s, qm_s, m_s, l_s, acc_s, *, qb_rows, kb_rows, sk, topk, n_steps):
    step = pl.program_id(0)
    i, ph, j, nv = (tab_ref[r * n_steps + step] for r in range(4))
    n_pairs = SA_HEADS // 2

    @pl.when((ph == 0) & (j == 0))
    def _init():
        qi, qb = qi_ref[...], qb_ref[...]
        lane = lax.broadcasted_iota(I32, (qb_rows, 128), 1)
        for h in range(SA_HEADS):
            hi, lo = _split_bf16(qi[:, h * 64:(h + 1) * 64])
            lhs_s[h * qb_rows:(h + 1) * qb_rows, :] = jnp.concatenate([hi, hi, lo, lo], axis=1)
        for p in range(n_pairs):
            qp = qb[:, p * 128:(p + 1) * 128]
            zero = jnp.zeros_like(qp)
            qm_s[p] = jnp.concatenate([jnp.where(lane < 64, qp, zero), jnp.where(lane < 64, zero, qp)], axis=0)
        m_s[...] = jnp.full(m_s.shape, NEG_BIG, F32)
        l_s[...] = jnp.zeros(l_s.shape, F32)
        acc_s[...] = jnp.zeros(acc_s.shape, F32)

    @pl.when(ph == 0)
    def _index():
        wt = wt_ref[...]
        qpos = i * qb_rows + lax.broadcasted_iota(I32, (sk, qb_rows), 1)
        for sub in range(kb_rows // sk):
            z = _dot_nt(ki4_ref[sub * sk:(sub + 1) * sk, :], lhs_s[...])
            sc = wt[0:1] * jnp.maximum(z[:, 0:qb_rows], 0.0)
            for h in range(1, SA_HEADS):
                sc = sc + wt[h:h + 1] * jnp.maximum(z[:, h * qb_rows:(h + 1) * qb_rows], 0.0)
            kpos = j * kb_rows + sub * sk + lax.broadcasted_iota(I32, (sk, qb_rows), 0)
            causal = kpos <= qpos
            sc = jnp.where(jnp.abs(sc) < F32_MIN_NORMAL, 0.0, sc)
            bits = pltpu.bitcast(sc, I32)
            key_s[j, sub * sk:(sub + 1) * sk, :] = jnp.where(causal, jnp.where(bits < 0, bits ^ 0x7FFFFFFF, bits),
                                                             INT_MIN)
            top = pltpu.bitcast(bits & jnp.int32(-65536), F32)
            top_s[j, sub * sk:(sub + 1) * sk, :] = jnp.where(causal, top, -jnp.inf).astype(BF16)

    @pl.when((ph == 1) & (j == 0))
    def _threshold():
        one, zero = jnp.ones((), BF16), jnp.zeros((), BF16)

        def count_top(cand):
            cand = jnp.maximum(cand, -32513)
            cand = jnp.where((cand < 0) & (cand > -129), -129, cand)
            pat = jnp.where(cand >= 0, cand, cand ^ 0x7FFF) & 0xFFFF
            cf = pltpu.bitcast(lax.shift_left(pat, 16), F32).astype(BF16)

            def body(jb, acc):
                x = jnp.where(top_s[jb] > cf, one, zero)
                parts = [x[r * 16:(r + 1) * 16] for r in range(kb_rows // 16)]
                while len(parts) > 1:
                    parts = [parts[a] + parts[a + 1] for a in range(0, len(parts), 2)]
                return acc + parts[0].astype(F32)

            acc = lax.fori_loop(0, nv, body, jnp.zeros((16, qb_rows), F32))
            return jnp.sum(acc, axis=0, keepdims=True).astype(I32)

        def top_step(it, c):
            t, cnt = c
            cand = t + lax.shift_left(jnp.int32(1), 15 - it)
            cnt_cand = count_top(cand)
            ok = cnt_cand >= topk
            return jnp.where(ok, cand, t), jnp.where(ok, cnt_cand, cnt)

        t_top0 = jnp.full((1, qb_rows), -32768, I32)
        t_top, cnt = lax.fori_loop(0, 16, top_step, (t_top0, count_top(t_top0)))

        def count_gt(t):
            def body(jb, acc):
                x = jnp.where(key_s[jb] > t, 1, 0)
                return acc + jnp.sum(x.reshape(kb_rows // 8, 8, qb_rows), axis=0)
            acc = lax.fori_loop(0, nv, body, jnp.zeros((8, qb_rows), I32))
            return jnp.sum(acc, axis=0, keepdims=True)

        thr_s[...] = _bisect_threshold(count_gt, lax.shift_left(t_top, 16) + 0xFFFF, cnt, topk, bits=16)

    @pl.when(ph == 1)
    def _attend():
        t = thr_s[...]
        for rows in (slice(0, kb_rows),):
            bias = jnp.where(key_s[j, rows, :] > t, 0.0, NEG_BIG)
            bias = jnp.concatenate([bias, bias], axis=1)
            scores = [_dot_nt(kb_ref[rows, p * 128:(p + 1) * 128], qm_s[p]) for p in range(n_pairs)]
            for p in range(n_pairs):
                s = scores[p] + bias
                m_old = m_s[p]
                m_new = jnp.maximum(m_old, jnp.max(s, axis=0, keepdims=True))
                alpha = jnp.exp2(m_old - m_new)
                pr = jnp.exp2(s - m_new)
                l_s[p] = alpha * l_s[p] + jnp.sum(pr, axis=0, keepdims=True)
                m_s[p] = m_new
                prb = pr.astype(BF16)
                for hh in range(2):
                    h = 2 * p + hh
                    cols = slice(hh * qb_rows, (hh + 1) * qb_rows)
                    pv = _dot(vt_ref[h * 64:(h + 1) * 64, rows], prb[:, cols])
                    acc_s[p, hh * 64:(hh + 1) * 64, :] = alpha[:, cols] * acc_s[p, hh * 64:(hh + 1) * 64, :] + pv

    @pl.when((ph == 1) & (j == nv - 1))
    def _finish():
        for p in range(n_pairs):
            inv_l = 1.0 / l_s[p]
            a = acc_s[p]
            a = jnp.concatenate([a[:64] * inv_l[:, :qb_rows], a[64:] * inv_l[:, qb_rows:]], axis=0)
            o_ref[:, p * 128:(p + 1) * 128] = a.T


def _dsa_prompt(qb, qi, wt, ki4, kb, vt, qb_rows, kb_rows):
    s = qb.shape[0]
    nq, nk = s // qb_rows, s // kb_rows
    sk = min(256, kb_rows)
    tab = _dsa_prompt_steps(nq, qb_rows, kb_rows)
    n = len(tab[0])
    qrow = lambda w: pl.BlockSpec((qb_rows, w), lambda st, t: (t[st], 0))
    kern = functools.partial(_dsa_prompt_kernel, qb_rows=qb_rows, kb_rows=kb_rows, sk=sk, topk=min(TOPK, s // 4),
                             n_steps=n)
    grid_spec = pltpu.PrefetchScalarGridSpec(
        num_scalar_prefetch=1,
        grid=(n,),
        in_specs=[qrow(512), qrow(512),
                  pl.BlockSpec((SA_HEADS, qb_rows), lambda st, t: (0, t[st])),
                  pl.BlockSpec((kb_rows, 256), lambda st, t: (t[4 * n + st], 0)),
                  pl.BlockSpec((kb_rows, 512), lambda st, t: (t[5 * n + st], 0)),
                  pl.BlockSpec((512, kb_rows), lambda st, t: (0, t[5 * n + st]))],
        out_specs=qrow(512),
        scratch_shapes=[pltpu.VMEM((SA_HEADS * qb_rows, 256), BF16),
                        pltpu.VMEM((nk, kb_rows, qb_rows), I32),
                        pltpu.VMEM((nk, kb_rows, qb_rows), BF16),
                        pltpu.VMEM((1, qb_rows), I32),
                        pltpu.VMEM((SA_HEADS // 2, 2 * qb_rows, 128), BF16),
                        pltpu.VMEM((SA_HEADS // 2, 1, 2 * qb_rows), F32),
                        pltpu.VMEM((SA_HEADS // 2, 1, 2 * qb_rows), F32),
                        pltpu.VMEM((SA_HEADS // 2, 128, qb_rows), F32)],
    )
    return pl.pallas_call(
        kern,
        grid_spec=grid_spec,
        out_shape=jax.ShapeDtypeStruct((s, 512), F32),
        compiler_params=_params("arbitrary"),
        name="dsa_prompt",
    )(jnp.asarray(tab, I32).reshape(-1), qb, qi, wt, ki4, kb, vt)


PAGES_PER_STEP = 16


def _dsa_sample_index_kernel(pt_ref, qi_ref, kw_ref, ki4n_ref, *rest, n_pages, t_len, topk):
    pages = rest[:PAGES_PER_STEP]
    key_ref, thr_ref, lhs_s, wcol_s = rest[PAGES_PER_STEP:]
    j = pl.program_id(1)
    n_steps = n_pages // PAGES_PER_STEP

    @pl.when(j == 0)
    def _init():
        qi, kw = qi_ref[0], kw_ref[0]
        hi, lo = _split_bf16(jnp.concatenate([qi[:, h * 64:(h + 1) * 64] for h in range(SA_HEADS)], axis=0))
        lhs_s[...] = jnp.concatenate([hi, hi, lo, lo], axis=1)
        for h in range(SA_HEADS):
            wcol_s[h] = kw[:, 64 + h:65 + h]

    def head_sum(z):
        sc = wcol_s[0] * jnp.maximum(z[0:t_len], 0.0)
        for h in range(1, SA_HEADS):
            sc = sc + wcol_s[h] * jnp.maximum(z[h * t_len:(h + 1) * t_len], 0.0)
        return sc

    hi, lo = _split_bf16(jnp.concatenate([p[...] for p in pages], axis=1))
    keys = _sortable(head_sum(_dot(lhs_s[...], jnp.concatenate([hi, lo, hi, lo], axis=0))))
    for r in range(PAGES_PER_STEP):
        key_ref[0, j * PAGES_PER_STEP + r] = keys[:, r * PAGE:(r + 1) * PAGE]

    @pl.when(j == n_steps - 1)
    def _finish():
        sc = head_sum(_dot_nt(lhs_s[...], ki4n_ref[0]))
        tq = lax.broadcasted_iota(I32, (t_len, 128), 0)
        ts = lax.broadcasted_iota(I32, (t_len, 128), 1)
        key_ref[0, n_pages] = jnp.where(ts <= tq, _sortable(sc), INT_MIN)

        def count_gt(t):
            x = jnp.where(key_ref[0] > t[None], 1, 0)
            return jnp.sum(jnp.sum(x, axis=0), axis=1, keepdims=True)

        t0 = jnp.full((t_len, 1), INT_MIN, I32)
        thr = _bisect_threshold(count_gt, t0, count_gt(t0), topk)
        thr_ref[0] = jnp.broadcast_to(thr, (t_len, 128))


def _dsa_sample_index(page_table, qi, kw, ki4_new, cache_kidx_t, layer, t_len):
    b = qi.shape[0]
    n_pages = page_table.shape[1]
    n_steps = n_pages // PAGES_PER_STEP

    def page_spec(r):
        return pl.BlockSpec((None, None, 64, PAGE),
                            lambda i, j, pt: (layer, pt[i * n_pages + j * PAGES_PER_STEP + r], 0, 0))

    kern = functools.partial(_dsa_sample_index_kernel, n_pages=n_pages, t_len=t_len,
                             topk=min(TOPK, (n_pages * PAGE + t_len) // 4))
    grid_spec = pltpu.PrefetchScalarGridSpec(
        num_scalar_prefetch=1,
        grid=(b, n_steps),
        in_specs=[pl.BlockSpec((1, t_len, 512), lambda i, j, pt: (i, 0, 0)),
                  pl.BlockSpec((1, t_len, 128), lambda i, j, pt: (i, 0, 0)),
                  pl.BlockSpec((1, PAGE, 256), lambda i, j, pt: (i, 0, 0))]
        + [page_spec(r) for r in range(PAGES_PER_STEP)],
        out_specs=[pl.BlockSpec((1, n_pages + 1, t_len, 128), lambda i, j, pt: (i, 0, 0, 0)),
                   pl.BlockSpec((1, t_len, 128), lambda i, j, pt: (i, 0, 0))],
        scratch_shapes=[pltpu.VMEM((SA_HEADS * t_len, 256), BF16), pltpu.VMEM((SA_HEADS, t_len, 1), F32)],
    )
    return pl.pallas_call(
        kern,
        grid_spec=grid_spec,
        out_shape=[jax.ShapeDtypeStruct((b, n_pages + 1, t_len, 128), I32),
                   jax.ShapeDtypeStruct((b, t_len, 128), I32)],
        compiler_params=_params("parallel", "arbitrary"),
        name="dsa_sample_index",
    )(page_table.reshape(-1), qi, kw, ki4_new, *([cache_kidx_t] * PAGES_PER_STEP))


def _dsa_sample_attend_kernel(pt_ref, qb_ref, key_ref, keyn_ref, thr_ref, kn_ref, vn_ref, *rest, n_pages, t_len):
    kpages = rest[:PAGES_PER_STEP]
    vpages = rest[PAGES_PER_STEP:2 * PAGES_PER_STEP]
    o_ref, qbd_s, m_s, l_s, acc_s = rest[2 * PAGES_PER_STEP:]
    j = pl.program_id(1)
    n_steps = n_pages // PAGES_PER_STEP
    rows = SA_HEADS * t_len

    @pl.when(j == 0)
    def _init():
        qb = qb_ref[0].astype(F32)
        lane = lax.broadcasted_iota(I32, (t_len, 512), 1) // 64
        qbd = jnp.concatenate([jnp.where(lane == h, qb, 0.0) for h in range(SA_HEADS)], axis=0)
        qbd_s[...] = qbd.astype(BF16)
        m_s[...] = jnp.full(m_s.shape, NEG_BIG, F32)
        l_s[...] = jnp.zeros(l_s.shape, F32)
        acc_s[...] = jnp.zeros(acc_s.shape, F32)

    thr = thr_ref[0][:, 0:1]

    def update(keys, s, pv):
        bias = jnp.where(keys > thr, 0.0, NEG_BIG)
        s = s + jnp.concatenate([bias] * SA_HEADS, axis=0)
        m_old = m_s[...]
        m_new = jnp.maximum(m_old, jnp.max(s, axis=1, keepdims=True))
        alpha = jnp.exp2(m_old - m_new)
        pr = jnp.exp2(s - m_new)
        l_s[...] = alpha * l_s[...] + jnp.sum(pr, axis=1, keepdims=True)
        acc_s[...] = alpha * acc_s[...] + pv(pr.astype(BF16))
        m_s[...] = m_new

    kcat = jnp.concatenate([p[...].astype(BF16) for p in kpages], axis=1)
    vcat = jnp.concatenate([p[...].astype(BF16) for p in vpages], axis=1)
    keys = jnp.concatenate([key_ref[0, r] for r in range(PAGES_PER_STEP)], axis=1)
    update(keys, _dot(qbd_s[...], kcat), lambda pr: _dot_nt(pr, vcat))

    @pl.when(j == n_steps - 1)
    def _finish():
        kn, vn = kn_ref[0].astype(BF16), vn_ref[0].astype(BF16)
        update(keyn_ref[0, 0], _dot_nt(qbd_s[...], kn), lambda pr: _dot(pr, vn))
        o = acc_s[...] / l_s[...]
        lane = lax.broadcasted_iota(I32, (t_len, 512), 1) // 64
        out = jnp.zeros((t_len, 512), F32)
        for h in range(SA_HEADS):
            out = out + jnp.where(lane == h, o[h * t_len:(h + 1) * t_len], 0.0)
        o_ref[0] = out


def _dsa_sample_attend(page_table, qb, keys, thr, k_new, v_new, cache_k_t, cache_v_t, layer, t_len):
    b = qb.shape[0]
    n_pages = page_table.shape[1]
    n_steps = n_pages // PAGES_PER_STEP
    rows = SA_HEADS * t_len

    def page_spec(r):
        return pl.BlockSpec((None, None, 512, PAGE),
                            lambda i, j, pt: (layer, pt[i * n_pages + j * PAGES_PER_STEP + r], 0, 0))

    per_b = lambda n: pl.BlockSpec((1, t_len, n), lambda i, j, pt: (i, 0, 0))
    kern = functools.partial(_dsa_sample_attend_kernel, n_pages=n_pages, t_len=t_len)
    grid_spec = pltpu.PrefetchScalarGridSpec(
        num_scalar_prefetch=1,
        grid=(b, n_steps),
        in_specs=[per_b(512),
                  pl.BlockSpec((1, PAGES_PER_STEP, t_len, 128), lambda i, j, pt: (i, j, 0, 0)),
                  pl.BlockSpec((1, 1, t_len, 128), lambda i, j, pt: (i, n_pages, 0, 0)),
                  per_b(128),
                  pl.BlockSpec((1, PAGE, 512), lambda i, j, pt: (i, 0, 0)),
                  pl.BlockSpec((1, PAGE, 512), lambda i, j, pt: (i, 0, 0))]
        + [page_spec(r) for r in range(PAGES_PER_STEP)] * 2,
        out_specs=per_b(512),
        scratch_shapes=[pltpu.VMEM((rows, 512), BF16), pltpu.VMEM((rows, 1), F32), pltpu.VMEM((rows, 1), F32),
                        pltpu.VMEM((rows, 512), F32)],
    )
    return pl.pallas_call(
        kern,
        grid_spec=grid_spec,
        out_shape=jax.ShapeDtypeStruct((b, t_len, 512), F32),
        compiler_params=_params("parallel", "arbitrary"),
        name="dsa_sample_attend",
    )(page_table.reshape(-1), qb, keys, keys, thr, k_new, v_new,
      *([cache_k_t] * PAGES_PER_STEP), *([cache_v_t] * PAGES_PER_STEP))


def _matmul_kernel(x_ref, w_ref, o_ref):
    o_ref[...] = _dot(x_ref[...].astype(BF16), w_ref[...])


def _matmul(x, w):
    m, n = x.shape[0], w.shape[1]
    return pl.pallas_call(
        _matmul_kernel,
        grid=(1,),
        in_specs=[_full(x.shape), _full(w.shape)],
        out_specs=_full((m, n)),
        out_shape=jax.ShapeDtypeStruct((m, n), F32),
        compiler_params=_params("arbitrary"),
        name="mem_kv_proj",
    )(x, w)


def _post1_kernel(x_ref, yrw_ref, ysa_ref, ylru_ref, wo1_ref, wo2_ref, wo3_ref, g_ref, wq_ref, x1_ref, qm_ref):
    x1 = (x_ref[...] + _dot(yrw_ref[...].astype(BF16), wo1_ref[...])
          + _dot(ysa_ref[...].astype(BF16), wo2_ref[...]) + _dot(ylru_ref[...].astype(BF16), wo3_ref[...]))
    x1_ref[...] = x1
    qm = _dot(_rms(x1, g_ref[...]).astype(BF16), wq_ref[...])
    qm_ref[...] = (qm * (MEM_HEAD_DIM ** -0.5)).astype(BF16)


def _post1(x, yrw, ysa, ylru, wo1, wo2, wo3, g, wq, tm):
    m = x.shape[0]
    row = lambda n: pl.BlockSpec((tm, n), lambda i: (i, 0))
    return pl.pallas_call(
        _post1_kernel,
        grid=(m // tm,),
        in_specs=[row(1024), row(256), row(512), row(256), _full(wo1.shape), _full(wo2.shape), _full(wo3.shape),
                  _full((1, 1024)), _full(wq.shape)],
        out_specs=[row(1024), row(1024)],
        out_shape=[jax.ShapeDtypeStruct((m, 1024), F32), jax.ShapeDtypeStruct((m, 1024), BF16)],
        compiler_params=_params("parallel"),
        name="out_proj_memq",
    )(x, yrw, ysa, ylru, wo1, wo2, wo3, g, wq)


def _mem_attn_kernel(q_ref, mk_ref, mv_ref, o_ref):
    q = q_ref[0]
    mk = mk_ref[0].astype(BF16)
    mv = mv_ref[0].astype(BF16)
    for h in range(MEM_HEADS):
        sl = slice(h * MEM_HEAD_DIM, (h + 1) * MEM_HEAD_DIM)
        s = _dot_nt(q[:, sl], mk[:, sl])
        p = jnp.exp(s - jnp.max(s, axis=1, keepdims=True))
        o = _dot(p.astype(BF16), mv[:, sl]) / jnp.sum(p, axis=1, keepdims=True)
        o_ref[0, :, sl] = o.astype(BF16)


def _mem_attn(q, mk, mv, tm):
    b, t, _ = q.shape
    return pl.pallas_call(
        _mem_attn_kernel,
        grid=(b, t // tm),
        in_specs=[pl.BlockSpec((1, tm, 1024), lambda i, j: (i, j, 0)),
                  pl.BlockSpec((1, 256, 1024), lambda i, j: (i, 0, 0)),
                  pl.BlockSpec((1, 256, 1024), lambda i, j: (i, 0, 0))],
        out_specs=pl.BlockSpec((1, tm, 1024), lambda i, j: (i, j, 0)),
        out_shape=jax.ShapeDtypeStruct((b, t, 1024), BF16),
        compiler_params=_params("parallel", "parallel"),
        name="mem_attn",
    )(q, mk, mv)


def _post2_kernel(x1_ref, o_ref, wo_ref, g_ref, wg_ref, wu_ref, wd_ref, gf_ref, out_ref, *, last):
    x2 = x1_ref[...] + _dot(o_ref[...], wo_ref[...])
    hb = _rms(x2, g_ref[...]).astype(BF16)
    gt = _dot(hb, wg_ref[...])
    up = _dot(hb, wu_ref[...])
    act = (gt * jax.nn.sigmoid(gt) * up).astype(BF16)
    x3 = x2 + _dot(act, wd_ref[...])
    out_ref[...] = _rms(x3, gf_ref[...]) if last else x3


def _post2(x1, o, wo, g, wg, wu, wd, gf, tm, last):
    m = x1.shape[0]
    row = lambda n: pl.BlockSpec((tm, n), lambda i: (i, 0))
    once = lambda w: pl.BlockSpec(w.shape, lambda i: (0, 0), pipeline_mode=pl.Buffered(1))
    return pl.pallas_call(
        functools.partial(_post2_kernel, last=last),
        grid=(m // tm,),
        in_specs=[row(1024), row(1024), once(wo), _full((1, 1024)), once(wg), once(wu), once(wd), _full((1, 1024))],
        out_specs=row(1024),
        out_shape=jax.ShapeDtypeStruct((m, 1024), F32),
        compiler_params=_params("parallel"),
        name="memo_swiglu",
    )(x1, o, wo, g, wg, wu, wd, gf)


def _blockdiag4(w):
    n = w.shape[-1]
    return jnp.einsum("gij,gh->gihj", w, jnp.eye(4, dtype=w.dtype)).reshape(4 * n, 4 * n)


def _state_to_blockdiag(s):
    b = s.shape[0]
    return jnp.einsum("bhvk,hg->bhvgk", s, jnp.eye(4, dtype=s.dtype)).reshape(b, 256, 256)


def _blockdiag_to_state(sbd):
    b = sbd.shape[0]
    return jnp.einsum("bhvgk,hg->bhvk", sbd.reshape(b, 4, 64, 4, 64), jnp.eye(4, dtype=sbd.dtype))


def _layer_weights(l, w_in, w_out, rw, lru, g_mix, g_mem, w_mq, w_mk, w_mv, w_mo, g_ffn, w_gate, w_up, w_down):
    wi = w_in[l]
    o = RW_COLS
    cut = lambda a, n: wi[:, a:a + n].astype(BF16)
    wkw = jnp.pad(wi[:, o + 2048:o + 2120], ((0, 0), (0, 56))).astype(BF16)
    in_ws = (cut(0, 1024), cut(o, 512), cut(o + 512, 512), cut(o + 1024, 512), cut(o + 1536, 512), wkw,
             cut(o + 2120, 512))
    (rw_mu, rw_w0, rw_w2, rw_a0, rw_a2, rw_g2, rw_kk, rw_ka, rw_rk, rw_ln_w, rw_ln_b) = rw
    z = jnp.zeros((64, 256), F32)
    v256 = lambda a: a[l].reshape(1, 256)
    rw_prm = (rw_mu[l].reshape(1, 1024), v256(rw_w0), jnp.concatenate([rw_w2[l], z], axis=0), v256(rw_a0),
              jnp.concatenate([z, rw_a2[l]], axis=0), rw_g2[l], v256(rw_kk), v256(rw_ka), v256(rw_rk),
              v256(rw_ln_w), v256(rw_ln_b))
    (lru_conv_w, lru_conv_b, lru_wa, lru_ba, lru_wx, lru_bx, lru_lambda) = lru
    lru_prm = (lru_conv_w[l], v256(lru_conv_b), _blockdiag4(lru_wa[l]), v256(lru_ba), _blockdiag4(lru_wx[l]),
               v256(lru_bx), v256(lru_lambda))
    wo = w_out[l].astype(BF16)
    return dict(
        in_ws=in_ws, rw=rw_prm, lru=lru_prm, g_mix=g_mix[l].reshape(1, 1024),
        wo=(wo[:256], wo[256:768], wo[768:]), g_mem=g_mem[l].reshape(1, 1024), wq=w_mq[l].astype(BF16),
        wmk=w_mk[l].astype(BF16), wmv=w_mv[l].astype(BF16), wmo=w_mo[l].astype(BF16),
        g_ffn=g_ffn[l].reshape(1, 1024), wg=w_gate[l].astype(BF16), wu=w_up[l].astype(BF16),
        wd=w_down[l].astype(BF16))


def _pick_tile(n, pref):
    t = min(n, pref)
    while n % t:
        t //= 2
    return t


def _prompt_layer(x, mem, w, tabs, g_final, last):
    s = x.shape[0]
    tm = _pick_tile(s, 256)
    urw, kt, vt, qi, _, kwt, ulru, qb, kb, vtb, ki4 = _in_proj(x, w["g_mix"], w["in_ws"], tabs, tm)
    tb = _pick_tile(s, 512)
    y_rw, sbd = _rwkv(urw[None], jnp.zeros((1, 1, 1024), F32), jnp.zeros((1, 256, 256), F32), w["rw"], tb, tb)
    y_lru, h_last = _lru(ulru[None], jnp.zeros((1, 3, 256), F32), jnp.zeros((1, 1, 256), F32), w["lru"], tb, True)
    y_sa = _dsa_prompt(qb, qi, kwt[64:64 + SA_HEADS], ki4, kb, vtb, _pick_tile(s, 256), _pick_tile(s, 1024))
    mk = _matmul(mem, w["wmk"])
    mv = _matmul(mem, w["wmv"])
    x1, qm = _post1(x, y_rw[0], y_sa, y_lru[0], *w["wo"], w["g_mem"], w["wq"], tm)
    o = _mem_attn(qm[None], mk[None], mv[None], tm)
    x3 = _post2(x1, o[0], w["wmo"], w["g_ffn"], w["wg"], w["wu"], w["wd"], g_final, tm, last)
    rows_major = lambda a: jnp.transpose(a.reshape(SA_HEADS, HEAD_DIM, s), (2, 0, 1))[None]
    new = dict(k=rows_major(kt), v=rows_major(vt), kidx=kwt[:64].T[None],
               mk=mk.reshape(1, 256, 4, 256), mv=mv.reshape(1, 256, 4, 256), rwkv=_blockdiag_to_state(sbd),
               shift=urw[None, s - 1], h=h_last[:, 0], conv=ulru[None, s - 3:, :256])
    return x3, new


def _sample_layer(x, l, w, tabs, g_final, cache_k, cache_v, cache_kidx, cache_mem_k, cache_mem_v, state_rwkv,
                  state_rwkv_shift, state_lru_h, state_lru_conv, page_table, db, t, last):
    m = db * t
    urw, kt, vt, qi, kw, _, ulru, qb, _, _, ki4 = _in_proj(x, w["g_mix"], w["in_ws"], tabs, m)
    k, v = kt.T, vt.T
    urw3 = urw.reshape(db, t, 1024)
    u_pad = jnp.pad(urw3, ((0, 0), (0, RW_CHUNK - t), (0, 0)))
    y_rw, sbd = _rwkv(u_pad, state_rwkv_shift[l][:, None], _state_to_blockdiag(state_rwkv[l]), w["rw"], RW_CHUNK, t)
    ulru3 = ulru.reshape(db, t, 512)
    y_lru, h_last = _lru(ulru3, state_lru_conv[l], state_lru_h[l][:, None], w["lru"], t, False)
    pad_new = lambda a: jnp.pad(a.reshape(db, t, -1), ((0, 0), (0, PAGE - t), (0, 0)))
    keys, thr = _dsa_sample_index(page_table, qi.reshape(db, t, 512), kw.reshape(db, t, 128), pad_new(ki4),
                                  cache_kidx, l, t)
    y_sa = _dsa_sample_attend(page_table, qb.reshape(db, t, 512), keys, thr, pad_new(k), pad_new(v),
                              cache_k, cache_v, l, t)
    x1, qm = _post1(x, y_rw[:, :t].reshape(m, 256), y_sa.reshape(m, 512), y_lru.reshape(m, 256), *w["wo"],
                    w["g_mem"], w["wq"], m)
    o = _mem_attn(qm.reshape(db, t, 1024), cache_mem_k[l].reshape(db, 256, 1024),
                  cache_mem_v[l].reshape(db, 256, 1024), t)
    x3 = _post2(x1, o.reshape(m, 1024), w["wmo"], w["g_ffn"], w["wg"], w["wu"], w["wd"], g_final, m, last)
    conv = jnp.concatenate([state_lru_conv[l], ulru3[:, :, :256]], axis=1)[:, t:]
    new = dict(k=k.reshape(db, t, 8, 64), v=v.reshape(db, t, 8, 64), kidx=kw.reshape(db, t, 128)[:, :, :64],
               rwkv=_blockdiag_to_state(sbd), shift=urw3[:, t - 1], h=h_last[:, 0], conv=conv)
    return x3, new


def kernel(x_prompt, x_sample, mem_prompt, cache_k, cache_v, cache_kidx, cache_mem_k, cache_mem_v, state_rwkv, state_rwkv_shift, state_lru_h, state_lru_conv, page_table, g_mix, w_in, w_out, rw_mu, rw_w0, rw_w2, rw_a0, rw_a2, rw_g2, rw_kk, rw_ka, rw_rk, rw_ln_w, rw_ln_b, lru_conv_w, lru_conv_b, lru_wa, lru_ba, lru_wx, lru_bx, lru_lambda, g_mem, w_mq, w_mk, w_mv, w_mo, g_ffn, w_gate, w_up, w_down, g_final):
    depth = w_in.shape[0]
    _, s, _ = x_prompt.shape
    db, t, _ = x_sample.shape
    past_len = page_table.shape[1] * PAGE
    rw = (rw_mu, rw_w0, rw_w2, rw_a0, rw_a2, rw_g2, rw_kk, rw_ka, rw_rk, rw_ln_w, rw_ln_b)
    lru = (lru_conv_w, lru_conv_b, lru_wa, lru_ba, lru_wx, lru_bx, lru_lambda)
    tabs_p = _rope_tables(jnp.arange(s))
    tabs_s = _rope_tables(jnp.tile(past_len + jnp.arange(t), db))
    gf = g_final.reshape(1, 1024)
    n_pool = cache_k.shape[1]
    cache_kidx = jnp.transpose(cache_kidx, (0, 1, 3, 2))
    cache_k = jnp.transpose(cache_k, (0, 1, 3, 4, 2)).reshape(depth, n_pool, SA_WIDTH, PAGE)
    cache_v = jnp.transpose(cache_v, (0, 1, 3, 4, 2)).reshape(depth, n_pool, SA_WIDTH, PAGE)
    xp, xs = x_prompt[0], x_sample.reshape(db * t, 1024)
    mem = mem_prompt[0]
    news_p, news_s = [], []
    for l in range(depth):
        w = _layer_weights(l, w_in, w_out, rw, lru, g_mix, g_mem, w_mq, w_mk, w_mv, w_mo, g_ffn, w_gate, w_up,
                           w_down)
        last = l == depth - 1
        xp, new_p = _prompt_layer(xp, mem, w, tabs_p, gf, last)
        xs, new_s = _sample_layer(xs, l, w, tabs_s, gf, cache_k, cache_v, cache_kidx, cache_mem_k, cache_mem_v,
                                  state_rwkv, state_rwkv_shift, state_lru_h, state_lru_conv, page_table, db, t, last)
        news_p.append(new_p)
        news_s.append(new_s)
    stk = lambda news, name: jnp.stack([n[name] for n in news])
    return (xp[None], xs.reshape(db, t, 1024),
            stk(news_p, "k"), stk(news_p, "v"), stk(news_p, "kidx"), stk(news_p, "mk"), stk(news_p, "mv"),
            stk(news_p, "rwkv"), stk(news_p, "shift"), stk(news_p, "h"), stk(news_p, "conv"),
            stk(news_s, "k"), stk(news_s, "v"), stk(news_s, "kidx"), stk(news_s, "rwkv"), stk(news_s, "shift"),
            stk(news_s, "h"), stk(news_s, "conv"))
```

```python
import functools
import math

import jax
import jax.numpy as jnp
from jax import lax
from jax.experimental import pallas as pl
from jax.experimental.pallas import tpu as pltpu

F32 = jnp.float32
BF16 = jnp.bfloat16
I32 = jnp.int32

D_MODEL = 1024
HEAD_DIM = 64
RW_HEADS = 4
RW_WIDTH = 256
RW_COLS = 1024
RW_GN_EPS = 64e-5
SA_HEADS = 8
SA_WIDTH = 512
IDX_SCALE = 512.0 ** -0.5
TOPK = 256
LRU_WIDTH = 256
LRU_C = 8.0
D_FF = 2816
MEM_HEADS = 4
MEM_HEAD_DIM = 256
ROPE_THETA = 10000.0
NORM_EPS = 1e-6
PAGE = 128
LOG2E = 1.4426950408889634
INT_MIN = -2 ** 31
F32_MIN_NORMAL = 2.0 ** -126
NO_TIE = 2 ** 30
NEG_BIG = -1e30
VMEM_LIMIT = 56 * 1024 * 1024
RW_CHUNK = 64


def _params(*sem):
    return pltpu.CompilerParams(dimension_semantics=sem, vmem_limit_bytes=VMEM_LIMIT)


def _full(shape):
    n = len(shape)
    return pl.BlockSpec(shape, lambda *_: (0,) * n)


def _dot(a, b):
    return jnp.dot(a, b, preferred_element_type=F32)


def _dot_nt(a, b):
    return lax.dot_general(a, b, (((1,), (1,)), ((), ())), preferred_element_type=F32)


def _rms(x, g):
    return x * lax.rsqrt(jnp.mean(x * x, axis=-1, keepdims=True) + NORM_EPS) * g


def _split_bf16(x):
    hi = x.astype(BF16)
    lo = (x - hi.astype(F32)).astype(BF16)
    return hi, lo


_NN = (((1,), (0,)), ((), ()))
_NT = (((1,), (1,)), ((), ()))
_TN = (((0,), (0,)), ((), ()))


def _dot1(a, b, dims=_NN):
    return lax.dot_general(a.astype(BF16), b.astype(BF16), dims, preferred_element_type=F32)


def _dot3(a, b, dims=_NN):
    ah, al = _split_bf16(a)
    bh, bl = _split_bf16(b)
    d = lambda x, y: lax.dot_general(x, y, dims, preferred_element_type=F32)
    return d(ah, bh) + (d(ah, bl) + d(al, bh))


def _dot_exact_rhs(a, m):
    hi = a.astype(BF16)
    r = a - hi.astype(F32)
    mid = r.astype(BF16)
    lo = (r - mid.astype(F32)).astype(BF16)
    return _dot(hi, m) + (_dot(mid, m) + _dot(lo, m))


def _dot_exact_lhs(m, b):
    hi = b.astype(BF16)
    r = b - hi.astype(F32)
    mid = r.astype(BF16)
    lo = (r - mid.astype(F32)).astype(BF16)
    return _dot(m, hi) + (_dot(m, mid) + _dot(m, lo))


def _rope(u, cos, sin):
    w = u.shape[-1]
    lane = lax.broadcasted_iota(I32, u.shape, 1)
    rot = jnp.where((lane & 63) < 32, pltpu.roll(u, w - 32, 1), pltpu.roll(u, 32, 1))
    return u * cos + rot * sin


def _in_proj_kernel(x_ref, g_ref, wrw_ref, wq_ref, wk_ref, wv_ref, wqi_ref, wkw_ref, wlru_ref,
                    cos_ref, sin_ref, ckw_ref, skw_ref,
                    urw_ref, kt_ref, vt_ref, qi_ref, kw_ref, kwt_ref, ulru_ref, qb_ref, kb_ref, vtb_ref, ki4_ref):
    hb = _rms(x_ref[...], g_ref[...]).astype(BF16)
    urw_ref[...] = _dot(hb, wrw_ref[...])
    ulru_ref[...] = _dot(hb, wlru_ref[...])
    cos = jnp.concatenate([cos_ref[...]] * 4, axis=1)
    sin = jnp.concatenate([sin_ref[...]] * 4, axis=1)
    q = _rope(_dot(hb, wq_ref[...]), cos, sin)
    k = _rope(_dot(hb, wk_ref[...]), cos, sin)
    v = _dot(hb, wv_ref[...])
    qi_ref[...] = _rope(_dot(hb, wqi_ref[...]), cos, sin)
    kw = _rope(_dot(hb, wkw_ref[...]), ckw_ref[...], skw_ref[...])
    vt = v.T
    kt_ref[...] = k.T
    vt_ref[...] = vt
    kw_ref[...] = kw
    kwt_ref[...] = kw.T
    qb_ref[...] = (q * (HEAD_DIM ** -0.5 * LOG2E)).astype(BF16)
    kb_ref[...] = k.astype(BF16)
    vtb_ref[...] = vt.astype(BF16)
    hi, lo = _split_bf16(kw[:, :64])
    ki4_ref[...] = jnp.concatenate([hi, lo, hi, lo], axis=1)


def _in_proj(x, g, ws, tabs, tm):
    m = x.shape[0]
    wrw, wq, wk, wv, wqi, wkw, wlru = ws
    row = lambda n: pl.BlockSpec((tm, n), lambda i: (i, 0))
    col = lambda n: pl.BlockSpec((n, tm), lambda i: (0, i))
    sds = jax.ShapeDtypeStruct
    out_shape = [sds((m, 1024), F32), sds((512, m), F32), sds((512, m), F32), sds((m, 512), F32),
                 sds((m, 128), F32), sds((128, m), F32), sds((m, 512), F32), sds((m, 512), BF16),
                 sds((m, 512), BF16), sds((512, m), BF16), sds((m, 256), BF16)]
    return pl.pallas_call(
        _in_proj_kernel,
        grid=(m // tm,),
        in_specs=[row(1024), _full((1, 1024))] + [_full(w.shape) for w in ws] + [row(128)] * 4,
        out_specs=[row(1024), col(512), col(512), row(512), row(128), col(128), row(512), row(512), row(512),
                   col(512), row(256)],
        out_shape=out_shape,
        compiler_params=_params("parallel"),
        name="in_proj",
    )(x, g, wrw, wq, wk, wv, wqi, wkw, wlru, *tabs)


def _rope_tables(pos):
    half = HEAD_DIM // 2
    inv = ROPE_THETA ** (-jnp.arange(half, dtype=F32) / half)
    ang = pos.astype(F32)[:, None] * inv[None, :]
    c, s = jnp.cos(ang), jnp.sin(ang)
    m = pos.shape[0]
    cos = jnp.concatenate([c, c, c, c], axis=1)
    sin = jnp.concatenate([-s, s, -s, s], axis=1)
    ckw = jnp.concatenate([c, c, jnp.full((m, SA_HEADS), IDX_SCALE, F32), jnp.zeros((m, 56), F32)], axis=1)
    skw = jnp.concatenate([-s, s, jnp.zeros((m, 64), F32)], axis=1)
    return cos, sin, ckw, skw


def _rwkv_kernel(u_ref, prev_ref, s0_ref, mu_ref, w0_ref, w2_ref, a0_ref, a2_ref, g2_ref, kk_ref, ka_ref,
                 rk_ref, lnw_ref, lnb_ref, y_ref, sout_ref,
                 sbd, carry, r_s, kn_s, a_s, k2_s, v_s, lw_s, y_s, *, tb_rows, n_valid):
    c_len = RW_CHUNK
    tb = pl.program_id(1)

    @pl.when(tb == 0)
    def _():
        sbd[...] = s0_ref[0]
        carry[...] = prev_ref[0]

    u = u_ref[0]
    row = lax.broadcasted_iota(I32, u.shape, 0)
    u_prev = jnp.where(row == 0, carry[...], pltpu.roll(u, 1, 0))
    last = min(n_valid, tb_rows) - 1
    carry[...] = u[last:last + 1, :]
    us = u + mu_ref[...] * (u_prev - u)
    r, k, v = us[:, 0:256], us[:, 256:512], us[:, 512:768]
    wa, gd = us[:, 768:896], us[:, 896:1024]

    li = lax.broadcasted_iota(I32, (256, 256), 0) // 64
    lj = lax.broadcasted_iota(I32, (256, 256), 1) // 64
    blockdiag = (li == lj).astype(F32)
    head_sum = functools.partial(_dot_exact_rhs, m=blockdiag.astype(BF16))

    xw = w0_ref[...] + _dot3(jnp.tanh(wa), w2_ref[...])
    logw = -math.exp(-0.5) * jax.nn.sigmoid(xw)
    a = jax.nn.sigmoid(a0_ref[...] + _dot3(wa, a2_ref[...]))
    g = _dot3(jax.nn.sigmoid(gd), g2_ref[...])
    kkv = k * kk_ref[...]
    kn = kkv / jnp.maximum(jnp.sqrt(head_sum(kkv * kkv)), 1e-12)
    k2 = k * (1.0 + (a - 1.0) * ka_ref[...])
    bonus = head_sum(r * k2 * rk_ref[...]) * v
    if n_valid < tb_rows:
        ok = lax.broadcasted_iota(I32, (tb_rows, 256), 0) < n_valid
        logw = jnp.where(ok, logw, 0.0)
        kn = jnp.where(ok, kn, 0.0)
        k2 = jnp.where(ok, k2, 0.0)
        v = jnp.where(ok, v, 0.0)
    r_s[...] = r
    kn_s[...] = kn
    a_s[...] = a
    k2_s[...] = k2
    v_s[...] = v
    lw_s[...] = logw

    lane = lax.broadcasted_iota(I32, (1, 256), 1) // 64
    hmask = [(lane == h).astype(F32) for h in range(RW_HEADS)]
    ci = lax.broadcasted_iota(I32, (c_len, c_len), 0)
    cj = lax.broadcasted_iota(I32, (c_len, c_len), 1)
    tri_incl = (ci >= cj).astype(BF16)
    eye = (ci == cj).astype(F32)
    mi = lax.broadcasted_iota(I32, (8 * c_len, 2 * c_len), 0)
    mj = lax.broadcasted_iota(I32, (8 * c_len, 2 * c_len), 1)
    mt, ms = mi & (c_len - 1), mj & (c_len - 1)
    keep = ms < mt + jnp.where(mi < 4 * c_len, 0, 1)
    pick = lambda x: sum(x[h * c_len:(h + 1) * c_len] * hmask[h] for h in range(RW_HEADS))
    n_sq = int(math.log2(c_len)) - 1

    def prepare(c):
        sl = pl.ds(pl.multiple_of(c * c_len, c_len), c_len)
        lw, rc, knc, ac, k2c, vc = lw_s[sl, :], r_s[sl, :], kn_s[sl, :], a_s[sl, :], k2_s[sl, :], v_s[sl, :]
        lcum = _dot_exact_lhs(tri_incl, lw)
        gam, gex, gin = jnp.exp(lcum), jnp.exp(lcum - lw), jnp.exp(-lcum)
        rt, at, bt, kt = rc * gam, knc * gex, -(knc * ac) * gin, k2c * gin
        g_end = gam[c_len - 1:c_len, :]
        lhs = jnp.concatenate([at * hm for hm in hmask] + [rt * hm for hm in hmask], axis=0)
        m = jnp.where(keep, _dot1(lhs, jnp.concatenate([bt, kt], axis=0), _NT), 0.0)
        w1 = pick(_dot1(m[:4 * c_len], jnp.concatenate([jnp.zeros_like(vc), vc], axis=0)))
        return dict(sl=sl, vc=vc, m=m, w1=w1, ar=jnp.concatenate([at, rt], axis=0), g_end=g_end,
                    bk=jnp.concatenate([bt * g_end, kt * g_end], axis=0))

    def invert(ms):
        qs = [m[h * c_len:(h + 1) * c_len, 0:c_len] for m in ms for h in range(RW_HEADS)]
        xs = [eye + q for q in qs]
        for _ in range(n_sq):
            qs = [_dot1(q, q) for q in qs]
            xs = [x + _dot1(x, q) for x, q in zip(xs, qs)]
        return [jnp.concatenate(xs[n * RW_HEADS:(n + 1) * RW_HEADS], axis=0) for n in range(len(ms))]

    def advance(d, inv):
        s = sbd[...]
        asrs = _dot1(d["ar"], s, _NT)
        p = pick(_dot1(inv, asrs[:c_len] + d["w1"]))
        pv = jnp.concatenate([p, d["vc"]], axis=0)
        y_s[d["sl"], :] = asrs[c_len:] + pick(_dot1(d["m"][4 * c_len:], pv))
        sbd[...] = s * d["g_end"] + _dot1(pv, d["bk"], _TN) * blockdiag

    n_chunks = tb_rows // c_len
    group = 4 if n_chunks % 4 == 0 else 1

    def chunk_group(gi, _):
        ds = [prepare(gi * group + n) for n in range(group)]
        for d, inv in zip(ds, invert([d["m"] for d in ds])):
            advance(d, inv)
        return 0

    lax.fori_loop(0, n_chunks // group, chunk_group, 0)

    y = y_s[...]
    mean = head_sum(y) * (1.0 / HEAD_DIM)
    yc = y - mean
    var = head_sum(yc * yc) * (1.0 / HEAD_DIM)
    yn = yc * lax.rsqrt(var + RW_GN_EPS) * lnw_ref[...] + lnb_ref[...]
    y_ref[0] = (yn + bonus) * g
    sout_ref[0] = sbd[...]


def _rwkv(u, prev, s0bd, prm, tb_rows, n_valid):
    b, t, _ = u.shape
    vec = lambda n: _full((1, n))
    kern = functools.partial(_rwkv_kernel, tb_rows=tb_rows, n_valid=n_valid)
    return pl.pallas_call(
        kern,
        grid=(b, t // tb_rows),
        in_specs=[pl.BlockSpec((1, tb_rows, 1024), lambda i, j: (i, j, 0)),
                  pl.BlockSpec((1, 1, 1024), lambda i, j: (i, 0, 0)),
                  pl.BlockSpec((1, 256, 256), lambda i, j: (i, 0, 0)),
                  vec(1024), vec(256), _full((128, 256)), vec(256), _full((128, 256)), _full((128, 256)),
                  vec(256), vec(256), vec(256), vec(256), vec(256)],
        out_specs=[pl.BlockSpec((1, tb_rows, 256), lambda i, j: (i, j, 0)),
                   pl.BlockSpec((1, 256, 256), lambda i, j: (i, 0, 0))],
        out_shape=[jax.ShapeDtypeStruct((b, t, 256), F32), jax.ShapeDtypeStruct((b, 256, 256), F32)],
        scratch_shapes=[pltpu.VMEM((256, 256), F32), pltpu.VMEM((1, 1024), F32)]
        + [pltpu.VMEM((tb_rows, 256), F32)] * 7,
        compiler_params=_params("parallel", "arbitrary"),
        name="rwkv7",
    )(u, prev, s0bd, *prm)


def _lru_kernel(u_ref, cb_ref, h0_ref, cw_ref, cbias_ref, wa_ref, ba_ref, wx_ref, bx_ref, lam_ref,
                out_ref, hl_ref, ext, hc, *, tb_rows, reset_first):
    tb = pl.program_id(1)

    @pl.when(tb == 0)
    def _():
        ext[0:8, :] = jnp.zeros((8, 256), F32)
        ext[5:8, :] = cb_ref[0]
        hc[...] = h0_ref[0]

    u = u_ref[0]
    xb, gate = u[:, :256], u[:, 256:]
    ext[8:8 + tb_rows, :] = xb
    cw = cw_ref[...]
    y = (cbias_ref[...] + cw[0:1] * ext[5:5 + tb_rows, :] + cw[1:2] * ext[6:6 + tb_rows, :]
         + cw[2:3] * ext[7:7 + tb_rows, :] + cw[3:4] * xb)
    ext[0:8, :] = ext[tb_rows:tb_rows + 8, :]
    gate_r = jax.nn.sigmoid(_dot3(y, wa_ref[...]) + ba_ref[...])
    gate_i = jax.nn.sigmoid(_dot3(y, wx_ref[...]) + bx_ref[...])
    lam = lam_ref[...]
    softplus_neg = jnp.maximum(-lam, 0.0) + jnp.log(1.0 + jnp.exp(-jnp.abs(lam)))
    log_a = (-LRU_C * softplus_neg) * gate_r
    a = jnp.exp(log_a)
    mult = jnp.sqrt(1.0 - jnp.exp(2.0 * log_a))
    row = lax.broadcasted_iota(I32, (tb_rows, 256), 0)
    if reset_first:
        mult = jnp.where((row == 0) & (tb == 0), 1.0, mult)
    bv = mult * gate_i * y
    d = 1
    while d < tb_rows:
        live = row >= d
        a_sh = jnp.where(live, pltpu.roll(a, d, 0), 1.0)
        b_sh = jnp.where(live, pltpu.roll(bv, d, 0), 0.0)
        bv = a * b_sh + bv
        a = a * a_sh
        d *= 2
    h = a * hc[...] + bv
    hc[...] = h[tb_rows - 1:tb_rows, :]
    gelu = 0.5 * gate * (1.0 + jnp.tanh(math.sqrt(2.0 / math.pi) * (gate + 0.044715 * gate * gate * gate)))
    out_ref[0] = h * gelu
    hl_ref[0] = h[tb_rows - 1:tb_rows, :]


def _lru(u, conv_buf, h0, prm, tb_rows, reset_first):
    b, t, _ = u.shape
    vec = _full((1, 256))
    kern = functools.partial(_lru_kernel, tb_rows=tb_rows, reset_first=reset_first)
    return pl.pallas_call(
        kern,
        grid=(b, t // tb_rows),
        in_specs=[pl.BlockSpec((1, tb_rows, 512), lambda i, j: (i, j, 0)),
                  pl.BlockSpec((1, 3, 256), lambda i, j: (i, 0, 0)),
                  pl.BlockSpec((1, 1, 256), lambda i, j: (i, 0, 0)),
                  _full((4, 256)), vec, _full((256, 256)), vec, _full((256, 256)), vec, vec],
        out_specs=[pl.BlockSpec((1, tb_rows, 256), lambda i, j: (i, j, 0)),
                   pl.BlockSpec((1, 1, 256), lambda i, j: (i, 0, 0))],
        out_shape=[jax.ShapeDtypeStruct((b, t, 256), F32), jax.ShapeDtypeStruct((b, 1, 256), F32)],
        scratch_shapes=[pltpu.VMEM((tb_rows + 8, 256), F32), pltpu.VMEM((1, 256), F32)],
        compiler_params=_params("parallel", "arbitrary"),
        name="rglru",
    )(u, conv_buf, h0, *prm)


def _sortable(x):
    bits = pltpu.bitcast(jnp.where(jnp.abs(x) < F32_MIN_NORMAL, 0.0, x), I32)
    return jnp.where(bits < 0, bits ^ 0x7FFFFFFF, bits)


def _bisect_threshold(count_gt, t0, cnt0, topk, bits=32):
    def cond(c):
        it, _, cnt = c
        return (it < bits) & (jnp.max(cnt) > topk)

    def body(c):
        it, t, cnt = c
        cand = t + lax.shift_left(jnp.int32(1), bits - 1 - it)
        cnt_cand = count_gt(cand)
        ok = cnt_cand >= topk
        return it + 1, jnp.where(ok, cand, t), jnp.where(ok, cnt_cand, cnt)

    _, t, cnt = lax.while_loop(cond, body, (jnp.int32(0), t0, cnt0))
    return t, cnt


def _dsa_prompt_steps(nq, qb_rows, kb_rows):
    steps = []
    for i in range(nq):
        nv = ((i + 1) * qb_rows + kb_rows - 1) // kb_rows
        steps += [(i, 0, j, nv, j, 0) for j in range(nv)] + [(i, 1, j, nv, nv - 1, j) for j in range(nv)]
    return list(zip(*steps))


def _dsa_prompt_kernel(tab_ref, qb_ref, qi_ref, wt_ref, ki4_ref, kb_ref, vt_ref, o_ref,
                       lhs_s, key_s, top_s, thr_s, qm_s, m_s, l_s, acc_s, bias_s, need_s, rank_s, tie_s,
                       *, qb_rows, kb_rows, sk, topk, n_steps):
    step = pl.program_id(0)
    i, ph, j, nv = (tab_ref[r * n_steps + step] for r in range(4))
    n_pairs = SA_HEADS // 2

    @pl.when((ph == 0) & (j == 0))
    def _init():
        qi, qb = qi_ref[...], qb_ref[...]
        lane = lax.broadcasted_iota(I32, (qb_rows, 128), 1)
        for h in range(SA_HEADS):
            hi, lo = _split_bf16(qi[:, h * 64:(h + 1) * 64])
            lhs_s[h * qb_rows:(h + 1) * qb_rows, :] = jnp.concatenate([hi, hi, lo, lo], axis=1)
        for p in range(n_pairs):
            qp = qb[:, p * 128:(p + 1) * 128]
            zero = jnp.zeros_like(qp)
            qm_s[p] = jnp.concatenate([jnp.where(lane < 64, qp, zero), jnp.where(lane < 64, zero, qp)], axis=0)
        m_s[...] = jnp.full(m_s.shape, NEG_BIG, F32)
        l_s[...] = jnp.zeros(l_s.shape, F32)
        acc_s[...] = jnp.zeros(acc_s.shape, F32)

    @pl.when(ph == 0)
    def _index():
        wt = wt_ref[...]
        qpos = i * qb_rows + lax.broadcasted_iota(I32, (sk, qb_rows), 1)
        for sub in range(kb_rows // sk):
            z = _dot_nt(ki4_ref[sub * sk:(sub + 1) * sk, :], lhs_s[...])
            sc = wt[0:1] * jnp.maximum(z[:, 0:qb_rows], 0.0)
            for h in range(1, SA_HEADS):
                sc = sc + wt[h:h + 1] * jnp.maximum(z[:, h * qb_rows:(h + 1) * qb_rows], 0.0)
            kpos = j * kb_rows + sub * sk + lax.broadcasted_iota(I32, (sk, qb_rows), 0)
            causal = kpos <= qpos
            sc = jnp.where(jnp.abs(sc) < F32_MIN_NORMAL, 0.0, sc)
            bits = pltpu.bitcast(sc, I32)
            key_s[j, sub * sk:(sub + 1) * sk, :] = jnp.where(causal, jnp.where(bits < 0, bits ^ 0x7FFFFFFF, bits),
                                                             INT_MIN)
            top = pltpu.bitcast(bits & jnp.int32(-65536), F32)
            top_s[j, sub * sk:(sub + 1) * sk, :] = jnp.where(causal, top, -jnp.inf).astype(BF16)

    @pl.when((ph == 1) & (j == 0))
    def _threshold():
        one, zero = jnp.ones((), BF16), jnp.zeros((), BF16)

        def count_top(cand):
            cand = jnp.maximum(cand, -32513)
            cand = jnp.where((cand < 0) & (cand > -129), -129, cand)
            pat = jnp.where(cand >= 0, cand, cand ^ 0x7FFF) & 0xFFFF
            cf = pltpu.bitcast(lax.shift_left(pat, 16), F32).astype(BF16)

            def body(jb, acc):
                x = jnp.where(top_s[jb] > cf, one, zero)
                parts = [x[r * 16:(r + 1) * 16] for r in range(kb_rows // 16)]
                while len(parts) > 1:
                    parts = [parts[a] + parts[a + 1] for a in range(0, len(parts), 2)]
                return acc + parts[0].astype(F32)

            acc = lax.fori_loop(0, nv, body, jnp.zeros((16, qb_rows), F32))
            return jnp.sum(acc, axis=0, keepdims=True).astype(I32)

        def top_step(it, c):
            t, cnt = c
            cand = t + lax.shift_left(jnp.int32(1), 15 - it)
            cnt_cand = count_top(cand)
            ok = cnt_cand >= topk
            return jnp.where(ok, cand, t), jnp.where(ok, cnt_cand, cnt)

        t_top0 = jnp.full((1, qb_rows), -32768, I32)
        t_top, cnt = lax.fori_loop(0, 16, top_step, (t_top0, count_top(t_top0)))

        def count_gt(t):
            def body(jb, acc):
                x = jnp.where(key_s[jb] > t, 1, 0)
                return acc + jnp.sum(x.reshape(kb_rows // 8, 8, qb_rows), axis=0)
            acc = lax.fori_loop(0, nv, body, jnp.zeros((8, qb_rows), I32))
            return jnp.sum(acc, axis=0, keepdims=True)

        t, cnt = _bisect_threshold(count_gt, lax.shift_left(t_top, 16) + 0xFFFF, cnt, topk, bits=16)
        thr_s[...] = t
        tied = cnt > topk
        tie_s[0] = jnp.max(jnp.where(tied, 1, 0))
        rank_s[...] = jnp.zeros((1, qb_rows), F32)

        @pl.when(tie_s[0] > 0)
        def _():
            need_s[...] = jnp.where(tied, topk - count_gt(t + 1), kb_rows * nv).astype(F32)

    @pl.when(ph == 1)
    def _attend():
        t = thr_s[...]
        bias_s[...] = jnp.where(key_s[j] > t, 0.0, NEG_BIG)

        @pl.when(tie_s[0] > 0)
        def _drop_late_ties():
            eq = jnp.where(key_s[j] == t + 1, 1.0, 0.0)
            ri = lax.broadcasted_iota(I32, (kb_rows, kb_rows), 0)
            ci = lax.broadcasted_iota(I32, (kb_rows, kb_rows), 1)
            rank = rank_s[...] + _dot((ri >= ci).astype(BF16), eq.astype(BF16))
            bias_s[...] = jnp.where(eq * rank > need_s[...], NEG_BIG, bias_s[...])
            rank_s[...] = rank_s[...] + jnp.sum(eq, axis=0, keepdims=True)

        for rows in (slice(0, kb_rows),):
            bias = bias_s[rows, :]
            bias = jnp.concatenate([bias, bias], axis=1)
            scores = [_dot_nt(kb_ref[rows, p * 128:(p + 1) * 128], qm_s[p]) for p in range(n_pairs)]
            for p in range(n_pairs):
                s = scores[p] + bias
                m_old = m_s[p]
                m_new = jnp.maximum(m_old, jnp.max(s, axis=0, keepdims=True))
                alpha = jnp.exp2(m_old - m_new)
                pr = jnp.exp2(s - m_new)
                l_s[p] = alpha * l_s[p] + jnp.sum(pr, axis=0, keepdims=True)
                m_s[p] = m_new
                prb = pr.astype(BF16)
                for hh in range(2):
                    h = 2 * p + hh
                    cols = slice(hh * qb_rows, (hh + 1) * qb_rows)
                    pv = _dot(vt_ref[h * 64:(h + 1) * 64, rows], prb[:, cols])
                    acc_s[p, hh * 64:(hh + 1) * 64, :] = alpha[:, cols] * acc_s[p, hh * 64:(hh + 1) * 64, :] + pv

    @pl.when((ph == 1) & (j == nv - 1))
    def _finish():
        for p in range(n_pairs):
            inv_l = 1.0 / l_s[p]
            a = acc_s[p]
            a = jnp.concatenate([a[:64] * inv_l[:, :qb_rows], a[64:] * inv_l[:, qb_rows:]], axis=0)
            o_ref[:, p * 128:(p + 1) * 128] = a.T


def _dsa_prompt(qb, qi, wt, ki4, kb, vt, qb_rows, kb_rows):
    s = qb.shape[0]
    nq, nk = s // qb_rows, s // kb_rows
    sk = min(256, kb_rows)
    tab = _dsa_prompt_steps(nq, qb_rows, kb_rows)
    n = len(tab[0])
    qrow = lambda w: pl.BlockSpec((qb_rows, w), lambda st, t: (t[st], 0))
    kern = functools.partial(_dsa_prompt_kernel, qb_rows=qb_rows, kb_rows=kb_rows, sk=sk, topk=min(TOPK, s // 4),
                             n_steps=n)
    grid_spec = pltpu.PrefetchScalarGridSpec(
        num_scalar_prefetch=1,
        grid=(n,),
        in_specs=[qrow(512), qrow(512),
                  pl.BlockSpec((SA_HEADS, qb_rows), lambda st, t: (0, t[st])),
                  pl.BlockSpec((kb_rows, 256), lambda st, t: (t[4 * n + st], 0)),
                  pl.BlockSpec((kb_rows, 512), lambda st, t: (t[5 * n + st], 0)),
                  pl.BlockSpec((512, kb_rows), lambda st, t: (0, t[5 * n + st]))],
        out_specs=qrow(512),
        scratch_shapes=[pltpu.VMEM((SA_HEADS * qb_rows, 256), BF16),
                        pltpu.VMEM((nk, kb_rows, qb_rows), I32),
                        pltpu.VMEM((nk, kb_rows, qb_rows), BF16),
                        pltpu.VMEM((1, qb_rows), I32),
                        pltpu.VMEM((SA_HEADS // 2, 2 * qb_rows, 128), BF16),
                        pltpu.VMEM((SA_HEADS // 2, 1, 2 * qb_rows), F32),
                        pltpu.VMEM((SA_HEADS // 2, 1, 2 * qb_rows), F32),
                        pltpu.VMEM((SA_HEADS // 2, 128, qb_rows), F32),
                        pltpu.VMEM((kb_rows, qb_rows), F32),
                        pltpu.VMEM((1, qb_rows), F32),
                        pltpu.VMEM((1, qb_rows), F32),
                        pltpu.SMEM((1,), I32)],
    )
    return pl.pallas_call(
        kern,
        grid_spec=grid_spec,
        out_shape=jax.ShapeDtypeStruct((s, 512), F32),
        compiler_params=_params("arbitrary"),
        name="dsa_prompt",
    )(jnp.asarray(tab, I32).reshape(-1), qb, qi, wt, ki4, kb, vt)


PAGES_PER_STEP = 16


def _dsa_sample_index_kernel(pt_ref, qi_ref, kw_ref, ki4n_ref, *rest, n_pages, t_len, topk):
    pages = rest[:PAGES_PER_STEP]
    key_ref, thr_ref, need_ref, lhs_s, wcol_s = rest[PAGES_PER_STEP:]
    j = pl.program_id(1)
    n_steps = n_pages // PAGES_PER_STEP

    @pl.when(j == 0)
    def _init():
        qi, kw = qi_ref[0], kw_ref[0]
        hi, lo = _split_bf16(jnp.concatenate([qi[:, h * 64:(h + 1) * 64] for h in range(SA_HEADS)], axis=0))
        lhs_s[...] = jnp.concatenate([hi, hi, lo, lo], axis=1)
        for h in range(SA_HEADS):
            wcol_s[h] = kw[:, 64 + h:65 + h]

    def head_sum(z):
        sc = wcol_s[0] * jnp.maximum(z[0:t_len], 0.0)
        for h in range(1, SA_HEADS):
            sc = sc + wcol_s[h] * jnp.maximum(z[h * t_len:(h + 1) * t_len], 0.0)
        return sc

    hi, lo = _split_bf16(jnp.concatenate([p[...] for p in pages], axis=1))
    keys = _sortable(head_sum(_dot(lhs_s[...], jnp.concatenate([hi, lo, hi, lo], axis=0))))
    for r in range(PAGES_PER_STEP):
        key_ref[0, j * PAGES_PER_STEP + r] = keys[:, r * PAGE:(r + 1) * PAGE]

    @pl.when(j == n_steps - 1)
    def _finish():
        sc = head_sum(_dot_nt(lhs_s[...], ki4n_ref[0]))
        tq = lax.broadcasted_iota(I32, (t_len, 128), 0)
        ts = lax.broadcasted_iota(I32, (t_len, 128), 1)
        key_ref[0, n_pages] = jnp.where(ts <= tq, _sortable(sc), INT_MIN)

        def count_gt(t):
            x = jnp.where(key_ref[0] > t[None], 1, 0)
            return jnp.sum(jnp.sum(x, axis=0), axis=1, keepdims=True)

        t0 = jnp.full((t_len, 1), INT_MIN, I32)
        thr, cnt = _bisect_threshold(count_gt, t0, count_gt(t0), topk)
        thr_ref[0] = jnp.broadcast_to(thr, (t_len, 128))
        need = jnp.where(cnt > topk, topk - count_gt(thr + 1), NO_TIE).astype(F32)
        need_ref[0] = jnp.broadcast_to(need, (t_len, 128))


def _dsa_sample_index(page_table, qi, kw, ki4_new, cache_kidx_t, layer, t_len):
    b = qi.shape[0]
    n_pages = page_table.shape[1]
    n_steps = n_pages // PAGES_PER_STEP

    def page_spec(r):
        return pl.BlockSpec((None, None, 64, PAGE),
                            lambda i, j, pt: (layer, pt[i * n_pages + j * PAGES_PER_STEP + r], 0, 0))

    kern = functools.partial(_dsa_sample_index_kernel, n_pages=n_pages, t_len=t_len,
                             topk=min(TOPK, (n_pages * PAGE + t_len) // 4))
    grid_spec = pltpu.PrefetchScalarGridSpec(
        num_scalar_prefetch=1,
        grid=(b, n_steps),
        in_specs=[pl.BlockSpec((1, t_len, 512), lambda i, j, pt: (i, 0, 0)),
                  pl.BlockSpec((1, t_len, 128), lambda i, j, pt: (i, 0, 0)),
                  pl.BlockSpec((1, PAGE, 256), lambda i, j, pt: (i, 0, 0))]
        + [page_spec(r) for r in range(PAGES_PER_STEP)],
        out_specs=[pl.BlockSpec((1, n_pages + 1, t_len, 128), lambda i, j, pt: (i, 0, 0, 0)),
                   pl.BlockSpec((1, t_len, 128), lambda i, j, pt: (i, 0, 0)),
                   pl.BlockSpec((1, t_len, 128), lambda i, j, pt: (i, 0, 0))],
        scratch_shapes=[pltpu.VMEM((SA_HEADS * t_len, 256), BF16), pltpu.VMEM((SA_HEADS, t_len, 1), F32)],
    )
    return pl.pallas_call(
        kern,
        grid_spec=grid_spec,
        out_shape=[jax.ShapeDtypeStruct((b, n_pages + 1, t_len, 128), I32),
                   jax.ShapeDtypeStruct((b, t_len, 128), I32),
                   jax.ShapeDtypeStruct((b, t_len, 128), F32)],
        compiler_params=_params("parallel", "arbitrary"),
        name="dsa_sample_index",
    )(page_table.reshape(-1), qi, kw, ki4_new, *([cache_kidx_t] * PAGES_PER_STEP))


def _dsa_sample_attend_kernel(pt_ref, qb_ref, key_ref, keyn_ref, thr_ref, need_ref, kn_ref, vn_ref, *rest,
                              n_pages, t_len):
    kpages = rest[:PAGES_PER_STEP]
    vpages = rest[PAGES_PER_STEP:2 * PAGES_PER_STEP]
    o_ref, qbd_s, m_s, l_s, acc_s, bias_s, rank_s, tie_s = rest[2 * PAGES_PER_STEP:]
    j = pl.program_id(1)
    n_steps = n_pages // PAGES_PER_STEP
    rows = SA_HEADS * t_len
    thr = thr_ref[0][:, 0:1]
    need = need_ref[0][:, 0:1]

    @pl.when(j == 0)
    def _init():
        qb = qb_ref[0].astype(F32)
        lane = lax.broadcasted_iota(I32, (t_len, 512), 1) // 64
        qbd = jnp.concatenate([jnp.where(lane == h, qb, 0.0) for h in range(SA_HEADS)], axis=0)
        qbd_s[...] = qbd.astype(BF16)
        m_s[...] = jnp.full(m_s.shape, NEG_BIG, F32)
        l_s[...] = jnp.zeros(l_s.shape, F32)
        acc_s[...] = jnp.zeros(acc_s.shape, F32)
        rank_s[...] = jnp.zeros(rank_s.shape, F32)
        tie_s[0] = (jnp.min(need) < NO_TIE).astype(I32)

    def update(keys, s, pv):
        n = keys.shape[1]
        bias_s[:, :n] = jnp.where(keys > thr, 0.0, NEG_BIG)

        @pl.when(tie_s[0] > 0)
        def _drop_late_ties():
            w = min(n, 512)
            upto = (lax.broadcasted_iota(I32, (w, w), 0) <= lax.broadcasted_iota(I32, (w, w), 1)).astype(BF16)
            for c in range(n // w):
                cols = slice(c * w, (c + 1) * w)
                eq = jnp.where(keys[:, cols] == thr + 1, 1.0, 0.0)
                rank = rank_s[...] + _dot(eq.astype(BF16), upto)
                bias_s[:, cols] = jnp.where(eq * rank > need, NEG_BIG, bias_s[:, cols])
                rank_s[...] = rank_s[...] + jnp.sum(eq, axis=1, keepdims=True)

        bias = bias_s[:, :n]
        s = s + jnp.concatenate([bias] * SA_HEADS, axis=0)
        m_old = m_s[...]
        m_new = jnp.maximum(m_old, jnp.max(s, axis=1, keepdims=True))
        alpha = jnp.exp2(m_old - m_new)
        pr = jnp.exp2(s - m_new)
        l_s[...] = alpha * l_s[...] + jnp.sum(pr, axis=1, keepdims=True)
        acc_s[...] = alpha * acc_s[...] + pv(pr.astype(BF16))
        m_s[...] = m_new

    kcat = jnp.concatenate([p[...].astype(BF16) for p in kpages], axis=1)
    vcat = jnp.concatenate([p[...].astype(BF16) for p in vpages], axis=1)
    keys = jnp.concatenate([key_ref[0, r] for r in range(PAGES_PER_STEP)], axis=1)
    update(keys, _dot(qbd_s[...], kcat), lambda pr: _dot_nt(pr, vcat))

    @pl.when(j == n_steps - 1)
    def _finish():
        kn, vn = kn_ref[0].astype(BF16), vn_ref[0].astype(BF16)
        update(keyn_ref[0, 0], _dot_nt(qbd_s[...], kn), lambda pr: _dot(pr, vn))
        o = acc_s[...] / l_s[...]
        lane = lax.broadcasted_iota(I32, (t_len, 512), 1) // 64
        out = jnp.zeros((t_len, 512), F32)
        for h in range(SA_HEADS):
            out = out + jnp.where(lane == h, o[h * t_len:(h + 1) * t_len], 0.0)
        o_ref[0] = out


def _dsa_sample_attend(page_table, qb, keys, thr, need, k_new, v_new, cache_k_t, cache_v_t, layer, t_len):
    b = qb.shape[0]
    n_pages = page_table.shape[1]
    n_steps = n_pages // PAGES_PER_STEP
    rows = SA_HEADS * t_len

    def page_spec(r):
        return pl.BlockSpec((None, None, 512, PAGE),
                            lambda i, j, pt: (layer, pt[i * n_pages + j * PAGES_PER_STEP + r], 0, 0))

    per_b = lambda n: pl.BlockSpec((1, t_len, n), lambda i, j, pt: (i, 0, 0))
    kern = functools.partial(_dsa_sample_attend_kernel, n_pages=n_pages, t_len=t_len)
    grid_spec = pltpu.PrefetchScalarGridSpec(
        num_scalar_prefetch=1,
        grid=(b, n_steps),
        in_specs=[per_b(512),
                  pl.BlockSpec((1, PAGES_PER_STEP, t_len, 128), lambda i, j, pt: (i, j, 0, 0)),
                  pl.BlockSpec((1, 1, t_len, 128), lambda i, j, pt: (i, n_pages, 0, 0)),
                  per_b(128), per_b(128),
                  pl.BlockSpec((1, PAGE, 512), lambda i, j, pt: (i, 0, 0)),
                  pl.BlockSpec((1, PAGE, 512), lambda i, j, pt: (i, 0, 0))]
        + [page_spec(r) for r in range(PAGES_PER_STEP)] * 2,
        out_specs=per_b(512),
        scratch_shapes=[pltpu.VMEM((rows, 512), BF16), pltpu.VMEM((rows, 1), F32), pltpu.VMEM((rows, 1), F32),
                        pltpu.VMEM((rows, 512), F32), pltpu.VMEM((t_len, PAGES_PER_STEP * PAGE), F32),
                        pltpu.VMEM((t_len, 1), F32), pltpu.SMEM((1,), I32)],
    )
    return pl.pallas_call(
        kern,
        grid_spec=grid_spec,
        out_shape=jax.ShapeDtypeStruct((b, t_len, 512), F32),
        compiler_params=_params("parallel", "arbitrary"),
        name="dsa_sample_attend",
    )(page_table.reshape(-1), qb, keys, keys, thr, need, k_new, v_new,
      *([cache_k_t] * PAGES_PER_STEP), *([cache_v_t] * PAGES_PER_STEP))


def _matmul_kernel(x_ref, w_ref, o_ref):
    o_ref[...] = _dot(x_ref[...].astype(BF16), w_ref[...])


def _matmul(x, w):
    m, n = x.shape[0], w.shape[1]
    return pl.pallas_call(
        _matmul_kernel,
        grid=(1,),
        in_specs=[_full(x.shape), _full(w.shape)],
        out_specs=_full((m, n)),
        out_shape=jax.ShapeDtypeStruct((m, n), F32),
        compiler_params=_params("arbitrary"),
        name="mem_kv_proj",
    )(x, w)


def _post1_kernel(x_ref, yrw_ref, ysa_ref, ylru_ref, wo1_ref, wo2_ref, wo3_ref, g_ref, wq_ref, x1_ref, qm_ref):
    x1 = (x_ref[...] + _dot(yrw_ref[...].astype(BF16), wo1_ref[...])
          + _dot(ysa_ref[...].astype(BF16), wo2_ref[...]) + _dot(ylru_ref[...].astype(BF16), wo3_ref[...]))
    x1_ref[...] = x1
    qm = _dot(_rms(x1, g_ref[...]).astype(BF16), wq_ref[...])
    qm_ref[...] = (qm * (MEM_HEAD_DIM ** -0.5)).astype(BF16)


def _post1(x, yrw, ysa, ylru, wo1, wo2, wo3, g, wq, tm):
    m = x.shape[0]
    row = lambda n: pl.BlockSpec((tm, n), lambda i: (i, 0))
    return pl.pallas_call(
        _post1_kernel,
        grid=(m // tm,),
        in_specs=[row(1024), row(256), row(512), row(256), _full(wo1.shape), _full(wo2.shape), _full(wo3.shape),
                  _full((1, 1024)), _full(wq.shape)],
        out_specs=[row(1024), row(1024)],
        out_shape=[jax.ShapeDtypeStruct((m, 1024), F32), jax.ShapeDtypeStruct((m, 1024), BF16)],
        compiler_params=_params("parallel"),
        name="out_proj_memq",
    )(x, yrw, ysa, ylru, wo1, wo2, wo3, g, wq)


def _mem_attn_kernel(q_ref, mk_ref, mv_ref, o_ref):
    q = q_ref[0]
    mk = mk_ref[0].astype(BF16)
    mv = mv_ref[0].astype(BF16)
    for h in range(MEM_HEADS):
        sl = slice(h * MEM_HEAD_DIM, (h + 1) * MEM_HEAD_DIM)
        s = _dot_nt(q[:, sl], mk[:, sl])
        p = jnp.exp(s - jnp.max(s, axis=1, keepdims=True))
        o = _dot(p.astype(BF16), mv[:, sl]) / jnp.sum(p, axis=1, keepdims=True)
        o_ref[0, :, sl] = o.astype(BF16)


def _mem_attn(q, mk, mv, tm):
    b, t, _ = q.shape
    return pl.pallas_call(
        _mem_attn_kernel,
        grid=(b, t // tm),
        in_specs=[pl.BlockSpec((1, tm, 1024), lambda i, j: (i, j, 0)),
                  pl.BlockSpec((1, 256, 1024), lambda i, j: (i, 0, 0)),
                  pl.BlockSpec((1, 256, 1024), lambda i, j: (i, 0, 0))],
        out_specs=pl.BlockSpec((1, tm, 1024), lambda i, j: (i, j, 0)),
        out_shape=jax.ShapeDtypeStruct((b, t, 1024), BF16),
        compiler_params=_params("parallel", "parallel"),
        name="mem_attn",
    )(q, mk, mv)


def _post2_kernel(x1_ref, o_ref, wo_ref, g_ref, wg_ref, wu_ref, wd_ref, gf_ref, out_ref, *, last):
    x2 = x1_ref[...] + _dot(o_ref[...], wo_ref[...])
    hb = _rms(x2, g_ref[...]).astype(BF16)
    gt = _dot(hb, wg_ref[...])
    up = _dot(hb, wu_ref[...])
    act = (gt * jax.nn.sigmoid(gt) * up).astype(BF16)
    x3 = x2 + _dot(act, wd_ref[...])
    out_ref[...] = _rms(x3, gf_ref[...]) if last else x3


def _post2(x1, o, wo, g, wg, wu, wd, gf, tm, last):
    m = x1.shape[0]
    row = lambda n: pl.BlockSpec((tm, n), lambda i: (i, 0))
    once = lambda w: pl.BlockSpec(w.shape, lambda i: (0, 0), pipeline_mode=pl.Buffered(1))
    return pl.pallas_call(
        functools.partial(_post2_kernel, last=last),
        grid=(m // tm,),
        in_specs=[row(1024), row(1024), once(wo), _full((1, 1024)), once(wg), once(wu), once(wd), _full((1, 1024))],
        out_specs=row(1024),
        out_shape=jax.ShapeDtypeStruct((m, 1024), F32),
        compiler_params=_params("parallel"),
        name="memo_swiglu",
    )(x1, o, wo, g, wg, wu, wd, gf)


def _blockdiag4(w):
    n = w.shape[-1]
    return jnp.einsum("gij,gh->gihj", w, jnp.eye(4, dtype=w.dtype)).reshape(4 * n, 4 * n)


def _state_to_blockdiag(s):
    b = s.shape[0]
    return jnp.einsum("bhvk,hg->bhvgk", s, jnp.eye(4, dtype=s.dtype)).reshape(b, 256, 256)


def _blockdiag_to_state(sbd):
    b = sbd.shape[0]
    return jnp.einsum("bhvgk,hg->bhvk", sbd.reshape(b, 4, 64, 4, 64), jnp.eye(4, dtype=sbd.dtype))


def _layer_weights(l, w_in, w_out, rw, lru, g_mix, g_mem, w_mq, w_mk, w_mv, w_mo, g_ffn, w_gate, w_up, w_down):
    wi = w_in[l]
    o = RW_COLS
    cut = lambda a, n: wi[:, a:a + n].astype(BF16)
    wkw = jnp.pad(wi[:, o + 2048:o + 2120], ((0, 0), (0, 56))).astype(BF16)
    in_ws = (cut(0, 1024), cut(o, 512), cut(o + 512, 512), cut(o + 1024, 512), cut(o + 1536, 512), wkw,
             cut(o + 2120, 512))
    (rw_mu, rw_w0, rw_w2, rw_a0, rw_a2, rw_g2, rw_kk, rw_ka, rw_rk, rw_ln_w, rw_ln_b) = rw
    z = jnp.zeros((64, 256), F32)
    v256 = lambda a: a[l].reshape(1, 256)
    rw_prm = (rw_mu[l].reshape(1, 1024), v256(rw_w0), jnp.concatenate([rw_w2[l], z], axis=0), v256(rw_a0),
              jnp.concatenate([z, rw_a2[l]], axis=0), rw_g2[l], v256(rw_kk), v256(rw_ka), v256(rw_rk),
              v256(rw_ln_w), v256(rw_ln_b))
    (lru_conv_w, lru_conv_b, lru_wa, lru_ba, lru_wx, lru_bx, lru_lambda) = lru
    lru_prm = (lru_conv_w[l], v256(lru_conv_b), _blockdiag4(lru_wa[l]), v256(lru_ba), _blockdiag4(lru_wx[l]),
               v256(lru_bx), v256(lru_lambda))
    wo = w_out[l].astype(BF16)
    return dict(
        in_ws=in_ws, rw=rw_prm, lru=lru_prm, g_mix=g_mix[l].reshape(1, 1024),
        wo=(wo[:256], wo[256:768], wo[768:]), g_mem=g_mem[l].reshape(1, 1024), wq=w_mq[l].astype(BF16),
        wmk=w_mk[l].astype(BF16), wmv=w_mv[l].astype(BF16), wmo=w_mo[l].astype(BF16),
        g_ffn=g_ffn[l].reshape(1, 1024), wg=w_gate[l].astype(BF16), wu=w_up[l].astype(BF16),
        wd=w_down[l].astype(BF16))


def _pick_tile(n, pref):
    t = min(n, pref)
    while n % t:
        t //= 2
    return t


def _prompt_layer(x, mem, w, tabs, g_final, last):
    s = x.shape[0]
    tm = _pick_tile(s, 256)
    urw, kt, vt, qi, _, kwt, ulru, qb, kb, vtb, ki4 = _in_proj(x, w["g_mix"], w["in_ws"], tabs, tm)
    tb = _pick_tile(s, 512)
    y_rw, sbd = _rwkv(urw[None], jnp.zeros((1, 1, 1024), F32), jnp.zeros((1, 256, 256), F32), w["rw"], tb, tb)
    y_lru, h_last = _lru(ulru[None], jnp.zeros((1, 3, 256), F32), jnp.zeros((1, 1, 256), F32), w["lru"], tb, True)
    y_sa = _dsa_prompt(qb, qi, kwt[64:64 + SA_HEADS], ki4, kb, vtb, _pick_tile(s, 256), _pick_tile(s, 1024))
    mk = _matmul(mem, w["wmk"])
    mv = _matmul(mem, w["wmv"])
    x1, qm = _post1(x, y_rw[0], y_sa, y_lru[0], *w["wo"], w["g_mem"], w["wq"], tm)
    o = _mem_attn(qm[None], mk[None], mv[None], tm)
    x3 = _post2(x1, o[0], w["wmo"], w["g_ffn"], w["wg"], w["wu"], w["wd"], g_final, tm, last)
    rows_major = lambda a: jnp.transpose(a.reshape(SA_HEADS, HEAD_DIM, s), (2, 0, 1))[None]
    new = dict(k=rows_major(kt), v=rows_major(vt), kidx=kwt[:64].T[None],
               mk=mk.reshape(1, 256, 4, 256), mv=mv.reshape(1, 256, 4, 256), rwkv=_blockdiag_to_state(sbd),
               shift=urw[None, s - 1], h=h_last[:, 0], conv=ulru[None, s - 3:, :256])
    return x3, new


def _sample_layer(x, l, w, tabs, g_final, cache_k, cache_v, cache_kidx, cache_mem_k, cache_mem_v, state_rwkv,
                  state_rwkv_shift, state_lru_h, state_lru_conv, page_table, db, t, last):
    m = db * t
    urw, kt, vt, qi, kw, _, ulru, qb, _, _, ki4 = _in_proj(x, w["g_mix"], w["in_ws"], tabs, m)
    k, v = kt.T, vt.T
    urw3 = urw.reshape(db, t, 1024)
    u_pad = jnp.pad(urw3, ((0, 0), (0, RW_CHUNK - t), (0, 0)))
    y_rw, sbd = _rwkv(u_pad, state_rwkv_shift[l][:, None], _state_to_blockdiag(state_rwkv[l]), w["rw"], RW_CHUNK, t)
    ulru3 = ulru.reshape(db, t, 512)
    y_lru, h_last = _lru(ulru3, state_lru_conv[l], state_lru_h[l][:, None], w["lru"], t, False)
    pad_new = lambda a: jnp.pad(a.reshape(db, t, -1), ((0, 0), (0, PAGE - t), (0, 0)))
    keys, thr, need = _dsa_sample_index(page_table, qi.reshape(db, t, 512), kw.reshape(db, t, 128), pad_new(ki4),
                                        cache_kidx, l, t)
    y_sa = _dsa_sample_attend(page_table, qb.reshape(db, t, 512), keys, thr, need, pad_new(k), pad_new(v),
                              cache_k, cache_v, l, t)
    x1, qm = _post1(x, y_rw[:, :t].reshape(m, 256), y_sa.reshape(m, 512), y_lru.reshape(m, 256), *w["wo"],
                    w["g_mem"], w["wq"], m)
    o = _mem_attn(qm.reshape(db, t, 1024), cache_mem_k[l].reshape(db, 256, 1024),
                  cache_mem_v[l].reshape(db, 256, 1024), t)
    x3 = _post2(x1, o.reshape(m, 1024), w["wmo"], w["g_ffn"], w["wg"], w["wu"], w["wd"], g_final, m, last)
    conv = jnp.concatenate([state_lru_conv[l], ulru3[:, :, :256]], axis=1)[:, t:]
    new = dict(k=k.reshape(db, t, 8, 64), v=v.reshape(db, t, 8, 64), kidx=kw.reshape(db, t, 128)[:, :, :64],
               rwkv=_blockdiag_to_state(sbd), shift=urw3[:, t - 1], h=h_last[:, 0], conv=conv)
    return x3, new


def kernel(x_prompt, x_sample, mem_prompt, cache_k, cache_v, cache_kidx, cache_mem_k, cache_mem_v, state_rwkv, state_rwkv_shift, state_lru_h, state_lru_conv, page_table, g_mix, w_in, w_out, rw_mu, rw_w0, rw_w2, rw_a0, rw_a2, rw_g2, rw_kk, rw_ka, rw_rk, rw_ln_w, rw_ln_b, lru_conv_w, lru_conv_b, lru_wa, lru_ba, lru_wx, lru_bx, lru_lambda, g_mem, w_mq, w_mk, w_mv, w_mo, g_ffn, w_gate, w_up, w_down, g_final):
    depth = w_in.shape[0]
    _, s, _ = x_prompt.shape
    db, t, _ = x_sample.shape
    past_len = page_table.shape[1] * PAGE
    rw = (rw_mu, rw_w0, rw_w2, rw_a0, rw_a2, rw_g2, rw_kk, rw_ka, rw_rk, rw_ln_w, rw_ln_b)
    lru = (lru_conv_w, lru_conv_b, lru_wa, lru_ba, lru_wx, lru_bx, lru_lambda)
    tabs_p = _rope_tables(jnp.arange(s))
    tabs_s = _rope_tables(jnp.tile(past_len + jnp.arange(t), db))
    gf = g_final.reshape(1, 1024)
    n_pool = cache_k.shape[1]
    cache_kidx = jnp.transpose(cache_kidx, (0, 1, 3, 2))
    cache_k = jnp.transpose(cache_k, (0, 1, 3, 4, 2)).reshape(depth, n_pool, SA_WIDTH, PAGE)
    cache_v = jnp.transpose(cache_v, (0, 1, 3, 4, 2)).reshape(depth, n_pool, SA_WIDTH, PAGE)
    xp, xs = x_prompt[0], x_sample.reshape(db * t, 1024)
    mem = mem_prompt[0]
    news_p, news_s = [], []
    for l in range(depth):
        w = _layer_weights(l, w_in, w_out, rw, lru, g_mix, g_mem, w_mq, w_mk, w_mv, w_mo, g_ffn, w_gate, w_up,
                           w_down)
        last = l == depth - 1
        xp, new_p = _prompt_layer(xp, mem, w, tabs_p, gf, last)
        xs, new_s = _sample_layer(xs, l, w, tabs_s, gf, cache_k, cache_v, cache_kidx, cache_mem_k, cache_mem_v,
                                  state_rwkv, state_rwkv_shift, state_lru_h, state_lru_conv, page_table, db, t, last)
        news_p.append(new_p)
        news_s.append(new_s)
    stk = lambda news, name: jnp.stack([n[name] for n in news])
    return (xp[None], xs.reshape(db, t, 1024),
            stk(news_p, "k"), stk(news_p, "v"), stk(news_p, "kidx"), stk(news_p, "mk"), stk(news_p, "mv"),
            stk(news_p, "rwkv"), stk(news_p, "shift"), stk(news_p, "h"), stk(news_p, "conv"),
            stk(news_s, "k"), stk(news_s, "v"), stk(news_s, "kidx"), stk(news_s, "rwkv"), stk(news_s, "shift"),
            stk(news_s, "h"), stk(news_s, "conv"))
```

```python
import functools
import math

import jax
import jax.numpy as jnp
from jax import lax
from jax.experimental import pallas as pl
from jax.experimental.pallas import tpu as pltpu

F32 = jnp.float32
BF16 = jnp.bfloat16
I32 = jnp.int32

D_MODEL = 1024
HEAD_DIM = 64
RW_HEADS = 4
RW_WIDTH = 256
RW_COLS = 1024
RW_GN_EPS = 64e-5
SA_HEADS = 8
SA_WIDTH = 512
IDX_SCALE = 512.0 ** -0.5
TOPK = 256
LRU_WIDTH = 256
LRU_C = 8.0
D_FF = 2816
MEM_HEADS = 4
MEM_HEAD_DIM = 256
ROPE_THETA = 10000.0
NORM_EPS = 1e-6
PAGE = 128
LOG2E = 1.4426950408889634
INT_MIN = -2 ** 31
F32_MIN_NORMAL = 2.0 ** -126
NO_TIE = 2 ** 30
V_ROWS = HEAD_DIM + 16
NEG_BIG = -1e30
VMEM_LIMIT = 56 * 1024 * 1024
RW_CHUNK = 64


def _params(*sem):
    return pltpu.CompilerParams(dimension_semantics=sem, vmem_limit_bytes=VMEM_LIMIT)


def _full(shape):
    n = len(shape)
    return pl.BlockSpec(shape, lambda *_: (0,) * n)


def _dot(a, b):
    return jnp.dot(a, b, preferred_element_type=F32)


def _dot_nt(a, b):
    return lax.dot_general(a, b, (((1,), (1,)), ((), ())), preferred_element_type=F32)


def _rms(x, g):
    return x * lax.rsqrt(jnp.mean(x * x, axis=-1, keepdims=True) + NORM_EPS) * g


def _split_bf16(x):
    hi = x.astype(BF16)
    lo = (x - hi.astype(F32)).astype(BF16)
    return hi, lo


_NN = (((1,), (0,)), ((), ()))
_NT = (((1,), (1,)), ((), ()))
_TN = (((0,), (0,)), ((), ()))


def _dot1(a, b, dims=_NN):
    return lax.dot_general(a.astype(BF16), b.astype(BF16), dims, preferred_element_type=F32)


def _dot3(a, b, dims=_NN):
    ah, al = _split_bf16(a)
    bh, bl = _split_bf16(b)
    d = lambda x, y: lax.dot_general(x, y, dims, preferred_element_type=F32)
    return d(ah, bh) + (d(ah, bl) + d(al, bh))


def _dot_exact_rhs(a, m):
    hi = a.astype(BF16)
    r = a - hi.astype(F32)
    mid = r.astype(BF16)
    lo = (r - mid.astype(F32)).astype(BF16)
    return _dot(hi, m) + (_dot(mid, m) + _dot(lo, m))


def _dot_exact_lhs(m, b):
    hi = b.astype(BF16)
    r = b - hi.astype(F32)
    mid = r.astype(BF16)
    lo = (r - mid.astype(F32)).astype(BF16)
    return _dot(m, hi) + (_dot(m, mid) + _dot(m, lo))


def _rope(u, cos, sin):
    w = u.shape[-1]
    lane = lax.broadcasted_iota(I32, u.shape, 1)
    rot = jnp.where((lane & 63) < 32, pltpu.roll(u, w - 32, 1), pltpu.roll(u, 32, 1))
    return u * cos + rot * sin


def _in_proj_kernel(x_ref, g_ref, wrw_ref, wq_ref, wk_ref, wv_ref, wqi_ref, wkw_ref, wlru_ref,
                    cos_ref, sin_ref, ckw_ref, skw_ref,
                    urw_ref, kt_ref, vt_ref, qi_ref, kw_ref, kwt_ref, ulru_ref, qb_ref, kb_ref, vtb_ref, ki4_ref):
    hb = _rms(x_ref[...], g_ref[...]).astype(BF16)
    urw_ref[...] = _dot(hb, wrw_ref[...])
    ulru_ref[...] = _dot(hb, wlru_ref[...])
    cos = jnp.concatenate([cos_ref[...]] * 4, axis=1)
    sin = jnp.concatenate([sin_ref[...]] * 4, axis=1)
    q = _rope(_dot(hb, wq_ref[...]), cos, sin)
    k = _rope(_dot(hb, wk_ref[...]), cos, sin)
    v = _dot(hb, wv_ref[...])
    qi_ref[...] = _rope(_dot(hb, wqi_ref[...]), cos, sin)
    kw = _rope(_dot(hb, wkw_ref[...]), ckw_ref[...], skw_ref[...])
    vt = v.T
    kt_ref[...] = k.T
    vt_ref[...] = vt
    kw_ref[...] = kw
    kwt_ref[...] = kw.T
    qb_ref[...] = (q * (HEAD_DIM ** -0.5 * LOG2E)).astype(BF16)
    kb_ref[...] = k.astype(BF16)
    ones = jnp.ones((V_ROWS - HEAD_DIM, vt.shape[1]), F32)
    vtb_ref[...] = jnp.concatenate(
        [blk for h in range(SA_HEADS) for blk in (vt[h * HEAD_DIM:(h + 1) * HEAD_DIM], ones)], axis=0).astype(BF16)
    hi, lo = _split_bf16(kw[:, :64])
    ki4_ref[...] = jnp.concatenate([hi, lo, hi, lo], axis=1)


def _in_proj(x, g, ws, tabs, tm):
    m = x.shape[0]
    wrw, wq, wk, wv, wqi, wkw, wlru = ws
    row = lambda n: pl.BlockSpec((tm, n), lambda i: (i, 0))
    col = lambda n: pl.BlockSpec((n, tm), lambda i: (0, i))
    sds = jax.ShapeDtypeStruct
    out_shape = [sds((m, 1024), F32), sds((512, m), F32), sds((512, m), F32), sds((m, 512), F32),
                 sds((m, 128), F32), sds((128, m), F32), sds((m, 512), F32), sds((m, 512), BF16),
                 sds((m, 512), BF16), sds((SA_HEADS * V_ROWS, m), BF16), sds((m, 256), BF16)]
    return pl.pallas_call(
        _in_proj_kernel,
        grid=(m // tm,),
        in_specs=[row(1024), _full((1, 1024))] + [_full(w.shape) for w in ws] + [row(128)] * 4,
        out_specs=[row(1024), col(512), col(512), row(512), row(128), col(128), row(512), row(512), row(512),
                   col(SA_HEADS * V_ROWS), row(256)],
        out_shape=out_shape,
        compiler_params=_params("parallel"),
        name="in_proj",
    )(x, g, wrw, wq, wk, wv, wqi, wkw, wlru, *tabs)


def _rope_tables(pos):
    half = HEAD_DIM // 2
    inv = ROPE_THETA ** (-jnp.arange(half, dtype=F32) / half)
    ang = pos.astype(F32)[:, None] * inv[None, :]
    c, s = jnp.cos(ang), jnp.sin(ang)
    m = pos.shape[0]
    cos = jnp.concatenate([c, c, c, c], axis=1)
    sin = jnp.concatenate([-s, s, -s, s], axis=1)
    ckw = jnp.concatenate([c, c, jnp.full((m, SA_HEADS), IDX_SCALE, F32), jnp.zeros((m, 56), F32)], axis=1)
    skw = jnp.concatenate([-s, s, jnp.zeros((m, 64), F32)], axis=1)
    return cos, sin, ckw, skw


def _rwkv_kernel(u_ref, prev_ref, s0_ref, mu_ref, w0_ref, w2_ref, a0_ref, a2_ref, g2_ref, kk_ref, ka_ref,
                 rk_ref, lnw_ref, lnb_ref, y_ref, sout_ref,
                 sbd, carry, r_s, kn_s, a_s, k2_s, v_s, lw_s, y_s, *, tb_rows, n_valid):
    c_len = RW_CHUNK
    tb = pl.program_id(1)

    @pl.when(tb == 0)
    def _():
        sbd[...] = s0_ref[0]
        carry[...] = prev_ref[0]

    u = u_ref[0]
    row = lax.broadcasted_iota(I32, u.shape, 0)
    u_prev = jnp.where(row == 0, carry[...], pltpu.roll(u, 1, 0))
    last = min(n_valid, tb_rows) - 1
    carry[...] = u[last:last + 1, :]
    us = u + mu_ref[...] * (u_prev - u)
    r, k, v = us[:, 0:256], us[:, 256:512], us[:, 512:768]
    wa, gd = us[:, 768:896], us[:, 896:1024]

    li = lax.broadcasted_iota(I32, (256, 256), 0) // 64
    lj = lax.broadcasted_iota(I32, (256, 256), 1) // 64
    blockdiag = (li == lj).astype(F32)
    head_sum = functools.partial(_dot_exact_rhs, m=blockdiag.astype(BF16))

    xw = w0_ref[...] + _dot3(jnp.tanh(wa), w2_ref[...])
    logw = -math.exp(-0.5) * jax.nn.sigmoid(xw)
    a = jax.nn.sigmoid(a0_ref[...] + _dot3(wa, a2_ref[...]))
    g = _dot3(jax.nn.sigmoid(gd), g2_ref[...])
    kkv = k * kk_ref[...]
    kn = kkv / jnp.maximum(jnp.sqrt(head_sum(kkv * kkv)), 1e-12)
    k2 = k * (1.0 + (a - 1.0) * ka_ref[...])
    bonus = head_sum(r * k2 * rk_ref[...]) * v
    if n_valid < tb_rows:
        ok = lax.broadcasted_iota(I32, (tb_rows, 256), 0) < n_valid
        logw = jnp.where(ok, logw, 0.0)
        kn = jnp.where(ok, kn, 0.0)
        k2 = jnp.where(ok, k2, 0.0)
        v = jnp.where(ok, v, 0.0)
    r_s[...] = r
    kn_s[...] = kn
    a_s[...] = a
    k2_s[...] = k2
    v_s[...] = v
    lw_s[...] = logw

    lane = lax.broadcasted_iota(I32, (1, 256), 1) // 64
    hmask = [(lane == h).astype(F32) for h in range(RW_HEADS)]
    ci = lax.broadcasted_iota(I32, (c_len, c_len), 0)
    cj = lax.broadcasted_iota(I32, (c_len, c_len), 1)
    tri_incl = (ci >= cj).astype(BF16)
    eye = (ci == cj).astype(F32)
    mi = lax.broadcasted_iota(I32, (8 * c_len, 2 * c_len), 0)
    mj = lax.broadcasted_iota(I32, (8 * c_len, 2 * c_len), 1)
    mt, ms = mi & (c_len - 1), mj & (c_len - 1)
    keep = ms < mt + jnp.where(mi < 4 * c_len, 0, 1)
    pick = lambda x: sum(x[h * c_len:(h + 1) * c_len] * hmask[h] for h in range(RW_HEADS))
    n_sq = int(math.log2(c_len)) - 1

    def prepare(c):
        sl = pl.ds(pl.multiple_of(c * c_len, c_len), c_len)
        lw, rc, knc, ac, k2c, vc = lw_s[sl, :], r_s[sl, :], kn_s[sl, :], a_s[sl, :], k2_s[sl, :], v_s[sl, :]
        lcum = _dot_exact_lhs(tri_incl, lw)
        gam, gex, gin = jnp.exp(lcum), jnp.exp(lcum - lw), jnp.exp(-lcum)
        rt, at, bt, kt = rc * gam, knc * gex, -(knc * ac) * gin, k2c * gin
        g_end = gam[c_len - 1:c_len, :]
        lhs = jnp.concatenate([at * hm for hm in hmask] + [rt * hm for hm in hmask], axis=0)
        m = jnp.where(keep, _dot1(lhs, jnp.concatenate([bt, kt], axis=0), _NT), 0.0)
        w1 = pick(_dot1(m[:4 * c_len], jnp.concatenate([jnp.zeros_like(vc), vc], axis=0)))
        return dict(sl=sl, vc=vc, m=m, w1=w1, ar=jnp.concatenate([at, rt], axis=0), g_end=g_end,
                    bk=jnp.concatenate([bt * g_end, kt * g_end], axis=0))

    def invert(ms):
        qs = [m[h * c_len:(h + 1) * c_len, 0:c_len] for m in ms for h in range(RW_HEADS)]
        xs = [eye + q for q in qs]
        for _ in range(n_sq):
            qs = [_dot1(q, q) for q in qs]
            xs = [x + _dot1(x, q) for x, q in zip(xs, qs)]
        return [jnp.concatenate(xs[n * RW_HEADS:(n + 1) * RW_HEADS], axis=0) for n in range(len(ms))]

    def advance(d, inv):
        s = sbd[...]
        asrs = _dot1(d["ar"], s, _NT)
        p = pick(_dot1(inv, asrs[:c_len] + d["w1"]))
        pv = jnp.concatenate([p, d["vc"]], axis=0)
        y_s[d["sl"], :] = asrs[c_len:] + pick(_dot1(d["m"][4 * c_len:], pv))
        sbd[...] = s * d["g_end"] + _dot1(pv, d["bk"], _TN) * blockdiag

    n_chunks = tb_rows // c_len
    group = 4 if n_chunks % 4 == 0 else 1

    def chunk_group(gi, _):
        ds = [prepare(gi * group + n) for n in range(group)]
        for d, inv in zip(ds, invert([d["m"] for d in ds])):
            advance(d, inv)
        return 0

    lax.fori_loop(0, n_chunks // group, chunk_group, 0)

    y = y_s[...]
    mean = head_sum(y) * (1.0 / HEAD_DIM)
    yc = y - mean
    var = head_sum(yc * yc) * (1.0 / HEAD_DIM)
    yn = yc * lax.rsqrt(var + RW_GN_EPS) * lnw_ref[...] + lnb_ref[...]
    y_ref[0] = (yn + bonus) * g
    sout_ref[0] = sbd[...]


def _rwkv(u, prev, s0bd, prm, tb_rows, n_valid):
    b, t, _ = u.shape
    vec = lambda n: _full((1, n))
    kern = functools.partial(_rwkv_kernel, tb_rows=tb_rows, n_valid=n_valid)
    return pl.pallas_call(
        kern,
        grid=(b, t // tb_rows),
        in_specs=[pl.BlockSpec((1, tb_rows, 1024), lambda i, j: (i, j, 0)),
                  pl.BlockSpec((1, 1, 1024), lambda i, j: (i, 0, 0)),
                  pl.BlockSpec((1, 256, 256), lambda i, j: (i, 0, 0)),
                  vec(1024), vec(256), _full((128, 256)), vec(256), _full((128, 256)), _full((128, 256)),
                  vec(256), vec(256), vec(256), vec(256), vec(256)],
        out_specs=[pl.BlockSpec((1, tb_rows, 256), lambda i, j: (i, j, 0)),
                   pl.BlockSpec((1, 256, 256), lambda i, j: (i, 0, 0))],
        out_shape=[jax.ShapeDtypeStruct((b, t, 256), F32), jax.ShapeDtypeStruct((b, 256, 256), F32)],
        scratch_shapes=[pltpu.VMEM((256, 256), F32), pltpu.VMEM((1, 1024), F32)]
        + [pltpu.VMEM((tb_rows, 256), F32)] * 7,
        compiler_params=_params("parallel", "arbitrary"),
        name="rwkv7",
    )(u, prev, s0bd, *prm)


def _lru_kernel(u_ref, cb_ref, h0_ref, cw_ref, cbias_ref, wa_ref, ba_ref, wx_ref, bx_ref, lam_ref,
                out_ref, hl_ref, ext, hc, *, tb_rows, reset_first):
    tb = pl.program_id(1)

    @pl.when(tb == 0)
    def _():
        ext[0:8, :] = jnp.zeros((8, 256), F32)
        ext[5:8, :] = cb_ref[0]
        hc[...] = h0_ref[0]

    u = u_ref[0]
    xb, gate = u[:, :256], u[:, 256:]
    ext[8:8 + tb_rows, :] = xb
    cw = cw_ref[...]
    y = (cbias_ref[...] + cw[0:1] * ext[5:5 + tb_rows, :] + cw[1:2] * ext[6:6 + tb_rows, :]
         + cw[2:3] * ext[7:7 + tb_rows, :] + cw[3:4] * xb)
    ext[0:8, :] = ext[tb_rows:tb_rows + 8, :]
    gate_r = jax.nn.sigmoid(_dot3(y, wa_ref[...]) + ba_ref[...])
    gate_i = jax.nn.sigmoid(_dot3(y, wx_ref[...]) + bx_ref[...])
    lam = lam_ref[...]
    softplus_neg = jnp.maximum(-lam, 0.0) + jnp.log(1.0 + jnp.exp(-jnp.abs(lam)))
    log_a = (-LRU_C * softplus_neg) * gate_r
    a = jnp.exp(log_a)
    mult = jnp.sqrt(1.0 - jnp.exp(2.0 * log_a))
    row = lax.broadcasted_iota(I32, (tb_rows, 256), 0)
    if reset_first:
        mult = jnp.where((row == 0) & (tb == 0), 1.0, mult)
    bv = mult * gate_i * y
    d = 1
    while d < tb_rows:
        live = row >= d
        a_sh = jnp.where(live, pltpu.roll(a, d, 0), 1.0)
        b_sh = jnp.where(live, pltpu.roll(bv, d, 0), 0.0)
        bv = a * b_sh + bv
        a = a * a_sh
        d *= 2
    h = a * hc[...] + bv
    hc[...] = h[tb_rows - 1:tb_rows, :]
    gelu = 0.5 * gate * (1.0 + jnp.tanh(math.sqrt(2.0 / math.pi) * (gate + 0.044715 * gate * gate * gate)))
    out_ref[0] = h * gelu
    hl_ref[0] = h[tb_rows - 1:tb_rows, :]


def _lru(u, conv_buf, h0, prm, tb_rows, reset_first):
    b, t, _ = u.shape
    vec = _full((1, 256))
    kern = functools.partial(_lru_kernel, tb_rows=tb_rows, reset_first=reset_first)
    return pl.pallas_call(
        kern,
        grid=(b, t // tb_rows),
        in_specs=[pl.BlockSpec((1, tb_rows, 512), lambda i, j: (i, j, 0)),
                  pl.BlockSpec((1, 3, 256), lambda i, j: (i, 0, 0)),
                  pl.BlockSpec((1, 1, 256), lambda i, j: (i, 0, 0)),
                  _full((4, 256)), vec, _full((256, 256)), vec, _full((256, 256)), vec, vec],
        out_specs=[pl.BlockSpec((1, tb_rows, 256), lambda i, j: (i, j, 0)),
                   pl.BlockSpec((1, 1, 256), lambda i, j: (i, 0, 0))],
        out_shape=[jax.ShapeDtypeStruct((b, t, 256), F32), jax.ShapeDtypeStruct((b, 1, 256), F32)],
        scratch_shapes=[pltpu.VMEM((tb_rows + 8, 256), F32), pltpu.VMEM((1, 256), F32)],
        compiler_params=_params("parallel", "arbitrary"),
        name="rglru",
    )(u, conv_buf, h0, *prm)


def _sortable(x):
    bits = pltpu.bitcast(jnp.where(jnp.abs(x) < F32_MIN_NORMAL, 0.0, x), I32)
    return jnp.where(bits < 0, bits ^ 0x7FFFFFFF, bits)


def _bisect_threshold(count_gt, t0, cnt0, topk, bits=32):
    def cond(c):
        it, _, cnt = c
        return (it < bits) & (jnp.max(cnt) > topk)

    def body(c):
        it, t, cnt = c
        cand = t + lax.shift_left(jnp.int32(1), bits - 1 - it)
        cnt_cand = count_gt(cand)
        ok = cnt_cand >= topk
        return it + 1, jnp.where(ok, cand, t), jnp.where(ok, cnt_cand, cnt)

    _, t, cnt = lax.while_loop(cond, body, (jnp.int32(0), t0, cnt0))
    return t, cnt


def _dsa_prompt_steps(nq, qb_rows, kb_rows):
    steps = []
    for i in range(nq):
        nv = ((i + 1) * qb_rows + kb_rows - 1) // kb_rows
        steps += [(i, 0, j, nv, j, 0) for j in range(nv)] + [(i, 1, j, nv, nv - 1, j) for j in range(nv)]
    return list(zip(*steps))


def _dsa_prompt_kernel(tab_ref, qb_ref, qi_ref, wt_ref, ki4_ref, kb_ref, vt_ref, o_ref,
                       lhs_s, key_s, top_s, thr_s, qm_s, m_s, acc_s, *, qb_rows, kb_rows, sk, topk, n_steps):
    step = pl.program_id(0)
    i, ph, j, nv = (tab_ref[r * n_steps + step] for r in range(4))
    n_pairs = SA_HEADS // 2

    @pl.when((ph == 0) & (j == 0))
    def _init():
        qi, qb = qi_ref[...], qb_ref[...]
        lane = lax.broadcasted_iota(I32, (qb_rows, 128), 1)
        for h in range(SA_HEADS):
            hi, lo = _split_bf16(qi[:, h * 64:(h + 1) * 64])
            lhs_s[h * qb_rows:(h + 1) * qb_rows, :] = jnp.concatenate([hi, hi, lo, lo], axis=1)
        for p in range(n_pairs):
            qp = qb[:, p * 128:(p + 1) * 128]
            zero = jnp.zeros_like(qp)
            qm_s[p] = jnp.concatenate([jnp.where(lane < 64, qp, zero), jnp.where(lane < 64, zero, qp)], axis=0)
        m_s[...] = jnp.full(m_s.shape, NEG_BIG, F32)
        acc_s[...] = jnp.zeros(acc_s.shape, F32)

    @pl.when(ph == 0)
    def _index():
        wt = wt_ref[...]
        qpos = i * qb_rows + lax.broadcasted_iota(I32, (sk, qb_rows), 1)
        for sub in range(kb_rows // sk):
            z = _dot_nt(ki4_ref[sub * sk:(sub + 1) * sk, :], lhs_s[...])
            sc = wt[0:1] * jnp.maximum(z[:, 0:qb_rows], 0.0)
            for h in range(1, SA_HEADS):
                sc = sc + wt[h:h + 1] * jnp.maximum(z[:, h * qb_rows:(h + 1) * qb_rows], 0.0)
            kpos = j * kb_rows + sub * sk + lax.broadcasted_iota(I32, (sk, qb_rows), 0)
            causal = kpos <= qpos
            sc = jnp.where(jnp.abs(sc) < F32_MIN_NORMAL, 0.0, sc)
            bits = pltpu.bitcast(sc, I32)
            key_s[j, sub * sk:(sub + 1) * sk, :] = jnp.where(causal, jnp.where(bits < 0, bits ^ 0x7FFFFFFF, bits),
                                                             INT_MIN)
            top = pltpu.bitcast(bits & jnp.int32(-65536), F32)
            top_s[j, sub * sk:(sub + 1) * sk, :] = jnp.where(causal, top, -jnp.inf).astype(BF16)

    @pl.when((ph == 1) & (j == 0))
    def _threshold():
        one, zero = jnp.ones((), BF16), jnp.zeros((), BF16)

        def count_top(cand):
            cand = jnp.maximum(cand, -32513)
            cand = jnp.where((cand < 0) & (cand > -129), -129, cand)
            pat = jnp.where(cand >= 0, cand, cand ^ 0x7FFF) & 0xFFFF
            cf = pltpu.bitcast(lax.shift_left(pat, 16), F32).astype(BF16)

            def body(jb, acc):
                x = jnp.where(top_s[jb] > cf, one, zero)
                parts = [x[r * 16:(r + 1) * 16] for r in range(kb_rows // 16)]
                while len(parts) > 1:
                    parts = [parts[a] + parts[a + 1] for a in range(0, len(parts), 2)]
                return acc + parts[0].astype(F32)

            acc = lax.fori_loop(0, nv, body, jnp.zeros((16, qb_rows), F32))
            return jnp.sum(acc, axis=0, keepdims=True).astype(I32)

        def top_step(it, c):
            t, cnt = c
            cand = t + lax.shift_left(jnp.int32(1), 15 - it)
            cnt_cand = count_top(cand)
            ok = cnt_cand >= topk
            return jnp.where(ok, cand, t), jnp.where(ok, cnt_cand, cnt)

        t_top0 = jnp.full((1, qb_rows), -32768, I32)
        t_top, cnt = lax.fori_loop(0, 16, top_step, (t_top0, count_top(t_top0)))

        def count_gt(t):
            def body(jb, acc):
                x = jnp.where(key_s[jb] > t, 1, 0)
                return acc + jnp.sum(x.reshape(kb_rows // 8, 8, qb_rows), axis=0)
            acc = lax.fori_loop(0, nv, body, jnp.zeros((8, qb_rows), I32))
            return jnp.sum(acc, axis=0, keepdims=True)

        t, cnt = _bisect_threshold(count_gt, lax.shift_left(t_top, 16) + 0xFFFF, cnt, topk, bits=16)
        thr_s[...] = t
        tied = cnt > topk

        @pl.when(jnp.max(jnp.where(tied, 1, 0)) > 0)
        def _drop_late_ties():
            need = jnp.where(tied, topk - count_gt(t + 1), NO_TIE).astype(F32)
            ri = lax.broadcasted_iota(I32, (kb_rows, kb_rows), 0)
            ci = lax.broadcasted_iota(I32, (kb_rows, kb_rows), 1)
            upto = (ri >= ci).astype(BF16)

            def body(jb, seen):
                k = key_s[jb]
                eq = jnp.where(k == t + 1, 1.0, 0.0)
                rank = seen + _dot(upto, eq.astype(BF16))
                key_s[jb] = jnp.where(eq * rank > need, INT_MIN, k)
                return seen + jnp.sum(eq, axis=0, keepdims=True)

            lax.fori_loop(0, nv, body, jnp.zeros((1, qb_rows), F32))

    @pl.when(ph == 1)
    def _attend():
        t = thr_s[...]
        for rows in (slice(0, kb_rows),):
            bias = jnp.where(key_s[j, rows, :] > t, 0.0, NEG_BIG)
            bias = jnp.concatenate([bias, bias], axis=1)
            scores = [_dot_nt(kb_ref[rows, p * 128:(p + 1) * 128], qm_s[p]) for p in range(n_pairs)]
            for p in range(n_pairs):
                s = scores[p] + bias
                m_old = m_s[p]
                m_new = jnp.maximum(m_old, jnp.max(s, axis=0, keepdims=True))
                alpha = jnp.exp2(m_old - m_new)
                prb = jnp.exp2(s - m_new).astype(BF16)
                m_s[p] = m_new
                for hh in range(2):
                    h = 2 * p + hh
                    cols = slice(hh * qb_rows, (hh + 1) * qb_rows)
                    span = slice(hh * V_ROWS, (hh + 1) * V_ROWS)
                    pv = _dot(vt_ref[h * V_ROWS:(h + 1) * V_ROWS, rows], prb[:, cols])
                    acc_s[p, span, :] = alpha[:, cols] * acc_s[p, span, :] + pv

    @pl.when((ph == 1) & (j == nv - 1))
    def _finish():
        for p in range(n_pairs):
            a = acc_s[p]
            a = jnp.concatenate([a[hh * V_ROWS:hh * V_ROWS + 64] / a[hh * V_ROWS + 64:hh * V_ROWS + 65]
                                 for hh in range(2)], axis=0)
            o_ref[:, p * 128:(p + 1) * 128] = a.T


def _dsa_prompt(qb, qi, wt, ki4, kb, vt, qb_rows, kb_rows):
    s = qb.shape[0]
    nq, nk = s // qb_rows, s // kb_rows
    sk = min(256, kb_rows)
    tab = _dsa_prompt_steps(nq, qb_rows, kb_rows)
    n = len(tab[0])
    qrow = lambda w: pl.BlockSpec((qb_rows, w), lambda st, t: (t[st], 0))
    kern = functools.partial(_dsa_prompt_kernel, qb_rows=qb_rows, kb_rows=kb_rows, sk=sk, topk=min(TOPK, s // 4),
                             n_steps=n)
    grid_spec = pltpu.PrefetchScalarGridSpec(
        num_scalar_prefetch=1,
        grid=(n,),
        in_specs=[qrow(512), qrow(512),
                  pl.BlockSpec((SA_HEADS, qb_rows), lambda st, t: (0, t[st])),
                  pl.BlockSpec((kb_rows, 256), lambda st, t: (t[4 * n + st], 0)),
                  pl.BlockSpec((kb_rows, 512), lambda st, t: (t[5 * n + st], 0)),
                  pl.BlockSpec((SA_HEADS * V_ROWS, kb_rows), lambda st, t: (0, t[5 * n + st]))],
        out_specs=qrow(512),
        scratch_shapes=[pltpu.VMEM((SA_HEADS * qb_rows, 256), BF16),
                        pltpu.VMEM((nk, kb_rows, qb_rows), I32),
                        pltpu.VMEM((nk, kb_rows, qb_rows), BF16),
                        pltpu.VMEM((1, qb_rows), I32),
                        pltpu.VMEM((SA_HEADS // 2, 2 * qb_rows, 128), BF16),
                        pltpu.VMEM((SA_HEADS // 2, 1, 2 * qb_rows), F32),
                        pltpu.VMEM((SA_HEADS // 2, 2 * V_ROWS, qb_rows), F32)],
    )
    return pl.pallas_call(
        kern,
        grid_spec=grid_spec,
        out_shape=jax.ShapeDtypeStruct((s, 512), F32),
        compiler_params=_params("arbitrary"),
        name="dsa_prompt",
    )(jnp.asarray(tab, I32).reshape(-1), qb, qi, wt, ki4, kb, vt)


PAGES_PER_STEP = 16


def _dsa_sample_index_kernel(pt_ref, qi_ref, kw_ref, ki4n_ref, *rest, n_pages, t_len, topk):
    pages = rest[:PAGES_PER_STEP]
    key_ref, thr_ref, lhs_s, wcol_s = rest[PAGES_PER_STEP:]
    j = pl.program_id(1)
    n_steps = n_pages // PAGES_PER_STEP

    @pl.when(j == 0)
    def _init():
        qi, kw = qi_ref[0], kw_ref[0]
        hi, lo = _split_bf16(jnp.concatenate([qi[:, h * 64:(h + 1) * 64] for h in range(SA_HEADS)], axis=0))
        lhs_s[...] = jnp.concatenate([hi, hi, lo, lo], axis=1)
        for h in range(SA_HEADS):
            wcol_s[h] = kw[:, 64 + h:65 + h]

    def head_sum(z):
        sc = wcol_s[0] * jnp.maximum(z[0:t_len], 0.0)
        for h in range(1, SA_HEADS):
            sc = sc + wcol_s[h] * jnp.maximum(z[h * t_len:(h + 1) * t_len], 0.0)
        return sc

    hi, lo = _split_bf16(jnp.concatenate([p[...] for p in pages], axis=1))
    keys = _sortable(head_sum(_dot(lhs_s[...], jnp.concatenate([hi, lo, hi, lo], axis=0))))
    for r in range(PAGES_PER_STEP):
        key_ref[0, j * PAGES_PER_STEP + r] = keys[:, r * PAGE:(r + 1) * PAGE]

    @pl.when(j == n_steps - 1)
    def _finish():
        sc = head_sum(_dot_nt(lhs_s[...], ki4n_ref[0]))
        tq = lax.broadcasted_iota(I32, (t_len, 128), 0)
        ts = lax.broadcasted_iota(I32, (t_len, 128), 1)
        key_ref[0, n_pages] = jnp.where(ts <= tq, _sortable(sc), INT_MIN)

        def count_gt(t):
            x = jnp.where(key_ref[0] > t[None], 1, 0)
            return jnp.sum(jnp.sum(x, axis=0), axis=1, keepdims=True)

        t0 = jnp.full((t_len, 1), INT_MIN, I32)
        thr, cnt = _bisect_threshold(count_gt, t0, count_gt(t0), topk)
        thr_ref[0] = jnp.broadcast_to(thr, (t_len, 128))
        tied = cnt > topk

        @pl.when(jnp.max(jnp.where(tied, 1, 0)) > 0)
        def _drop_late_ties():
            need = jnp.where(tied, topk - count_gt(thr + 1), NO_TIE).astype(F32)
            upto = (lax.broadcasted_iota(I32, (PAGE, PAGE), 0) <= lax.broadcasted_iota(I32, (PAGE, PAGE), 1))

            def body(p, seen):
                k = key_ref[0, p]
                eq = jnp.where(k == thr + 1, 1.0, 0.0)
                rank = seen + _dot(eq.astype(BF16), upto.astype(BF16))
                key_ref[0, p] = jnp.where(eq * rank > need, INT_MIN, k)
                return seen + jnp.sum(eq, axis=1, keepdims=True)

            lax.fori_loop(0, n_pages + 1, body, jnp.zeros((t_len, 1), F32))


def _dsa_sample_index(page_table, qi, kw, ki4_new, cache_kidx_t, layer, t_len):
    b = qi.shape[0]
    n_pages = page_table.shape[1]
    n_steps = n_pages // PAGES_PER_STEP

    def page_spec(r):
        return pl.BlockSpec((None, None, 64, PAGE),
                            lambda i, j, pt: (layer, pt[i * n_pages + j * PAGES_PER_STEP + r], 0, 0))

    kern = functools.partial(_dsa_sample_index_kernel, n_pages=n_pages, t_len=t_len,
                             topk=min(TOPK, (n_pages * PAGE + t_len) // 4))
    grid_spec = pltpu.PrefetchScalarGridSpec(
        num_scalar_prefetch=1,
        grid=(b, n_steps),
        in_specs=[pl.BlockSpec((1, t_len, 512), lambda i, j, pt: (i, 0, 0)),
                  pl.BlockSpec((1, t_len, 128), lambda i, j, pt: (i, 0, 0)),
                  pl.BlockSpec((1, PAGE, 256), lambda i, j, pt: (i, 0, 0))]
        + [page_spec(r) for r in range(PAGES_PER_STEP)],
        out_specs=[pl.BlockSpec((1, n_pages + 1, t_len, 128), lambda i, j, pt: (i, 0, 0, 0)),
                   pl.BlockSpec((1, t_len, 128), lambda i, j, pt: (i, 0, 0))],
        scratch_shapes=[pltpu.VMEM((SA_HEADS * t_len, 256), BF16), pltpu.VMEM((SA_HEADS, t_len, 1), F32)],
    )
    return pl.pallas_call(
        kern,
        grid_spec=grid_spec,
        out_shape=[jax.ShapeDtypeStruct((b, n_pages + 1, t_len, 128), I32),
                   jax.ShapeDtypeStruct((b, t_len, 128), I32)],
        compiler_params=_params("parallel", "arbitrary"),
        name="dsa_sample_index",
    )(page_table.reshape(-1), qi, kw, ki4_new, *([cache_kidx_t] * PAGES_PER_STEP))


def _dsa_sample_attend_kernel(pt_ref, qb_ref, key_ref, keyn_ref, thr_ref, kn_ref, vn_ref, *rest, n_pages, t_len):
    kpages = rest[:PAGES_PER_STEP]
    vpages = rest[PAGES_PER_STEP:2 * PAGES_PER_STEP]
    o_ref, qbd_s, m_s, l_s, acc_s = rest[2 * PAGES_PER_STEP:]
    j = pl.program_id(1)
    n_steps = n_pages // PAGES_PER_STEP
    rows = SA_HEADS * t_len
    thr = thr_ref[0][:, 0:1]

    @pl.when(j == 0)
    def _init():
        qb = qb_ref[0].astype(F32)
        lane = lax.broadcasted_iota(I32, (t_len, 512), 1) // 64
        qbd = jnp.concatenate([jnp.where(lane == h, qb, 0.0) for h in range(SA_HEADS)], axis=0)
        qbd_s[...] = qbd.astype(BF16)
        m_s[...] = jnp.full(m_s.shape, NEG_BIG, F32)
        l_s[...] = jnp.zeros(l_s.shape, F32)
        acc_s[...] = jnp.zeros(acc_s.shape, F32)

    def update(keys, s, pv):
        bias = jnp.where(keys > thr, 0.0, NEG_BIG)
        s = s + jnp.concatenate([bias] * SA_HEADS, axis=0)
        m_old = m_s[...]
        m_new = jnp.maximum(m_old, jnp.max(s, axis=1, keepdims=True))
        alpha = jnp.exp2(m_old - m_new)
        pr = jnp.exp2(s - m_new)
        l_s[...] = alpha * l_s[...] + jnp.sum(pr, axis=1, keepdims=True)
        acc_s[...] = alpha * acc_s[...] + pv(pr.astype(BF16))
        m_s[...] = m_new

    kcat = jnp.concatenate([p[...].astype(BF16) for p in kpages], axis=1)
    vcat = jnp.concatenate([p[...].astype(BF16) for p in vpages], axis=1)
    keys = jnp.concatenate([key_ref[0, r] for r in range(PAGES_PER_STEP)], axis=1)
    update(keys, _dot(qbd_s[...], kcat), lambda pr: _dot_nt(pr, vcat))

    @pl.when(j == n_steps - 1)
    def _finish():
        kn, vn = kn_ref[0].astype(BF16), vn_ref[0].astype(BF16)
        update(keyn_ref[0, 0], _dot_nt(qbd_s[...], kn), lambda pr: _dot(pr, vn))
        o = acc_s[...] / l_s[...]
        lane = lax.broadcasted_iota(I32, (t_len, 512), 1) // 64
        out = jnp.zeros((t_len, 512), F32)
        for h in range(SA_HEADS):
            out = out + jnp.where(lane == h, o[h * t_len:(h + 1) * t_len], 0.0)
        o_ref[0] = out


def _dsa_sample_attend(page_table, qb, keys, thr, k_new, v_new, cache_k_t, cache_v_t, layer, t_len):
    b = qb.shape[0]
    n_pages = page_table.shape[1]
    n_steps = n_pages // PAGES_PER_STEP
    rows = SA_HEADS * t_len

    def page_spec(r):
        return pl.BlockSpec((None, None, 512, PAGE),
                            lambda i, j, pt: (layer, pt[i * n_pages + j * PAGES_PER_STEP + r], 0, 0))

    per_b = lambda n: pl.BlockSpec((1, t_len, n), lambda i, j, pt: (i, 0, 0))
    kern = functools.partial(_dsa_sample_attend_kernel, n_pages=n_pages, t_len=t_len)
    grid_spec = pltpu.PrefetchScalarGridSpec(
        num_scalar_prefetch=1,
        grid=(b, n_steps),
        in_specs=[per_b(512),
                  pl.BlockSpec((1, PAGES_PER_STEP, t_len, 128), lambda i, j, pt: (i, j, 0, 0)),
                  pl.BlockSpec((1, 1, t_len, 128), lambda i, j, pt: (i, n_pages, 0, 0)),
                  per_b(128),
                  pl.BlockSpec((1, PAGE, 512), lambda i, j, pt: (i, 0, 0)),
                  pl.BlockSpec((1, PAGE, 512), lambda i, j, pt: (i, 0, 0))]
        + [page_spec(r) for r in range(PAGES_PER_STEP)] * 2,
        out_specs=per_b(512),
        scratch_shapes=[pltpu.VMEM((rows, 512), BF16), pltpu.VMEM((rows, 1), F32), pltpu.VMEM((rows, 1), F32),
                        pltpu.VMEM((rows, 512), F32)],
    )
    return pl.pallas_call(
        kern,
        grid_spec=grid_spec,
        out_shape=jax.ShapeDtypeStruct((b, t_len, 512), F32),
        compiler_params=_params("parallel", "arbitrary"),
        name="dsa_sample_attend",
    )(page_table.reshape(-1), qb, keys, keys, thr, k_new, v_new,
      *([cache_k_t] * PAGES_PER_STEP), *([cache_v_t] * PAGES_PER_STEP))


def _matmul_kernel(x_ref, w_ref, o_ref):
    o_ref[...] = _dot(x_ref[...].astype(BF16), w_ref[...])


def _matmul(x, w):
    m, n = x.shape[0], w.shape[1]
    return pl.pallas_call(
        _matmul_kernel,
        grid=(1,),
        in_specs=[_full(x.shape), _full(w.shape)],
        out_specs=_full((m, n)),
        out_shape=jax.ShapeDtypeStruct((m, n), F32),
        compiler_params=_params("arbitrary"),
        name="mem_kv_proj",
    )(x, w)


def _post1_kernel(x_ref, yrw_ref, ysa_ref, ylru_ref, wo1_ref, wo2_ref, wo3_ref, g_ref, wq_ref, x1_ref, qm_ref):
    x1 = (x_ref[...] + _dot(yrw_ref[...].astype(BF16), wo1_ref[...])
          + _dot(ysa_ref[...].astype(BF16), wo2_ref[...]) + _dot(ylru_ref[...].astype(BF16), wo3_ref[...]))
    x1_ref[...] = x1
    qm = _dot(_rms(x1, g_ref[...]).astype(BF16), wq_ref[...])
    qm_ref[...] = (qm * (MEM_HEAD_DIM ** -0.5)).astype(BF16)


def _post1(x, yrw, ysa, ylru, wo1, wo2, wo3, g, wq, tm):
    m = x.shape[0]
    row = lambda n: pl.BlockSpec((tm, n), lambda i: (i, 0))
    return pl.pallas_call(
        _post1_kernel,
        grid=(m // tm,),
        in_specs=[row(1024), row(256), row(512), row(256), _full(wo1.shape), _full(wo2.shape), _full(wo3.shape),
                  _full((1, 1024)), _full(wq.shape)],
        out_specs=[row(1024), row(1024)],
        out_shape=[jax.ShapeDtypeStruct((m, 1024), F32), jax.ShapeDtypeStruct((m, 1024), BF16)],
        compiler_params=_params("parallel"),
        name="out_proj_memq",
    )(x, yrw, ysa, ylru, wo1, wo2, wo3, g, wq)


def _mem_attn_kernel(q_ref, mk_ref, mv_ref, o_ref):
    q = q_ref[0]
    for h in range(MEM_HEADS):
        sl = slice(h * MEM_HEAD_DIM, (h + 1) * MEM_HEAD_DIM)
        s = _dot_nt(q[:, sl], mk_ref[0, :, h, :].astype(BF16))
        p = jnp.exp(s - jnp.max(s, axis=1, keepdims=True))
        o = _dot(p.astype(BF16), mv_ref[0, :, h, :].astype(BF16)) / jnp.sum(p, axis=1, keepdims=True)
        o_ref[0, :, sl] = o.astype(BF16)


def _mem_attn(q, mk, mv, layer, tm):
    b, t, _ = q.shape
    slab = pl.BlockSpec((None, 1, 256, MEM_HEADS, MEM_HEAD_DIM), lambda i, j: (layer, i, 0, 0, 0))
    return pl.pallas_call(
        _mem_attn_kernel,
        grid=(b, t // tm),
        in_specs=[pl.BlockSpec((1, tm, 1024), lambda i, j: (i, j, 0)), slab, slab],
        out_specs=pl.BlockSpec((1, tm, 1024), lambda i, j: (i, j, 0)),
        out_shape=jax.ShapeDtypeStruct((b, t, 1024), BF16),
        compiler_params=_params("parallel", "parallel"),
        name="mem_attn",
    )(q, mk, mv)


def _post2_kernel(x1_ref, o_ref, wo_ref, g_ref, wg_ref, wu_ref, wd_ref, gf_ref, out_ref, *, last):
    x2 = x1_ref[...] + _dot(o_ref[...], wo_ref[...])
    hb = _rms(x2, g_ref[...]).astype(BF16)
    gt = _dot(hb, wg_ref[...])
    up = _dot(hb, wu_ref[...])
    act = (gt * jax.nn.sigmoid(gt) * up).astype(BF16)
    x3 = x2 + _dot(act, wd_ref[...])
    out_ref[...] = _rms(x3, gf_ref[...]) if last else x3


def _post2(x1, o, wo, g, wg, wu, wd, gf, tm, last):
    m = x1.shape[0]
    row = lambda n: pl.BlockSpec((tm, n), lambda i: (i, 0))
    once = lambda w: pl.BlockSpec(w.shape, lambda i: (0, 0), pipeline_mode=pl.Buffered(1))
    return pl.pallas_call(
        functools.partial(_post2_kernel, last=last),
        grid=(m // tm,),
        in_specs=[row(1024), row(1024), once(wo), _full((1, 1024)), once(wg), once(wu), once(wd), _full((1, 1024))],
        out_specs=row(1024),
        out_shape=jax.ShapeDtypeStruct((m, 1024), F32),
        compiler_params=_params("parallel"),
        name="memo_swiglu",
    )(x1, o, wo, g, wg, wu, wd, gf)


def _blockdiag4(w):
    n = w.shape[-1]
    return jnp.einsum("gij,gh->gihj", w, jnp.eye(4, dtype=w.dtype)).reshape(4 * n, 4 * n)


def _state_to_blockdiag(s):
    b = s.shape[0]
    return jnp.einsum("bhvk,hg->bhvgk", s, jnp.eye(4, dtype=s.dtype)).reshape(b, 256, 256)


def _blockdiag_to_state(sbd):
    b = sbd.shape[0]
    return jnp.einsum("bhvgk,hg->bhvk", sbd.reshape(b, 4, 64, 4, 64), jnp.eye(4, dtype=sbd.dtype))


def _layer_weights(l, w_in, w_out, rw, lru, g_mix, g_mem, w_mq, w_mk, w_mv, w_mo, g_ffn, w_gate, w_up, w_down):
    wi = w_in[l]
    o = RW_COLS
    cut = lambda a, n: wi[:, a:a + n].astype(BF16)
    wkw = jnp.pad(wi[:, o + 2048:o + 2120], ((0, 0), (0, 56))).astype(BF16)
    in_ws = (cut(0, 1024), cut(o, 512), cut(o + 512, 512), cut(o + 1024, 512), cut(o + 1536, 512), wkw,
             cut(o + 2120, 512))
    (rw_mu, rw_w0, rw_w2, rw_a0, rw_a2, rw_g2, rw_kk, rw_ka, rw_rk, rw_ln_w, rw_ln_b) = rw
    z = jnp.zeros((64, 256), F32)
    v256 = lambda a: a[l].reshape(1, 256)
    rw_prm = (rw_mu[l].reshape(1, 1024), v256(rw_w0), jnp.concatenate([rw_w2[l], z], axis=0), v256(rw_a0),
              jnp.concatenate([z, rw_a2[l]], axis=0), rw_g2[l], v256(rw_kk), v256(rw_ka), v256(rw_rk),
              v256(rw_ln_w), v256(rw_ln_b))
    (lru_conv_w, lru_conv_b, lru_wa, lru_ba, lru_wx, lru_bx, lru_lambda) = lru
    lru_prm = (lru_conv_w[l], v256(lru_conv_b), _blockdiag4(lru_wa[l]), v256(lru_ba), _blockdiag4(lru_wx[l]),
               v256(lru_bx), v256(lru_lambda))
    wo = w_out[l].astype(BF16)
    return dict(
        in_ws=in_ws, rw=rw_prm, lru=lru_prm, g_mix=g_mix[l].reshape(1, 1024),
        wo=(wo[:256], wo[256:768], wo[768:]), g_mem=g_mem[l].reshape(1, 1024), wq=w_mq[l].astype(BF16),
        wmk=w_mk[l].astype(BF16), wmv=w_mv[l].astype(BF16), wmo=w_mo[l].astype(BF16),
        g_ffn=g_ffn[l].reshape(1, 1024), wg=w_gate[l].astype(BF16), wu=w_up[l].astype(BF16),
        wd=w_down[l].astype(BF16))


def _pick_tile(n, pref):
    t = min(n, pref)
    while n % t:
        t //= 2
    return t


def _prompt_layer(x, mem, w, tabs, g_final, last):
    s = x.shape[0]
    tm = _pick_tile(s, 256)
    urw, kt, vt, qi, _, kwt, ulru, qb, kb, vtb, ki4 = _in_proj(x, w["g_mix"], w["in_ws"], tabs, tm)
    tb = _pick_tile(s, 512)
    y_rw, sbd = _rwkv(urw[None], jnp.zeros((1, 1, 1024), F32), jnp.zeros((1, 256, 256), F32), w["rw"], tb, tb)
    y_lru, h_last = _lru(ulru[None], jnp.zeros((1, 3, 256), F32), jnp.zeros((1, 1, 256), F32), w["lru"], tb, True)
    y_sa = _dsa_prompt(qb, qi, kwt[64:64 + SA_HEADS], ki4, kb, vtb, _pick_tile(s, 256), _pick_tile(s, 1024))
    mk = _matmul(mem, w["wmk"])
    mv = _matmul(mem, w["wmv"])
    x1, qm = _post1(x, y_rw[0], y_sa, y_lru[0], *w["wo"], w["g_mem"], w["wq"], tm)
    o = _mem_attn(qm[None], mk.reshape(1, 1, 256, 4, 256), mv.reshape(1, 1, 256, 4, 256), 0, tm)
    x3 = _post2(x1, o[0], w["wmo"], w["g_ffn"], w["wg"], w["wu"], w["wd"], g_final, tm, last)
    rows_major = lambda a: jnp.transpose(a.reshape(SA_HEADS, HEAD_DIM, s), (2, 0, 1))[None]
    new = dict(k=rows_major(kt), v=rows_major(vt), kidx=kwt[:64].T[None],
               mk=mk.reshape(1, 256, 4, 256), mv=mv.reshape(1, 256, 4, 256), rwkv=_blockdiag_to_state(sbd),
               shift=urw[None, s - 1], h=h_last[:, 0], conv=ulru[None, s - 3:, :256])
    return x3, new


def _sample_layer(x, l, w, tabs, g_final, cache_k, cache_v, cache_kidx, cache_mem_k, cache_mem_v, state_rwkv,
                  state_rwkv_shift, state_lru_h, state_lru_conv, page_table, db, t, last):
    m = db * t
    urw, kt, vt, qi, kw, _, ulru, qb, _, _, ki4 = _in_proj(x, w["g_mix"], w["in_ws"], tabs, m)
    k, v = kt.T, vt.T
    urw3 = urw.reshape(db, t, 1024)
    u_pad = jnp.pad(urw3, ((0, 0), (0, RW_CHUNK - t), (0, 0)))
    y_rw, sbd = _rwkv(u_pad, state_rwkv_shift[l][:, None], _state_to_blockdiag(state_rwkv[l]), w["rw"], RW_CHUNK, t)
    ulru3 = ulru.reshape(db, t, 512)
    y_lru, h_last = _lru(ulru3, state_lru_conv[l], state_lru_h[l][:, None], w["lru"], t, False)
    pad_new = lambda a: jnp.pad(a.reshape(db, t, -1), ((0, 0), (0, PAGE - t), (0, 0)))
    keys, thr = _dsa_sample_index(page_table, qi.reshape(db, t, 512), kw.reshape(db, t, 128), pad_new(ki4),
                                  cache_kidx, l, t)
    y_sa = _dsa_sample_attend(page_table, qb.reshape(db, t, 512), keys, thr, pad_new(k), pad_new(v),
                              cache_k, cache_v, l, t)
    x1, qm = _post1(x, y_rw[:, :t].reshape(m, 256), y_sa.reshape(m, 512), y_lru.reshape(m, 256), *w["wo"],
                    w["g_mem"], w["wq"], m)
    o = _mem_attn(qm.reshape(db, t, 1024), cache_mem_k, cache_mem_v, l, t)
    x3 = _post2(x1, o.reshape(m, 1024), w["wmo"], w["g_ffn"], w["wg"], w["wu"], w["wd"], g_final, m, last)
    conv = jnp.concatenate([state_lru_conv[l], ulru3[:, :, :256]], axis=1)[:, t:]
    new = dict(k=k.reshape(db, t, 8, 64), v=v.reshape(db, t, 8, 64), kidx=kw.reshape(db, t, 128)[:, :, :64],
               rwkv=_blockdiag_to_state(sbd), shift=urw3[:, t - 1], h=h_last[:, 0], conv=conv)
    return x3, new


def kernel(x_prompt, x_sample, mem_prompt, cache_k, cache_v, cache_kidx, cache_mem_k, cache_mem_v, state_rwkv, state_rwkv_shift, state_lru_h, state_lru_conv, page_table, g_mix, w_in, w_out, rw_mu, rw_w0, rw_w2, rw_a0, rw_a2, rw_g2, rw_kk, rw_ka, rw_rk, rw_ln_w, rw_ln_b, lru_conv_w, lru_conv_b, lru_wa, lru_ba, lru_wx, lru_bx, lru_lambda, g_mem, w_mq, w_mk, w_mv, w_mo, g_ffn, w_gate, w_up, w_down, g_final):
    depth = w_in.shape[0]
    _, s, _ = x_prompt.shape
    db, t, _ = x_sample.shape
    past_len = page_table.shape[1] * PAGE
    rw = (rw_mu, rw_w0, rw_w2, rw_a0, rw_a2, rw_g2, rw_kk, rw_ka, rw_rk, rw_ln_w, rw_ln_b)
    lru = (lru_conv_w, lru_conv_b, lru_wa, lru_ba, lru_wx, lru_bx, lru_lambda)
    tabs_p = _rope_tables(jnp.arange(s))
    tabs_s = _rope_tables(jnp.tile(past_len + jnp.arange(t), db))
    gf = g_final.reshape(1, 1024)
    n_pool = cache_k.shape[1]
    cache_kidx = jnp.transpose(cache_kidx, (0, 1, 3, 2))
    cache_k = jnp.transpose(cache_k, (0, 1, 3, 4, 2)).reshape(depth, n_pool, SA_WIDTH, PAGE)
    cache_v = jnp.transpose(cache_v, (0, 1, 3, 4, 2)).reshape(depth, n_pool, SA_WIDTH, PAGE)
    xp, xs = x_prompt[0], x_sample.reshape(db * t, 1024)
    mem = mem_prompt[0]
    news_p, news_s = [], []
    for l in range(depth):
        w = _layer_weights(l, w_in, w_out, rw, lru, g_mix, g_mem, w_mq, w_mk, w_mv, w_mo, g_ffn, w_gate, w_up,
                           w_down)
        last = l == depth - 1
        xp, new_p = _prompt_layer(xp, mem, w, tabs_p, gf, last)
        xs, new_s = _sample_layer(xs, l, w, tabs_s, gf, cache_k, cache_v, cache_kidx, cache_mem_k, cache_mem_v,
                                  state_rwkv, state_rwkv_shift, state_lru_h, state_lru_conv, page_table, db, t, last)
        news_p.append(new_p)
        news_s.append(new_s)
    stk = lambda news, name: jnp.stack([n[name] for n in news])
    return (xp[None], xs.reshape(db, t, 1024),
            stk(news_p, "k"), stk(news_p, "v"), stk(news_p, "kidx"), stk(news_p, "mk"), stk(news_p, "mv"),
            stk(news_p, "rwkv"), stk(news_p, "shift"), stk(news_p, "h"), stk(news_p, "conv"),
            stk(news_s, "k"), stk(news_s, "v"), stk(news_s, "kidx"), stk(news_s, "rwkv"), stk(news_s, "shift"),
            stk(news_s, "h"), stk(news_s, "conv"))
```

```python
import functools
import math

import jax
import jax.numpy as jnp
from jax import lax
from jax.experimental import pallas as pl
from jax.experimental.pallas import tpu as pltpu

F32 = jnp.float32
BF16 = jnp.bfloat16
I32 = jnp.int32

D_MODEL = 1024
HEAD_DIM = 64
RW_HEADS = 4
RW_WIDTH = 256
RW_COLS = 1024
RW_GN_EPS = 64e-5
SA_HEADS = 8
SA_WIDTH = 512
IDX_SCALE = 512.0 ** -0.5
TOPK = 256
LRU_WIDTH = 256
LRU_C = 8.0
D_FF = 2816
MEM_HEADS = 4
MEM_HEAD_DIM = 256
ROPE_THETA = 10000.0
NORM_EPS = 1e-6
PAGE = 128
LOG2E = 1.4426950408889634
INT_MIN = -2 ** 31
F32_MIN_NORMAL = 2.0 ** -126
NO_TIE = 2 ** 30
V_ROWS = HEAD_DIM + 16
NEG_BIG = -1e30
VMEM_LIMIT = 56 * 1024 * 1024
RW_CHUNK = 64


def _params(*sem):
    return pltpu.CompilerParams(dimension_semantics=sem, vmem_limit_bytes=VMEM_LIMIT)


def _full(shape):
    n = len(shape)
    return pl.BlockSpec(shape, lambda *_: (0,) * n)


def _dot(a, b):
    return jnp.dot(a, b, preferred_element_type=F32)


def _dot_nt(a, b):
    return lax.dot_general(a, b, (((1,), (1,)), ((), ())), preferred_element_type=F32)


def _rms(x, g):
    return x * lax.rsqrt(jnp.mean(x * x, axis=-1, keepdims=True) + NORM_EPS) * g


def _split_bf16(x):
    hi = x.astype(BF16)
    lo = (x - hi.astype(F32)).astype(BF16)
    return hi, lo


_NN = (((1,), (0,)), ((), ()))
_NT = (((1,), (1,)), ((), ()))
_TN = (((0,), (0,)), ((), ()))


def _dot1(a, b, dims=_NN):
    return lax.dot_general(a.astype(BF16), b.astype(BF16), dims, preferred_element_type=F32)


def _dot3(a, b, dims=_NN):
    ah, al = _split_bf16(a)
    bh, bl = _split_bf16(b)
    d = lambda x, y: lax.dot_general(x, y, dims, preferred_element_type=F32)
    return d(ah, bh) + (d(ah, bl) + d(al, bh))


def _dot_exact_rhs(a, m):
    hi = a.astype(BF16)
    r = a - hi.astype(F32)
    mid = r.astype(BF16)
    lo = (r - mid.astype(F32)).astype(BF16)
    return _dot(hi, m) + (_dot(mid, m) + _dot(lo, m))


def _dot_exact_lhs(m, b):
    hi = b.astype(BF16)
    r = b - hi.astype(F32)
    mid = r.astype(BF16)
    lo = (r - mid.astype(F32)).astype(BF16)
    return _dot(m, hi) + (_dot(m, mid) + _dot(m, lo))


def _rope(u, cos, sin):
    w = u.shape[-1]
    lane = lax.broadcasted_iota(I32, u.shape, 1)
    rot = jnp.where((lane & 63) < 32, pltpu.roll(u, w - 32, 1), pltpu.roll(u, 32, 1))
    return u * cos + rot * sin


def _in_proj_kernel(x_ref, g_ref, wrw_ref, wq_ref, wk_ref, wv_ref, wqi_ref, wkw_ref, wlru_ref,
                    cos_ref, sin_ref, ckw_ref, skw_ref,
                    urw_ref, kt_ref, vt_ref, qi_ref, kw_ref, kwt_ref, ulru_ref, qb_ref, kb_ref, vtb_ref, ki4_ref):
    hb = _rms(x_ref[...], g_ref[...]).astype(BF16)
    urw_ref[...] = _dot(hb, wrw_ref[...])
    ulru_ref[...] = _dot(hb, wlru_ref[...])
    cos = jnp.concatenate([cos_ref[...]] * 4, axis=1)
    sin = jnp.concatenate([sin_ref[...]] * 4, axis=1)
    q = _rope(_dot(hb, wq_ref[...]), cos, sin)
    k = _rope(_dot(hb, wk_ref[...]), cos, sin)
    v = _dot(hb, wv_ref[...])
    qi_ref[...] = _rope(_dot(hb, wqi_ref[...]), cos, sin)
    kw = _rope(_dot(hb, wkw_ref[...]), ckw_ref[...], skw_ref[...])
    vt = v.T
    kt_ref[...] = k.T
    vt_ref[...] = vt
    kw_ref[...] = kw
    kwt_ref[...] = kw.T
    qb_ref[...] = (q * (HEAD_DIM ** -0.5 * LOG2E)).astype(BF16)
    kb_ref[...] = k.astype(BF16)
    ones = jnp.ones((V_ROWS - HEAD_DIM, vt.shape[1]), F32)
    vtb_ref[...] = jnp.concatenate(
        [blk for h in range(SA_HEADS) for blk in (vt[h * HEAD_DIM:(h + 1) * HEAD_DIM], ones)], axis=0).astype(BF16)
    hi, lo = _split_bf16(kw[:, :64])
    ki4_ref[...] = jnp.concatenate([hi, lo, hi, lo], axis=1)


def _in_proj(x, g, ws, tabs, tm):
    m = x.shape[0]
    wrw, wq, wk, wv, wqi, wkw, wlru = ws
    row = lambda n: pl.BlockSpec((tm, n), lambda i: (i, 0))
    col = lambda n: pl.BlockSpec((n, tm), lambda i: (0, i))
    sds = jax.ShapeDtypeStruct
    out_shape = [sds((m, 1024), F32), sds((512, m), F32), sds((512, m), F32), sds((m, 512), F32),
                 sds((m, 128), F32), sds((128, m), F32), sds((m, 512), F32), sds((m, 512), BF16),
                 sds((m, 512), BF16), sds((SA_HEADS * V_ROWS, m), BF16), sds((m, 256), BF16)]
    return pl.pallas_call(
        _in_proj_kernel,
        grid=(m // tm,),
        in_specs=[row(1024), _full((1, 1024))] + [_full(w.shape) for w in ws] + [row(128)] * 4,
        out_specs=[row(1024), col(512), col(512), row(512), row(128), col(128), row(512), row(512), row(512),
                   col(SA_HEADS * V_ROWS), row(256)],
        out_shape=out_shape,
        compiler_params=_params("parallel"),
        name="in_proj",
    )(x, g, wrw, wq, wk, wv, wqi, wkw, wlru, *tabs)


def _rope_tables(pos):
    half = HEAD_DIM // 2
    inv = ROPE_THETA ** (-jnp.arange(half, dtype=F32) / half)
    ang = pos.astype(F32)[:, None] * inv[None, :]
    c, s = jnp.cos(ang), jnp.sin(ang)
    m = pos.shape[0]
    cos = jnp.concatenate([c, c, c, c], axis=1)
    sin = jnp.concatenate([-s, s, -s, s], axis=1)
    ckw = jnp.concatenate([c, c, jnp.full((m, SA_HEADS), IDX_SCALE, F32), jnp.zeros((m, 56), F32)], axis=1)
    skw = jnp.concatenate([-s, s, jnp.zeros((m, 64), F32)], axis=1)
    return cos, sin, ckw, skw


def _rwkv_kernel(u_ref, prev_ref, s0_ref, mu_ref, w0_ref, w2_ref, a0_ref, a2_ref, g2_ref, kk_ref, ka_ref,
                 rk_ref, lnw_ref, lnb_ref, y_ref, sout_ref,
                 sbd, carry, r_s, kn_s, a_s, k2_s, v_s, lw_s, y_s, *, tb_rows, n_valid):
    c_len = RW_CHUNK
    tb = pl.program_id(1)

    @pl.when(tb == 0)
    def _():
        sbd[...] = s0_ref[0]
        carry[...] = prev_ref[0]

    u = u_ref[0]
    row = lax.broadcasted_iota(I32, u.shape, 0)
    u_prev = jnp.where(row == 0, carry[...], pltpu.roll(u, 1, 0))
    last = min(n_valid, tb_rows) - 1
    carry[...] = u[last:last + 1, :]
    us = u + mu_ref[...] * (u_prev - u)
    r, k, v = us[:, 0:256], us[:, 256:512], us[:, 512:768]
    wa, gd = us[:, 768:896], us[:, 896:1024]

    li = lax.broadcasted_iota(I32, (256, 256), 0) // 64
    lj = lax.broadcasted_iota(I32, (256, 256), 1) // 64
    blockdiag = (li == lj).astype(F32)
    head_sum = functools.partial(_dot_exact_rhs, m=blockdiag.astype(BF16))

    xw = w0_ref[...] + _dot3(jnp.tanh(wa), w2_ref[...])
    logw = -math.exp(-0.5) * jax.nn.sigmoid(xw)
    a = jax.nn.sigmoid(a0_ref[...] + _dot3(wa, a2_ref[...]))
    g = _dot3(jax.nn.sigmoid(gd), g2_ref[...])
    kkv = k * kk_ref[...]
    kn = kkv / jnp.maximum(jnp.sqrt(head_sum(kkv * kkv)), 1e-12)
    k2 = k * (1.0 + (a - 1.0) * ka_ref[...])
    bonus = head_sum(r * k2 * rk_ref[...]) * v
    if n_valid < tb_rows:
        ok = lax.broadcasted_iota(I32, (tb_rows, 256), 0) < n_valid
        logw = jnp.where(ok, logw, 0.0)
        kn = jnp.where(ok, kn, 0.0)
        k2 = jnp.where(ok, k2, 0.0)
        v = jnp.where(ok, v, 0.0)
    r_s[...] = r
    kn_s[...] = kn
    a_s[...] = a
    k2_s[...] = k2
    v_s[...] = v
    lw_s[...] = logw

    lane = lax.broadcasted_iota(I32, (1, 256), 1) // 64
    hmask = [(lane == h).astype(F32) for h in range(RW_HEADS)]
    ci = lax.broadcasted_iota(I32, (c_len, c_len), 0)
    cj = lax.broadcasted_iota(I32, (c_len, c_len), 1)
    tri_incl = (ci >= cj).astype(BF16)
    eye = (ci == cj).astype(F32)
    mi = lax.broadcasted_iota(I32, (8 * c_len, 2 * c_len), 0)
    mj = lax.broadcasted_iota(I32, (8 * c_len, 2 * c_len), 1)
    mt, ms = mi & (c_len - 1), mj & (c_len - 1)
    keep = ms < mt + jnp.where(mi < 4 * c_len, 0, 1)
    pick = lambda x: sum(x[h * c_len:(h + 1) * c_len] * hmask[h] for h in range(RW_HEADS))
    n_sq = int(math.log2(c_len)) - 1

    def prepare(c):
        sl = pl.ds(pl.multiple_of(c * c_len, c_len), c_len)
        lw, rc, knc, ac, k2c, vc = lw_s[sl, :], r_s[sl, :], kn_s[sl, :], a_s[sl, :], k2_s[sl, :], v_s[sl, :]
        lcum = _dot_exact_lhs(tri_incl, lw)
        gam, gex, gin = jnp.exp(lcum), jnp.exp(lcum - lw), jnp.exp(-lcum)
        rt, at, bt, kt = rc * gam, knc * gex, -(knc * ac) * gin, k2c * gin
        g_end = gam[c_len - 1:c_len, :]
        lhs = jnp.concatenate([at * hm for hm in hmask] + [rt * hm for hm in hmask], axis=0)
        m = jnp.where(keep, _dot1(lhs, jnp.concatenate([bt, kt], axis=0), _NT), 0.0)
        w1 = pick(_dot1(m[:4 * c_len], jnp.concatenate([jnp.zeros_like(vc), vc], axis=0)))
        return dict(sl=sl, vc=vc, m=m, w1=w1, ar=jnp.concatenate([at, rt], axis=0), g_end=g_end,
                    bk=jnp.concatenate([bt * g_end, kt * g_end], axis=0))

    def invert(ms):
        qs = [m[h * c_len:(h + 1) * c_len, 0:c_len] for m in ms for h in range(RW_HEADS)]
        xs = [eye + q for q in qs]
        for _ in range(n_sq):
            qs = [_dot1(q, q) for q in qs]
            xs = [x + _dot1(x, q) for x, q in zip(xs, qs)]
        return [jnp.concatenate(xs[n * RW_HEADS:(n + 1) * RW_HEADS], axis=0) for n in range(len(ms))]

    def advance(d, inv):
        s = sbd[...]
        asrs = _dot1(d["ar"], s, _NT)
        p = pick(_dot1(inv, asrs[:c_len] + d["w1"]))
        pv = jnp.concatenate([p, d["vc"]], axis=0)
        y_s[d["sl"], :] = asrs[c_len:] + pick(_dot1(d["m"][4 * c_len:], pv))
        sbd[...] = s * d["g_end"] + _dot1(pv, d["bk"], _TN) * blockdiag

    n_chunks = tb_rows // c_len
    group = 4 if n_chunks % 4 == 0 else 1

    def chunk_group(gi, _):
        ds = [prepare(gi * group + n) for n in range(group)]
        for d, inv in zip(ds, invert([d["m"] for d in ds])):
            advance(d, inv)
        return 0

    lax.fori_loop(0, n_chunks // group, chunk_group, 0)

    y = y_s[...]
    mean = head_sum(y) * (1.0 / HEAD_DIM)
    yc = y - mean
    var = head_sum(yc * yc) * (1.0 / HEAD_DIM)
    yn = yc * lax.rsqrt(var + RW_GN_EPS) * lnw_ref[...] + lnb_ref[...]
    y_ref[0] = (yn + bonus) * g
    sout_ref[0] = sbd[...]


def _rwkv(u, prev, s0bd, prm, tb_rows, n_valid):
    b, t, _ = u.shape
    vec = lambda n: _full((1, n))
    kern = functools.partial(_rwkv_kernel, tb_rows=tb_rows, n_valid=n_valid)
    return pl.pallas_call(
        kern,
        grid=(b, t // tb_rows),
        in_specs=[pl.BlockSpec((1, tb_rows, 1024), lambda i, j: (i, j, 0)),
                  pl.BlockSpec((1, 1, 1024), lambda i, j: (i, 0, 0)),
                  pl.BlockSpec((1, 256, 256), lambda i, j: (i, 0, 0)),
                  vec(1024), vec(256), _full((128, 256)), vec(256), _full((128, 256)), _full((128, 256)),
                  vec(256), vec(256), vec(256), vec(256), vec(256)],
        out_specs=[pl.BlockSpec((1, tb_rows, 256), lambda i, j: (i, j, 0)),
                   pl.BlockSpec((1, 256, 256), lambda i, j: (i, 0, 0))],
        out_shape=[jax.ShapeDtypeStruct((b, t, 256), F32), jax.ShapeDtypeStruct((b, 256, 256), F32)],
        scratch_shapes=[pltpu.VMEM((256, 256), F32), pltpu.VMEM((1, 1024), F32)]
        + [pltpu.VMEM((tb_rows, 256), F32)] * 7,
        compiler_params=_params("parallel", "arbitrary"),
        name="rwkv7",
    )(u, prev, s0bd, *prm)


def _lru_kernel(u_ref, cb_ref, h0_ref, cw_ref, cbias_ref, wa_ref, ba_ref, wx_ref, bx_ref, lam_ref,
                out_ref, hl_ref, ext, hc, *, tb_rows, reset_first):
    tb = pl.program_id(1)

    @pl.when(tb == 0)
    def _():
        ext[0:8, :] = jnp.zeros((8, 256), F32)
        ext[5:8, :] = cb_ref[0]
        hc[...] = h0_ref[0]

    u = u_ref[0]
    xb, gate = u[:, :256], u[:, 256:]
    ext[8:8 + tb_rows, :] = xb
    cw = cw_ref[...]
    y = (cbias_ref[...] + cw[0:1] * ext[5:5 + tb_rows, :] + cw[1:2] * ext[6:6 + tb_rows, :]
         + cw[2:3] * ext[7:7 + tb_rows, :] + cw[3:4] * xb)
    ext[0:8, :] = ext[tb_rows:tb_rows + 8, :]
    gate_r = jax.nn.sigmoid(_dot3(y, wa_ref[...]) + ba_ref[...])
    gate_i = jax.nn.sigmoid(_dot3(y, wx_ref[...]) + bx_ref[...])
    lam = lam_ref[...]
    softplus_neg = jnp.maximum(-lam, 0.0) + jnp.log(1.0 + jnp.exp(-jnp.abs(lam)))
    log_a = (-LRU_C * softplus_neg) * gate_r
    a = jnp.exp(log_a)
    mult = jnp.sqrt(1.0 - jnp.exp(2.0 * log_a))
    row = lax.broadcasted_iota(I32, (tb_rows, 256), 0)
    if reset_first:
        mult = jnp.where((row == 0) & (tb == 0), 1.0, mult)
    bv = mult * gate_i * y
    d = 1
    while d < tb_rows:
        live = row >= d
        a_sh = jnp.where(live, pltpu.roll(a, d, 0), 1.0)
        b_sh = jnp.where(live, pltpu.roll(bv, d, 0), 0.0)
        bv = a * b_sh + bv
        a = a * a_sh
        d *= 2
    h = a * hc[...] + bv
    hc[...] = h[tb_rows - 1:tb_rows, :]
    gelu = 0.5 * gate * (1.0 + jnp.tanh(math.sqrt(2.0 / math.pi) * (gate + 0.044715 * gate * gate * gate)))
    out_ref[0] = h * gelu
    hl_ref[0] = h[tb_rows - 1:tb_rows, :]


def _lru(u, conv_buf, h0, prm, tb_rows, reset_first):
    b, t, _ = u.shape
    vec = _full((1, 256))
    kern = functools.partial(_lru_kernel, tb_rows=tb_rows, reset_first=reset_first)
    return pl.pallas_call(
        kern,
        grid=(b, t // tb_rows),
        in_specs=[pl.BlockSpec((1, tb_rows, 512), lambda i, j: (i, j, 0)),
                  pl.BlockSpec((1, 3, 256), lambda i, j: (i, 0, 0)),
                  pl.BlockSpec((1, 1, 256), lambda i, j: (i, 0, 0)),
                  _full((4, 256)), vec, _full((256, 256)), vec, _full((256, 256)), vec, vec],
        out_specs=[pl.BlockSpec((1, tb_rows, 256), lambda i, j: (i, j, 0)),
                   pl.BlockSpec((1, 1, 256), lambda i, j: (i, 0, 0))],
        out_shape=[jax.ShapeDtypeStruct((b, t, 256), F32), jax.ShapeDtypeStruct((b, 1, 256), F32)],
        scratch_shapes=[pltpu.VMEM((tb_rows + 8, 256), F32), pltpu.VMEM((1, 256), F32)],
        compiler_params=_params("parallel", "arbitrary"),
        name="rglru",
    )(u, conv_buf, h0, *prm)


def _sortable(x):
    bits = pltpu.bitcast(jnp.where(jnp.abs(x) < F32_MIN_NORMAL, 0.0, x), I32)
    return jnp.where(bits < 0, bits ^ 0x7FFFFFFF, bits)


def _bisect_threshold(count_gt, t0, cnt0, topk, bits=32):
    def cond(c):
        it, _, cnt = c
        return (it < bits) & (jnp.max(cnt) > topk)

    def body(c):
        it, t, cnt = c
        cand = t + lax.shift_left(jnp.int32(1), bits - 1 - it)
        cnt_cand = count_gt(cand)
        ok = cnt_cand >= topk
        return it + 1, jnp.where(ok, cand, t), jnp.where(ok, cnt_cand, cnt)

    _, t, cnt = lax.while_loop(cond, body, (jnp.int32(0), t0, cnt0))
    return t, cnt


def _dsa_prompt_steps(nq, qb_rows, kb_rows):
    steps = []
    for i in range(nq):
        nv = ((i + 1) * qb_rows + kb_rows - 1) // kb_rows
        steps += [(i, 0, j, nv, j, 0) for j in range(nv)] + [(i, 1, j, nv, nv - 1, j) for j in range(nv)]
    return list(zip(*steps))


def _dsa_prompt_kernel(tab_ref, qb_ref, qi_ref, wt_ref, ki4_ref, kb_ref, vt_ref, o_ref,
                       lhs_s, key_s, top_s, thr_s, qm_s, m_s, acc_s, *, qb_rows, kb_rows, sk, topk, n_steps):
    step = pl.program_id(0)
    i, ph, j, nv = (tab_ref[r * n_steps + step] for r in range(4))
    n_pairs = SA_HEADS // 2

    @pl.when((ph == 0) & (j == 0))
    def _init():
        qi, qb = qi_ref[...], qb_ref[...]
        lane = lax.broadcasted_iota(I32, (qb_rows, 128), 1)
        for h in range(SA_HEADS):
            hi, lo = _split_bf16(qi[:, h * 64:(h + 1) * 64])
            lhs_s[h * qb_rows:(h + 1) * qb_rows, :] = jnp.concatenate([hi, hi, lo, lo], axis=1)
        for p in range(n_pairs):
            qp = qb[:, p * 128:(p + 1) * 128]
            zero = jnp.zeros_like(qp)
            qm_s[p] = jnp.concatenate([jnp.where(lane < 64, qp, zero), jnp.where(lane < 64, zero, qp)], axis=0)
        m_s[...] = jnp.full(m_s.shape, NEG_BIG, F32)
        acc_s[...] = jnp.zeros(acc_s.shape, F32)

    @pl.when(ph == 0)
    def _index():
        wt = wt_ref[...]
        qpos = i * qb_rows + lax.broadcasted_iota(I32, (sk, qb_rows), 1)
        for sub in range(kb_rows // sk):
            z = _dot_nt(ki4_ref[sub * sk:(sub + 1) * sk, :], lhs_s[...])
            sc = wt[0:1] * jnp.maximum(z[:, 0:qb_rows], 0.0)
            for h in range(1, SA_HEADS):
                sc = sc + wt[h:h + 1] * jnp.maximum(z[:, h * qb_rows:(h + 1) * qb_rows], 0.0)
            kpos = j * kb_rows + sub * sk + lax.broadcasted_iota(I32, (sk, qb_rows), 0)
            causal = kpos <= qpos
            sc = jnp.where(jnp.abs(sc) < F32_MIN_NORMAL, 0.0, sc)
            bits = pltpu.bitcast(sc, I32)
            key_s[j, sub * sk:(sub + 1) * sk, :] = jnp.where(causal, jnp.where(bits < 0, bits ^ 0x7FFFFFFF, bits),
                                                             INT_MIN)
            top = pltpu.bitcast(bits & jnp.int32(-65536), F32)
            top_s[j, sub * sk:(sub + 1) * sk, :] = jnp.where(causal, top, -jnp.inf).astype(BF16)

    @pl.when((ph == 1) & (j == 0))
    def _threshold():
        one, zero = jnp.ones((), BF16), jnp.zeros((), BF16)

        def count_top(cand):
            cand = jnp.maximum(cand, -32513)
            cand = jnp.where((cand < 0) & (cand > -129), -129, cand)
            pat = jnp.where(cand >= 0, cand, cand ^ 0x7FFF) & 0xFFFF
            cf = pltpu.bitcast(lax.shift_left(pat, 16), F32).astype(BF16)

            def body(jb, acc):
                x = jnp.where(top_s[jb] > cf, one, zero)
                parts = [x[r * 16:(r + 1) * 16] for r in range(kb_rows // 16)]
                while len(parts) > 1:
                    parts = [parts[a] + parts[a + 1] for a in range(0, len(parts), 2)]
                return acc + parts[0].astype(F32)

            acc = lax.fori_loop(0, nv, body, jnp.zeros((16, qb_rows), F32))
            return jnp.sum(acc, axis=0, keepdims=True).astype(I32)

        def top_step(it, c):
            t, cnt = c
            cand = t + lax.shift_left(jnp.int32(1), 15 - it)
            cnt_cand = count_top(cand)
            ok = cnt_cand >= topk
            return jnp.where(ok, cand, t), jnp.where(ok, cnt_cand, cnt)

        t_top0 = jnp.full((1, qb_rows), -32768, I32)
        t_top, cnt = lax.fori_loop(0, 16, top_step, (t_top0, count_top(t_top0)))

        def count_gt(t):
            def body(jb, acc):
                x = jnp.where(key_s[jb] > t, 1, 0)
                return acc + jnp.sum(x.reshape(kb_rows // 8, 8, qb_rows), axis=0)
            acc = lax.fori_loop(0, nv, body, jnp.zeros((8, qb_rows), I32))
            return jnp.sum(acc, axis=0, keepdims=True)

        t, cnt = _bisect_threshold(count_gt, lax.shift_left(t_top, 16) + 0xFFFF, cnt, topk, bits=16)
        thr_s[...] = t
        tied = cnt > topk

        @pl.when(jnp.max(jnp.where(tied, 1, 0)) > 0)
        def _drop_late_ties():
            need = jnp.where(tied, topk - count_gt(t + 1), NO_TIE).astype(F32)
            ri = lax.broadcasted_iota(I32, (kb_rows, kb_rows), 0)
            ci = lax.broadcasted_iota(I32, (kb_rows, kb_rows), 1)
            upto = (ri >= ci).astype(BF16)

            def body(jb, seen):
                k = key_s[jb]
                eq = jnp.where(k == t + 1, 1.0, 0.0)
                rank = seen + _dot(upto, eq.astype(BF16))
                key_s[jb] = jnp.where(eq * rank > need, INT_MIN, k)
                return seen + jnp.sum(eq, axis=0, keepdims=True)

            lax.fori_loop(0, nv, body, jnp.zeros((1, qb_rows), F32))

    @pl.when(ph == 1)
    def _attend():
        t = thr_s[...]
        for rows in (slice(0, kb_rows),):
            bias = jnp.where(key_s[j, rows, :] > t, 0.0, NEG_BIG)
            bias = jnp.concatenate([bias, bias], axis=1)
            scores = [_dot_nt(kb_ref[rows, p * 128:(p + 1) * 128], qm_s[p]) for p in range(n_pairs)]
            for p in range(n_pairs):
                s = scores[p] + bias
                m_old = m_s[p]
                m_new = jnp.maximum(m_old, jnp.max(s, axis=0, keepdims=True))
                alpha = jnp.exp2(m_old - m_new)
                prb = jnp.exp2(s - m_new).astype(BF16)
                m_s[p] = m_new
                for hh in range(2):
                    h = 2 * p + hh
                    cols = slice(hh * qb_rows, (hh + 1) * qb_rows)
                    span = slice(hh * V_ROWS, (hh + 1) * V_ROWS)
                    pv = _dot(vt_ref[h * V_ROWS:(h + 1) * V_ROWS, rows], prb[:, cols])
                    acc_s[p, span, :] = alpha[:, cols] * acc_s[p, span, :] + pv

    @pl.when((ph == 1) & (j == nv - 1))
    def _finish():
        for p in range(n_pairs):
            a = acc_s[p]
            a = jnp.concatenate([a[hh * V_ROWS:hh * V_ROWS + 64] / a[hh * V_ROWS + 64:hh * V_ROWS + 65]
                                 for hh in range(2)], axis=0)
            o_ref[:, p * 128:(p + 1) * 128] = a.T


def _dsa_prompt(qb, qi, wt, ki4, kb, vt, qb_rows, kb_rows):
    s = qb.shape[0]
    nq, nk = s // qb_rows, s // kb_rows
    sk = min(256, kb_rows)
    tab = _dsa_prompt_steps(nq, qb_rows, kb_rows)
    n = len(tab[0])
    qrow = lambda w: pl.BlockSpec((qb_rows, w), lambda st, t: (t[st], 0))
    kern = functools.partial(_dsa_prompt_kernel, qb_rows=qb_rows, kb_rows=kb_rows, sk=sk, topk=min(TOPK, s // 4),
                             n_steps=n)
    grid_spec = pltpu.PrefetchScalarGridSpec(
        num_scalar_prefetch=1,
        grid=(n,),
        in_specs=[qrow(512), qrow(512),
                  pl.BlockSpec((SA_HEADS, qb_rows), lambda st, t: (0, t[st])),
                  pl.BlockSpec((kb_rows, 256), lambda st, t: (t[4 * n + st], 0)),
                  pl.BlockSpec((kb_rows, 512), lambda st, t: (t[5 * n + st], 0)),
                  pl.BlockSpec((SA_HEADS * V_ROWS, kb_rows), lambda st, t: (0, t[5 * n + st]))],
        out_specs=qrow(512),
        scratch_shapes=[pltpu.VMEM((SA_HEADS * qb_rows, 256), BF16),
                        pltpu.VMEM((nk, kb_rows, qb_rows), I32),
                        pltpu.VMEM((nk, kb_rows, qb_rows), BF16),
                        pltpu.VMEM((1, qb_rows), I32),
                        pltpu.VMEM((SA_HEADS // 2, 2 * qb_rows, 128), BF16),
                        pltpu.VMEM((SA_HEADS // 2, 1, 2 * qb_rows), F32),
                        pltpu.VMEM((SA_HEADS // 2, 2 * V_ROWS, qb_rows), F32)],
    )
    return pl.pallas_call(
        kern,
        grid_spec=grid_spec,
        out_shape=jax.ShapeDtypeStruct((s, 512), F32),
        compiler_params=_params("arbitrary"),
        name="dsa_prompt",
    )(jnp.asarray(tab, I32).reshape(-1), qb, qi, wt, ki4, kb, vt)


PAGES_PER_STEP = 16
INDEX_PAGES_PER_STEP = 32


def _dsa_sample_index_kernel(pt_ref, qi_ref, kw_ref, ki4n_ref, *rest, n_pages, t_len, topk, pps):
    pages = rest[:pps]
    key_ref, thr_ref, lhs_s, wcol_s = rest[pps:]
    j = pl.program_id(1)
    n_steps = n_pages // pps

    @pl.when(j == 0)
    def _init():
        qi, kw = qi_ref[0], kw_ref[0]
        hi, lo = _split_bf16(jnp.concatenate([qi[:, h * 64:(h + 1) * 64] for h in range(SA_HEADS)], axis=0))
        lhs_s[...] = jnp.concatenate([hi, hi, lo, lo], axis=1)
        for h in range(SA_HEADS):
            wcol_s[h] = kw[:, 64 + h:65 + h]

    def head_sum(z):
        sc = wcol_s[0] * jnp.maximum(z[0:t_len], 0.0)
        for h in range(1, SA_HEADS):
            sc = sc + wcol_s[h] * jnp.maximum(z[h * t_len:(h + 1) * t_len], 0.0)
        return sc

    hi, lo = _split_bf16(jnp.concatenate([p[...] for p in pages], axis=1))
    keys = _sortable(head_sum(_dot(lhs_s[...], jnp.concatenate([hi, lo, hi, lo], axis=0))))
    for r in range(pps):
        key_ref[0, j * pps + r] = keys[:, r * PAGE:(r + 1) * PAGE]

    @pl.when(j == n_steps - 1)
    def _finish():
        sc = head_sum(_dot_nt(lhs_s[...], ki4n_ref[0]))
        tq = lax.broadcasted_iota(I32, (t_len, 128), 0)
        ts = lax.broadcasted_iota(I32, (t_len, 128), 1)
        key_ref[0, n_pages] = jnp.where(ts <= tq, _sortable(sc), INT_MIN)

        def count_gt(t):
            x = jnp.where(key_ref[0] > t[None], 1, 0)
            return jnp.sum(jnp.sum(x, axis=0), axis=1, keepdims=True)

        t0 = jnp.full((t_len, 1), INT_MIN, I32)
        thr, cnt = _bisect_threshold(count_gt, t0, count_gt(t0), topk)
        thr_ref[0] = jnp.broadcast_to(thr, (t_len, 128))
        tied = cnt > topk

        @pl.when(jnp.max(jnp.where(tied, 1, 0)) > 0)
        def _drop_late_ties():
            need = jnp.where(tied, topk - count_gt(thr + 1), NO_TIE).astype(F32)
            upto = (lax.broadcasted_iota(I32, (PAGE, PAGE), 0) <= lax.broadcasted_iota(I32, (PAGE, PAGE), 1))

            def body(p, seen):
                k = key_ref[0, p]
                eq = jnp.where(k == thr + 1, 1.0, 0.0)
                rank = seen + _dot(eq.astype(BF16), upto.astype(BF16))
                key_ref[0, p] = jnp.where(eq * rank > need, INT_MIN, k)
                return seen + jnp.sum(eq, axis=1, keepdims=True)

            lax.fori_loop(0, n_pages + 1, body, jnp.zeros((t_len, 1), F32))


def _dsa_sample_index(page_table, qi, kw, ki4_new, cache_kidx_t, layer, t_len):
    b = qi.shape[0]
    n_pages = page_table.shape[1]
    pps = math.gcd(n_pages, INDEX_PAGES_PER_STEP)
    n_steps = n_pages // pps

    def page_spec(r):
        return pl.BlockSpec((None, None, 64, PAGE),
                            lambda i, j, pt: (layer, pt[i * n_pages + j * pps + r], 0, 0))

    kern = functools.partial(_dsa_sample_index_kernel, n_pages=n_pages, t_len=t_len, pps=pps,
                             topk=min(TOPK, (n_pages * PAGE + t_len) // 4))
    grid_spec = pltpu.PrefetchScalarGridSpec(
        num_scalar_prefetch=1,
        grid=(b, n_steps),
        in_specs=[pl.BlockSpec((1, t_len, 512), lambda i, j, pt: (i, 0, 0)),
                  pl.BlockSpec((1, t_len, 128), lambda i, j, pt: (i, 0, 0)),
                  pl.BlockSpec((1, PAGE, 256), lambda i, j, pt: (i, 0, 0))]
        + [page_spec(r) for r in range(pps)],
        out_specs=[pl.BlockSpec((1, n_pages + 1, t_len, 128), lambda i, j, pt: (i, 0, 0, 0)),
                   pl.BlockSpec((1, t_len, 128), lambda i, j, pt: (i, 0, 0))],
        scratch_shapes=[pltpu.VMEM((SA_HEADS * t_len, 256), BF16), pltpu.VMEM((SA_HEADS, t_len, 1), F32)],
    )
    return pl.pallas_call(
        kern,
        grid_spec=grid_spec,
        out_shape=[jax.ShapeDtypeStruct((b, n_pages + 1, t_len, 128), I32),
                   jax.ShapeDtypeStruct((b, t_len, 128), I32)],
        compiler_params=_params("parallel", "arbitrary"),
        name="dsa_sample_index",
    )(page_table.reshape(-1), qi, kw, ki4_new, *([cache_kidx_t] * pps))


def _dsa_sample_attend_kernel(pt_ref, qb_ref, key_ref, keyn_ref, thr_ref, kn_ref, vn_ref, *rest, n_pages, t_len):
    kpages = rest[:PAGES_PER_STEP]
    vpages = rest[PAGES_PER_STEP:2 * PAGES_PER_STEP]
    o_ref, qbd_s, m_s, l_s, acc_s = rest[2 * PAGES_PER_STEP:]
    j = pl.program_id(1)
    n_steps = n_pages // PAGES_PER_STEP
    rows = SA_HEADS * t_len
    thr = thr_ref[0][:, 0:1]

    @pl.when(j == 0)
    def _init():
        qb = qb_ref[0].astype(F32)
        lane = lax.broadcasted_iota(I32, (t_len, 512), 1) // 64
        qbd = jnp.concatenate([jnp.where(lane == h, qb, 0.0) for h in range(SA_HEADS)], axis=0)
        qbd_s[...] = qbd.astype(BF16)
        m_s[...] = jnp.full(m_s.shape, NEG_BIG, F32)
        l_s[...] = jnp.zeros(l_s.shape, F32)
        acc_s[...] = jnp.zeros(acc_s.shape, F32)

    def update(keys, s, pv):
        bias = jnp.where(keys > thr, 0.0, NEG_BIG)
        s = s + jnp.concatenate([bias] * SA_HEADS, axis=0)
        m_old = m_s[...]
        m_new = jnp.maximum(m_old, jnp.max(s, axis=1, keepdims=True))
        alpha = jnp.exp2(m_old - m_new)
        pr = jnp.exp2(s - m_new)
        l_s[...] = alpha * l_s[...] + jnp.sum(pr, axis=1, keepdims=True)
        acc_s[...] = alpha * acc_s[...] + pv(pr.astype(BF16))
        m_s[...] = m_new

    kcat = jnp.concatenate([p[...].astype(BF16) for p in kpages], axis=1)
    vcat = jnp.concatenate([p[...].astype(BF16) for p in vpages], axis=1)
    keys = jnp.concatenate([key_ref[0, r] for r in range(PAGES_PER_STEP)], axis=1)
    update(keys, _dot(qbd_s[...], kcat), lambda pr: _dot_nt(pr, vcat))

    @pl.when(j == n_steps - 1)
    def _finish():
        kn, vn = kn_ref[0].astype(BF16), vn_ref[0].astype(BF16)
        update(keyn_ref[0, 0], _dot_nt(qbd_s[...], kn), lambda pr: _dot(pr, vn))
        o = acc_s[...] / l_s[...]
        lane = lax.broadcasted_iota(I32, (t_len, 512), 1) // 64
        out = jnp.zeros((t_len, 512), F32)
        for h in range(SA_HEADS):
            out = out + jnp.where(lane == h, o[h * t_len:(h + 1) * t_len], 0.0)
        o_ref[0] = out


def _dsa_sample_attend(page_table, qb, keys, thr, k_new, v_new, cache_k_t, cache_v_t, layer, t_len):
    b = qb.shape[0]
    n_pages = page_table.shape[1]
    n_steps = n_pages // PAGES_PER_STEP
    rows = SA_HEADS * t_len

    def page_spec(r):
        return pl.BlockSpec((None, None, 512, PAGE),
                            lambda i, j, pt: (layer, pt[i * n_pages + j * PAGES_PER_STEP + r], 0, 0))

    per_b = lambda n: pl.BlockSpec((1, t_len, n), lambda i, j, pt: (i, 0, 0))
    kern = functools.partial(_dsa_sample_attend_kernel, n_pages=n_pages, t_len=t_len)
    grid_spec = pltpu.PrefetchScalarGridSpec(
        num_scalar_prefetch=1,
        grid=(b, n_steps),
        in_specs=[per_b(512),
                  pl.BlockSpec((1, PAGES_PER_STEP, t_len, 128), lambda i, j, pt: (i, j, 0, 0)),
                  pl.BlockSpec((1, 1, t_len, 128), lambda i, j, pt: (i, n_pages, 0, 0)),
                  per_b(128),
                  pl.BlockSpec((1, PAGE, 512), lambda i, j, pt: (i, 0, 0)),
                  pl.BlockSpec((1, PAGE, 512), lambda i, j, pt: (i, 0, 0))]
        + [page_spec(r) for r in range(PAGES_PER_STEP)] * 2,
        out_specs=per_b(512),
        scratch_shapes=[pltpu.VMEM((rows, 512), BF16), pltpu.VMEM((rows, 1), F32), pltpu.VMEM((rows, 1), F32),
                        pltpu.VMEM((rows, 512), F32)],
    )
    return pl.pallas_call(
        kern,
        grid_spec=grid_spec,
        out_shape=jax.ShapeDtypeStruct((b, t_len, 512), F32),
        compiler_params=_params("parallel", "arbitrary"),
        name="dsa_sample_attend",
    )(page_table.reshape(-1), qb, keys, keys, thr, k_new, v_new,
      *([cache_k_t] * PAGES_PER_STEP), *([cache_v_t] * PAGES_PER_STEP))


def _matmul_kernel(x_ref, w_ref, o_ref):
    o_ref[...] = _dot(x_ref[...].astype(BF16), w_ref[...])


def _matmul(x, w):
    m, n = x.shape[0], w.shape[1]
    return pl.pallas_call(
        _matmul_kernel,
        grid=(1,),
        in_specs=[_full(x.shape), _full(w.shape)],
        out_specs=_full((m, n)),
        out_shape=jax.ShapeDtypeStruct((m, n), F32),
        compiler_params=_params("arbitrary"),
        name="mem_kv_proj",
    )(x, w)


def _post1_kernel(x_ref, yrw_ref, ysa_ref, ylru_ref, wo1_ref, wo2_ref, wo3_ref, g_ref, wq_ref, x1_ref, qm_ref):
    x1 = (x_ref[...] + _dot(yrw_ref[...].astype(BF16), wo1_ref[...])
          + _dot(ysa_ref[...].astype(BF16), wo2_ref[...]) + _dot(ylru_ref[...].astype(BF16), wo3_ref[...]))
    x1_ref[...] = x1
    qm = _dot(_rms(x1, g_ref[...]).astype(BF16), wq_ref[...])
    qm_ref[...] = (qm * (MEM_HEAD_DIM ** -0.5)).astype(BF16)


def _post1(x, yrw, ysa, ylru, wo1, wo2, wo3, g, wq, tm):
    m = x.shape[0]
    row = lambda n: pl.BlockSpec((tm, n), lambda i: (i, 0))
    return pl.pallas_call(
        _post1_kernel,
        grid=(m // tm,),
        in_specs=[row(1024), row(256), row(512), row(256), _full(wo1.shape), _full(wo2.shape), _full(wo3.shape),
                  _full((1, 1024)), _full(wq.shape)],
        out_specs=[row(1024), row(1024)],
        out_shape=[jax.ShapeDtypeStruct((m, 1024), F32), jax.ShapeDtypeStruct((m, 1024), BF16)],
        compiler_params=_params("parallel"),
        name="out_proj_memq",
    )(x, yrw, ysa, ylru, wo1, wo2, wo3, g, wq)


def _mem_attn_kernel(q_ref, mk_ref, mv_ref, o_ref):
    q = q_ref[0]
    mk = mk_ref[0].astype(BF16)
    mv = mv_ref[0].astype(BF16)
    for h in range(MEM_HEADS):
        sl = slice(h * MEM_HEAD_DIM, (h + 1) * MEM_HEAD_DIM)
        s = _dot_nt(q[:, sl], mk[:, sl])
        p = jnp.exp(s - jnp.max(s, axis=1, keepdims=True))
        o = _dot(p.astype(BF16), mv[:, sl]) / jnp.sum(p, axis=1, keepdims=True)
        o_ref[0, :, sl] = o.astype(BF16)


def _mem_attn(q, mk, mv, tm):
    b, t, _ = q.shape
    return pl.pallas_call(
        _mem_attn_kernel,
        grid=(b, t // tm),
        in_specs=[pl.BlockSpec((1, tm, 1024), lambda i, j: (i, j, 0)),
                  pl.BlockSpec((1, 256, 1024), lambda i, j: (i, 0, 0)),
                  pl.BlockSpec((1, 256, 1024), lambda i, j: (i, 0, 0))],
        out_specs=pl.BlockSpec((1, tm, 1024), lambda i, j: (i, j, 0)),
        out_shape=jax.ShapeDtypeStruct((b, t, 1024), BF16),
        compiler_params=_params("parallel", "parallel"),
        name="mem_attn",
    )(q, mk, mv)


def _post2_kernel(x1_ref, o_ref, wo_ref, g_ref, wg_ref, wu_ref, wd_ref, gf_ref, out_ref, *, last):
    x2 = x1_ref[...] + _dot(o_ref[...], wo_ref[...])
    hb = _rms(x2, g_ref[...]).astype(BF16)
    gt = _dot(hb, wg_ref[...])
    up = _dot(hb, wu_ref[...])
    act = (gt * jax.nn.sigmoid(gt) * up).astype(BF16)
    x3 = x2 + _dot(act, wd_ref[...])
    out_ref[...] = _rms(x3, gf_ref[...]) if last else x3


def _post2(x1, o, wo, g, wg, wu, wd, gf, tm, last):
    m = x1.shape[0]
    row = lambda n: pl.BlockSpec((tm, n), lambda i: (i, 0))
    once = lambda w: pl.BlockSpec(w.shape, lambda i: (0, 0), pipeline_mode=pl.Buffered(1))
    return pl.pallas_call(
        functools.partial(_post2_kernel, last=last),
        grid=(m // tm,),
        in_specs=[row(1024), row(1024), once(wo), _full((1, 1024)), once(wg), once(wu), once(wd), _full((1, 1024))],
        out_specs=row(1024),
        out_shape=jax.ShapeDtypeStruct((m, 1024), F32),
        compiler_params=_params("parallel"),
        name="memo_swiglu",
    )(x1, o, wo, g, wg, wu, wd, gf)


def _blockdiag4(w):
    n = w.shape[-1]
    return jnp.einsum("gij,gh->gihj", w, jnp.eye(4, dtype=w.dtype)).reshape(4 * n, 4 * n)


def _state_to_blockdiag(s):
    b = s.shape[0]
    return jnp.einsum("bhvk,hg->bhvgk", s, jnp.eye(4, dtype=s.dtype)).reshape(b, 256, 256)


def _blockdiag_to_state(sbd):
    b = sbd.shape[0]
    return jnp.einsum("bhvgk,hg->bhvk", sbd.reshape(b, 4, 64, 4, 64), jnp.eye(4, dtype=sbd.dtype))


def _layer_weights(l, w_in, w_out, rw, lru, g_mix, g_mem, w_mq, w_mk, w_mv, w_mo, g_ffn, w_gate, w_up, w_down):
    wi = w_in[l]
    o = RW_COLS
    cut = lambda a, n: wi[:, a:a + n].astype(BF16)
    wkw = jnp.pad(wi[:, o + 2048:o + 2120], ((0, 0), (0, 56))).astype(BF16)
    in_ws = (cut(0, 1024), cut(o, 512), cut(o + 512, 512), cut(o + 1024, 512), cut(o + 1536, 512), wkw,
             cut(o + 2120, 512))
    (rw_mu, rw_w0, rw_w2, rw_a0, rw_a2, rw_g2, rw_kk, rw_ka, rw_rk, rw_ln_w, rw_ln_b) = rw
    z = jnp.zeros((64, 256), F32)
    v256 = lambda a: a[l].reshape(1, 256)
    rw_prm = (rw_mu[l].reshape(1, 1024), v256(rw_w0), jnp.concatenate([rw_w2[l], z], axis=0), v256(rw_a0),
              jnp.concatenate([z, rw_a2[l]], axis=0), rw_g2[l], v256(rw_kk), v256(rw_ka), v256(rw_rk),
              v256(rw_ln_w), v256(rw_ln_b))
    (lru_conv_w, lru_conv_b, lru_wa, lru_ba, lru_wx, lru_bx, lru_lambda) = lru
    lru_prm = (lru_conv_w[l], v256(lru_conv_b), _blockdiag4(lru_wa[l]), v256(lru_ba), _blockdiag4(lru_wx[l]),
               v256(lru_bx), v256(lru_lambda))
    wo = w_out[l].astype(BF16)
    return dict(
        in_ws=in_ws, rw=rw_prm, lru=lru_prm, g_mix=g_mix[l].reshape(1, 1024),
        wo=(wo[:256], wo[256:768], wo[768:]), g_mem=g_mem[l].reshape(1, 1024), wq=w_mq[l].astype(BF16),
        wmk=w_mk[l].astype(BF16), wmv=w_mv[l].astype(BF16), wmo=w_mo[l].astype(BF16),
        g_ffn=g_ffn[l].reshape(1, 1024), wg=w_gate[l].astype(BF16), wu=w_up[l].astype(BF16),
        wd=w_down[l].astype(BF16))


def _pick_tile(n, pref):
    t = min(n, pref)
    while n % t:
        t //= 2
    return t


def _prompt_layer(x, mem, w, tabs, g_final, last):
    s = x.shape[0]
    tm = _pick_tile(s, 256)
    urw, kt, vt, qi, _, kwt, ulru, qb, kb, vtb, ki4 = _in_proj(x, w["g_mix"], w["in_ws"], tabs, tm)
    tb = _pick_tile(s, 512)
    y_rw, sbd = _rwkv(urw[None], jnp.zeros((1, 1, 1024), F32), jnp.zeros((1, 256, 256), F32), w["rw"], tb, tb)
    y_lru, h_last = _lru(ulru[None], jnp.zeros((1, 3, 256), F32), jnp.zeros((1, 1, 256), F32), w["lru"], tb, True)
    y_sa = _dsa_prompt(qb, qi, kwt[64:64 + SA_HEADS], ki4, kb, vtb, _pick_tile(s, 256), _pick_tile(s, 1024))
    mk = _matmul(mem, w["wmk"])
    mv = _matmul(mem, w["wmv"])
    x1, qm = _post1(x, y_rw[0], y_sa, y_lru[0], *w["wo"], w["g_mem"], w["wq"], tm)
    o = _mem_attn(qm[None], mk[None], mv[None], tm)
    x3 = _post2(x1, o[0], w["wmo"], w["g_ffn"], w["wg"], w["wu"], w["wd"], g_final, tm, last)
    rows_major = lambda a: jnp.transpose(a.reshape(SA_HEADS, HEAD_DIM, s), (2, 0, 1))[None]
    new = dict(k=rows_major(kt), v=rows_major(vt), kidx=kwt[:64].T[None],
               mk=mk.reshape(1, 256, 4, 256), mv=mv.reshape(1, 256, 4, 256), rwkv=_blockdiag_to_state(sbd),
               shift=urw[None, s - 1], h=h_last[:, 0], conv=ulru[None, s - 3:, :256])
    return x3, new


def _sample_layer(x, l, w, tabs, g_final, cache_k, cache_v, cache_kidx, cache_mem_k, cache_mem_v, state_rwkv,
                  state_rwkv_shift, state_lru_h, state_lru_conv, page_table, db, t, last):
    m = db * t
    urw, kt, vt, qi, kw, _, ulru, qb, _, _, ki4 = _in_proj(x, w["g_mix"], w["in_ws"], tabs, m)
    k, v = kt.T, vt.T
    urw3 = urw.reshape(db, t, 1024)
    u_pad = jnp.pad(urw3, ((0, 0), (0, RW_CHUNK - t), (0, 0)))
    y_rw, sbd = _rwkv(u_pad, state_rwkv_shift[l][:, None], _state_to_blockdiag(state_rwkv[l]), w["rw"], RW_CHUNK, t)
    ulru3 = ulru.reshape(db, t, 512)
    y_lru, h_last = _lru(ulru3, state_lru_conv[l], state_lru_h[l][:, None], w["lru"], t, False)
    pad_new = lambda a: jnp.pad(a.reshape(db, t, -1), ((0, 0), (0, PAGE - t), (0, 0)))
    keys, thr = _dsa_sample_index(page_table, qi.reshape(db, t, 512), kw.reshape(db, t, 128), pad_new(ki4),
                                  cache_kidx, l, t)
    y_sa = _dsa_sample_attend(page_table, qb.reshape(db, t, 512), keys, thr, pad_new(k), pad_new(v),
                              cache_k, cache_v, l, t)
    x1, qm = _post1(x, y_rw[:, :t].reshape(m, 256), y_sa.reshape(m, 512), y_lru.reshape(m, 256), *w["wo"],
                    w["g_mem"], w["wq"], m)
    o = _mem_attn(qm.reshape(db, t, 1024), cache_mem_k[l].reshape(db, 256, 1024),
                  cache_mem_v[l].reshape(db, 256, 1024), t)
    x3 = _post2(x1, o.reshape(m, 1024), w["wmo"], w["g_ffn"], w["wg"], w["wu"], w["wd"], g_final, m, last)
    conv = jnp.concatenate([state_lru_conv[l], ulru3[:, :, :256]], axis=1)[:, t:]
    new = dict(k=k.reshape(db, t, 8, 64), v=v.reshape(db, t, 8, 64), kidx=kw.reshape(db, t, 128)[:, :, :64],
               rwkv=_blockdiag_to_state(sbd), shift=urw3[:, t - 1], h=h_last[:, 0], conv=conv)
    return x3, new


def kernel(x_prompt, x_sample, mem_prompt, cache_k, cache_v, cache_kidx, cache_mem_k, cache_mem_v, state_rwkv, state_rwkv_shift, state_lru_h, state_lru_conv, page_table, g_mix, w_in, w_out, rw_mu, rw_w0, rw_w2, rw_a0, rw_a2, rw_g2, rw_kk, rw_ka, rw_rk, rw_ln_w, rw_ln_b, lru_conv_w, lru_conv_b, lru_wa, lru_ba, lru_wx, lru_bx, lru_lambda, g_mem, w_mq, w_mk, w_mv, w_mo, g_ffn, w_gate, w_up, w_down, g_final):
    depth = w_in.shape[0]
    _, s, _ = x_prompt.shape
    db, t, _ = x_sample.shape
    past_len = page_table.shape[1] * PAGE
    rw = (rw_mu, rw_w0, rw_w2, rw_a0, rw_a2, rw_g2, rw_kk, rw_ka, rw_rk, rw_ln_w, rw_ln_b)
    lru = (lru_conv_w, lru_conv_b, lru_wa, lru_ba, lru_wx, lru_bx, lru_lambda)
    tabs_p = _rope_tables(jnp.arange(s))
    tabs_s = _rope_tables(jnp.tile(past_len + jnp.arange(t), db))
    gf = g_final.reshape(1, 1024)
    n_pool = cache_k.shape[1]
    cache_kidx = jnp.transpose(cache_kidx, (0, 1, 3, 2))
    cache_k = jnp.transpose(cache_k, (0, 1, 3, 4, 2)).reshape(depth, n_pool, SA_WIDTH, PAGE)
    cache_v = jnp.transpose(cache_v, (0, 1, 3, 4, 2)).reshape(depth, n_pool, SA_WIDTH, PAGE)
    xp, xs = x_prompt[0], x_sample.reshape(db * t, 1024)
    mem = mem_prompt[0]
    news_p, news_s = [], []
    for l in range(depth):
        w = _layer_weights(l, w_in, w_out, rw, lru, g_mix, g_mem, w_mq, w_mk, w_mv, w_mo, g_ffn, w_gate, w_up,
                           w_down)
        last = l == depth - 1
        xp, new_p = _prompt_layer(xp, mem, w, tabs_p, gf, last)
        xs, new_s = _sample_layer(xs, l, w, tabs_s, gf, cache_k, cache_v, cache_kidx, cache_mem_k, cache_mem_v,
                                  state_rwkv, state_rwkv_shift, state_lru_h, state_lru_conv, page_table, db, t, last)
        news_p.append(new_p)
        news_s.append(new_s)
    stk = lambda news, name: jnp.stack([n[name] for n in news])
    return (xp[None], xs.reshape(db, t, 1024),
            stk(news_p, "k"), stk(news_p, "v"), stk(news_p, "kidx"), stk(news_p, "mk"), stk(news_p, "mv"),
            stk(news_p, "rwkv"), stk(news_p, "shift"), stk(news_p, "h"), stk(news_p, "conv"),
            stk(news_s, "k"), stk(news_s, "v"), stk(news_s, "kidx"), stk(news_s, "rwkv"), stk(news_s, "shift"),
            stk(news_s, "h"), stk(news_s, "conv"))
```

```python
import functools
import math

import jax
import jax.numpy as jnp
from jax import lax
from jax.experimental import pallas as pl
from jax.experimental.pallas import tpu as pltpu

F32 = jnp.float32
BF16 = jnp.bfloat16
I32 = jnp.int32

HEAD_DIM = 64
RW_HEADS = 4
RW_COLS = 1024
RW_GN_EPS = 64e-5
SA_HEADS = 8
SA_WIDTH = 512
IDX_SCALE = 512.0 ** -0.5
TOPK = 256
LRU_C = 8.0
MEM_HEADS = 4
MEM_HEAD_DIM = 256
ROPE_THETA = 10000.0
NORM_EPS = 1e-6
PAGE = 128
LOG2E = 1.4426950408889634
INT_MIN = -2 ** 31
F32_MIN_NORMAL = 2.0 ** -126
NO_TIE = 2 ** 30
V_ROWS = HEAD_DIM + 16
NEG_BIG = -1e30
VMEM_LIMIT = 56 * 1024 * 1024
RW_CHUNK = 64
KEY_BLOCKS_PER_STEP = 2


def _params(*sem):
    return pltpu.CompilerParams(dimension_semantics=sem, vmem_limit_bytes=VMEM_LIMIT)


def _full(shape):
    n = len(shape)
    return pl.BlockSpec(shape, lambda *_: (0,) * n)


def _dot(a, b):
    return jnp.dot(a, b, preferred_element_type=F32)


def _dot_nt(a, b):
    return lax.dot_general(a, b, (((1,), (1,)), ((), ())), preferred_element_type=F32)


def _rms(x, g):
    return x * lax.rsqrt(jnp.mean(x * x, axis=-1, keepdims=True) + NORM_EPS) * g


def _split_bf16(x):
    hi = x.astype(BF16)
    lo = (x - hi.astype(F32)).astype(BF16)
    return hi, lo


_NN = (((1,), (0,)), ((), ()))
_NT = (((1,), (1,)), ((), ()))
_TN = (((0,), (0,)), ((), ()))


def _dot1(a, b, dims=_NN):
    return lax.dot_general(a.astype(BF16), b.astype(BF16), dims, preferred_element_type=F32)


def _dot3(a, b, dims=_NN):
    ah, al = _split_bf16(a)
    bh, bl = _split_bf16(b)
    d = lambda x, y: lax.dot_general(x, y, dims, preferred_element_type=F32)
    return d(ah, bh) + (d(ah, bl) + d(al, bh))


def _dot_exact_rhs(a, m):
    hi = a.astype(BF16)
    r = a - hi.astype(F32)
    mid = r.astype(BF16)
    lo = (r - mid.astype(F32)).astype(BF16)
    return _dot(hi, m) + (_dot(mid, m) + _dot(lo, m))


def _dot_exact_lhs(m, b):
    hi = b.astype(BF16)
    r = b - hi.astype(F32)
    mid = r.astype(BF16)
    lo = (r - mid.astype(F32)).astype(BF16)
    return _dot(m, hi) + (_dot(m, mid) + _dot(m, lo))


def _rope(u, cos, sin):
    w = u.shape[-1]
    lane = lax.broadcasted_iota(I32, u.shape, 1)
    rot = jnp.where((lane & 63) < 32, pltpu.roll(u, w - 32, 1), pltpu.roll(u, 32, 1))
    return u * cos + rot * sin


def _in_proj_kernel(x_ref, g_ref, wrw_ref, wq_ref, wk_ref, wv_ref, wqi_ref, wkw_ref, wlru_ref,
                    cos_ref, sin_ref, ckw_ref, skw_ref,
                    urw_ref, kt_ref, vt_ref, qi_ref, kw_ref, kwt_ref, ulru_ref, qb_ref, kb_ref, vtb_ref, ki4_ref):
    hb = _rms(x_ref[...], g_ref[...]).astype(BF16)
    urw_ref[...] = _dot(hb, wrw_ref[...])
    ulru_ref[...] = _dot(hb, wlru_ref[...])
    cos = jnp.concatenate([cos_ref[...]] * 4, axis=1)
    sin = jnp.concatenate([sin_ref[...]] * 4, axis=1)
    q = _rope(_dot(hb, wq_ref[...]), cos, sin)
    k = _rope(_dot(hb, wk_ref[...]), cos, sin)
    v = _dot(hb, wv_ref[...])
    qi_ref[...] = _rope(_dot(hb, wqi_ref[...]), cos, sin)
    kw = _rope(_dot(hb, wkw_ref[...]), ckw_ref[...], skw_ref[...])
    vt = v.T
    kt_ref[...] = k.T
    vt_ref[...] = vt
    kw_ref[...] = kw
    kwt_ref[...] = kw.T
    qb_ref[...] = (q * (HEAD_DIM ** -0.5 * LOG2E)).astype(BF16)
    kb_ref[...] = k.astype(BF16)
    ones = jnp.ones((V_ROWS - HEAD_DIM, vt.shape[1]), F32)
    vtb_ref[...] = jnp.concatenate(
        [blk for h in range(SA_HEADS) for blk in (vt[h * HEAD_DIM:(h + 1) * HEAD_DIM], ones)], axis=0).astype(BF16)
    hi, lo = _split_bf16(kw[:, :64])
    ki4_ref[...] = jnp.concatenate([hi, lo, hi, lo], axis=1)


def _in_proj(x, g, ws, tabs, tm):
    m = x.shape[0]
    wrw, wq, wk, wv, wqi, wkw, wlru = ws
    row = lambda n: pl.BlockSpec((tm, n), lambda i: (i, 0))
    col = lambda n: pl.BlockSpec((n, tm), lambda i: (0, i))
    sds = jax.ShapeDtypeStruct
    out_shape = [sds((m, 1024), F32), sds((512, m), F32), sds((512, m), F32), sds((m, 512), F32),
                 sds((m, 128), F32), sds((128, m), F32), sds((m, 512), F32), sds((m, 512), BF16),
                 sds((m, 512), BF16), sds((SA_HEADS * V_ROWS, m), BF16), sds((m, 256), BF16)]
    return pl.pallas_call(
        _in_proj_kernel,
        grid=(m // tm,),
        in_specs=[row(1024), _full((1, 1024))] + [_full(w.shape) for w in ws] + [row(128)] * 4,
        out_specs=[row(1024), col(512), col(512), row(512), row(128), col(128), row(512), row(512), row(512),
                   col(SA_HEADS * V_ROWS), row(256)],
        out_shape=out_shape,
        compiler_params=_params("parallel"),
        name="in_proj",
    )(x, g, wrw, wq, wk, wv, wqi, wkw, wlru, *tabs)


def _rope_tables(pos):
    half = HEAD_DIM // 2
    inv = ROPE_THETA ** (-jnp.arange(half, dtype=F32) / half)
    ang = pos.astype(F32)[:, None] * inv[None, :]
    c, s = jnp.cos(ang), jnp.sin(ang)
    m = pos.shape[0]
    cos = jnp.concatenate([c, c, c, c], axis=1)
    sin = jnp.concatenate([-s, s, -s, s], axis=1)
    ckw = jnp.concatenate([c, c, jnp.full((m, SA_HEADS), IDX_SCALE, F32), jnp.zeros((m, 56), F32)], axis=1)
    skw = jnp.concatenate([-s, s, jnp.zeros((m, 64), F32)], axis=1)
    return cos, sin, ckw, skw


def _rwkv_kernel(u_ref, prev_ref, s0_ref, mu_ref, w0_ref, w2_ref, a0_ref, a2_ref, g2_ref, kk_ref, ka_ref,
                 rk_ref, lnw_ref, lnb_ref, y_ref, sout_ref,
                 sbd, carry, r_s, kn_s, a_s, k2_s, v_s, lw_s, y_s, *, tb_rows, n_valid):
    c_len = RW_CHUNK
    tb = pl.program_id(1)

    @pl.when(tb == 0)
    def _():
        sbd[...] = s0_ref[0]
        carry[...] = prev_ref[0]

    u = u_ref[0]
    row = lax.broadcasted_iota(I32, u.shape, 0)
    u_prev = jnp.where(row == 0, carry[...], pltpu.roll(u, 1, 0))
    last = min(n_valid, tb_rows) - 1
    carry[...] = u[last:last + 1, :]
    us = u + mu_ref[...] * (u_prev - u)
    r, k, v = us[:, 0:256], us[:, 256:512], us[:, 512:768]
    wa, gd = us[:, 768:896], us[:, 896:1024]

    li = lax.broadcasted_iota(I32, (256, 256), 0) // 64
    lj = lax.broadcasted_iota(I32, (256, 256), 1) // 64
    blockdiag = (li == lj).astype(F32)
    head_sum = functools.partial(_dot_exact_rhs, m=blockdiag.astype(BF16))

    xw = w0_ref[...] + _dot3(jnp.tanh(wa), w2_ref[...])
    logw = -math.exp(-0.5) * jax.nn.sigmoid(xw)
    a = jax.nn.sigmoid(a0_ref[...] + _dot3(wa, a2_ref[...]))
    g = _dot3(jax.nn.sigmoid(gd), g2_ref[...])
    kkv = k * kk_ref[...]
    kn = kkv / jnp.maximum(jnp.sqrt(head_sum(kkv * kkv)), 1e-12)
    k2 = k * (1.0 + (a - 1.0) * ka_ref[...])
    bonus = head_sum(r * k2 * rk_ref[...]) * v
    if n_valid < tb_rows:
        ok = lax.broadcasted_iota(I32, (tb_rows, 256), 0) < n_valid
        logw = jnp.where(ok, logw, 0.0)
        kn = jnp.where(ok, kn, 0.0)
        k2 = jnp.where(ok, k2, 0.0)
        v = jnp.where(ok, v, 0.0)
    r_s[...] = r
    kn_s[...] = kn
    a_s[...] = a
    k2_s[...] = k2
    v_s[...] = v
    lw_s[...] = logw

    lane = lax.broadcasted_iota(I32, (1, 256), 1) // 64
    hmask = [(lane == h).astype(F32) for h in range(RW_HEADS)]
    ci = lax.broadcasted_iota(I32, (c_len, c_len), 0)
    cj = lax.broadcasted_iota(I32, (c_len, c_len), 1)
    tri_incl = (ci >= cj).astype(BF16)
    eye = (ci == cj).astype(F32)
    mi = lax.broadcasted_iota(I32, (8 * c_len, 2 * c_len), 0)
    mj = lax.broadcasted_iota(I32, (8 * c_len, 2 * c_len), 1)
    mt, ms = mi & (c_len - 1), mj & (c_len - 1)
    keep = ms < mt + jnp.where(mi < 4 * c_len, 0, 1)
    pick = lambda x: sum(x[h * c_len:(h + 1) * c_len] * hmask[h] for h in range(RW_HEADS))
    n_sq = int(math.log2(c_len)) - 1

    def prepare(c):
        sl = pl.ds(pl.multiple_of(c * c_len, c_len), c_len)
        lw, rc, knc, ac, k2c, vc = lw_s[sl, :], r_s[sl, :], kn_s[sl, :], a_s[sl, :], k2_s[sl, :], v_s[sl, :]
        lcum = _dot_exact_lhs(tri_incl, lw)
        gam, gex, gin = jnp.exp(lcum), jnp.exp(lcum - lw), jnp.exp(-lcum)
        rt, at, bt, kt = rc * gam, knc * gex, -(knc * ac) * gin, k2c * gin
        g_end = gam[c_len - 1:c_len, :]
        lhs = jnp.concatenate([at * hm for hm in hmask] + [rt * hm for hm in hmask], axis=0)
        m = jnp.where(keep, _dot1(lhs, jnp.concatenate([bt, kt], axis=0), _NT), 0.0)
        w1 = pick(_dot1(m[:4 * c_len], jnp.concatenate([jnp.zeros_like(vc), vc], axis=0)))
        return dict(sl=sl, vc=vc, m=m, w1=w1, ar=jnp.concatenate([at, rt], axis=0), g_end=g_end,
                    bk=jnp.concatenate([bt * g_end, kt * g_end], axis=0))

    def invert(ms):
        qs = [m[h * c_len:(h + 1) * c_len, 0:c_len] for m in ms for h in range(RW_HEADS)]
        xs = [eye + q for q in qs]
        for _ in range(n_sq):
            qs = [_dot1(q, q) for q in qs]
            xs = [x + _dot1(x, q) for x, q in zip(xs, qs)]
        return [jnp.concatenate(xs[n * RW_HEADS:(n + 1) * RW_HEADS], axis=0) for n in range(len(ms))]

    def advance(d, inv):
        s = sbd[...]
        asrs = _dot1(d["ar"], s, _NT)
        p = pick(_dot1(inv, asrs[:c_len] + d["w1"]))
        pv = jnp.concatenate([p, d["vc"]], axis=0)
        y_s[d["sl"], :] = asrs[c_len:] + pick(_dot1(d["m"][4 * c_len:], pv))
        sbd[...] = s * d["g_end"] + _dot1(pv, d["bk"], _TN) * blockdiag

    n_chunks = tb_rows // c_len
    group = 4 if n_chunks % 4 == 0 else 1

    def chunk_group(gi, _):
        ds = [prepare(gi * group + n) for n in range(group)]
        for d, inv in zip(ds, invert([d["m"] for d in ds])):
            advance(d, inv)
        return 0

    lax.fori_loop(0, n_chunks // group, chunk_group, 0)

    y = y_s[...]
    mean = head_sum(y) * (1.0 / HEAD_DIM)
    yc = y - mean
    var = head_sum(yc * yc) * (1.0 / HEAD_DIM)
    yn = yc * lax.rsqrt(var + RW_GN_EPS) * lnw_ref[...] + lnb_ref[...]
    y_ref[0] = (yn + bonus) * g
    sout_ref[0] = sbd[...]


def _rwkv(u, prev, s0bd, prm, tb_rows, n_valid):
    b, t, _ = u.shape
    vec = lambda n: _full((1, n))
    kern = functools.partial(_rwkv_kernel, tb_rows=tb_rows, n_valid=n_valid)
    return pl.pallas_call(
        kern,
        grid=(b, t // tb_rows),
        in_specs=[pl.BlockSpec((1, tb_rows, 1024), lambda i, j: (i, j, 0)),
                  pl.BlockSpec((1, 1, 1024), lambda i, j: (i, 0, 0)),
                  pl.BlockSpec((1, 256, 256), lambda i, j: (i, 0, 0)),
                  vec(1024), vec(256), _full((128, 256)), vec(256), _full((128, 256)), _full((128, 256)),
                  vec(256), vec(256), vec(256), vec(256), vec(256)],
        out_specs=[pl.BlockSpec((1, tb_rows, 256), lambda i, j: (i, j, 0)),
                   pl.BlockSpec((1, 256, 256), lambda i, j: (i, 0, 0))],
        out_shape=[jax.ShapeDtypeStruct((b, t, 256), F32), jax.ShapeDtypeStruct((b, 256, 256), F32)],
        scratch_shapes=[pltpu.VMEM((256, 256), F32), pltpu.VMEM((1, 1024), F32)]
        + [pltpu.VMEM((tb_rows, 256), F32)] * 7,
        compiler_params=_params("parallel", "arbitrary"),
        name="rwkv7",
    )(u, prev, s0bd, *prm)


def _lru_kernel(u_ref, cb_ref, h0_ref, cw_ref, cbias_ref, wa_ref, ba_ref, wx_ref, bx_ref, lam_ref,
                out_ref, hl_ref, ext, hc, *, tb_rows, reset_first):
    tb = pl.program_id(1)

    @pl.when(tb == 0)
    def _():
        ext[0:8, :] = jnp.zeros((8, 256), F32)
        ext[5:8, :] = cb_ref[0]
        hc[...] = h0_ref[0]

    u = u_ref[0]
    xb, gate = u[:, :256], u[:, 256:]
    ext[8:8 + tb_rows, :] = xb
    cw = cw_ref[...]
    y = (cbias_ref[...] + cw[0:1] * ext[5:5 + tb_rows, :] + cw[1:2] * ext[6:6 + tb_rows, :]
         + cw[2:3] * ext[7:7 + tb_rows, :] + cw[3:4] * xb)
    ext[0:8, :] = ext[tb_rows:tb_rows + 8, :]
    gate_r = jax.nn.sigmoid(_dot3(y, wa_ref[...]) + ba_ref[...])
    gate_i = jax.nn.sigmoid(_dot3(y, wx_ref[...]) + bx_ref[...])
    lam = lam_ref[...]
    softplus_neg = jnp.maximum(-lam, 0.0) + jnp.log(1.0 + jnp.exp(-jnp.abs(lam)))
    log_a = (-LRU_C * softplus_neg) * gate_r
    a = jnp.exp(log_a)
    mult = jnp.sqrt(1.0 - jnp.exp(2.0 * log_a))
    row = lax.broadcasted_iota(I32, (tb_rows, 256), 0)
    if reset_first:
        mult = jnp.where((row == 0) & (tb == 0), 1.0, mult)
    bv = mult * gate_i * y
    d = 1
    while d < tb_rows:
        live = row >= d
        a_sh = jnp.where(live, pltpu.roll(a, d, 0), 1.0)
        b_sh = jnp.where(live, pltpu.roll(bv, d, 0), 0.0)
        bv = a * b_sh + bv
        a = a * a_sh
        d *= 2
    h = a * hc[...] + bv
    hc[...] = h[tb_rows - 1:tb_rows, :]
    gelu = 0.5 * gate * (1.0 + jnp.tanh(math.sqrt(2.0 / math.pi) * (gate + 0.044715 * gate * gate * gate)))
    out_ref[0] = h * gelu
    hl_ref[0] = h[tb_rows - 1:tb_rows, :]


def _lru(u, conv_buf, h0, prm, tb_rows, reset_first):
    b, t, _ = u.shape
    vec = _full((1, 256))
    kern = functools.partial(_lru_kernel, tb_rows=tb_rows, reset_first=reset_first)
    return pl.pallas_call(
        kern,
        grid=(b, t // tb_rows),
        in_specs=[pl.BlockSpec((1, tb_rows, 512), lambda i, j: (i, j, 0)),
                  pl.BlockSpec((1, 3, 256), lambda i, j: (i, 0, 0)),
                  pl.BlockSpec((1, 1, 256), lambda i, j: (i, 0, 0)),
                  _full((4, 256)), vec, _full((256, 256)), vec, _full((256, 256)), vec, vec],
        out_specs=[pl.BlockSpec((1, tb_rows, 256), lambda i, j: (i, j, 0)),
                   pl.BlockSpec((1, 1, 256), lambda i, j: (i, 0, 0))],
        out_shape=[jax.ShapeDtypeStruct((b, t, 256), F32), jax.ShapeDtypeStruct((b, 1, 256), F32)],
        scratch_shapes=[pltpu.VMEM((tb_rows + 8, 256), F32), pltpu.VMEM((1, 256), F32)],
        compiler_params=_params("parallel", "arbitrary"),
        name="rglru",
    )(u, conv_buf, h0, *prm)


def _sortable(x):
    bits = pltpu.bitcast(jnp.where(jnp.abs(x) < F32_MIN_NORMAL, 0.0, x), I32)
    return jnp.where(bits < 0, bits ^ 0x7FFFFFFF, bits)


def _bisect_threshold(count_gt, t0, cnt0, topk, bits=32):
    def cond(c):
        it, _, cnt = c
        return (it < bits) & (jnp.max(cnt) > topk)

    def body(c):
        it, t, cnt = c
        cand = t + lax.shift_left(jnp.int32(1), bits - 1 - it)
        cnt_cand = count_gt(cand)
        ok = cnt_cand >= topk
        return it + 1, jnp.where(ok, cand, t), jnp.where(ok, cnt_cand, cnt)

    _, t, cnt = lax.while_loop(cond, body, (jnp.int32(0), t0, cnt0))
    return t, cnt


def _dsa_prompt_steps(nq, qb_rows, kb_rows, kpg):
    steps = []
    for i in range(nq):
        nv = ((i + 1) * qb_rows + kb_rows - 1) // kb_rows
        ng = (nv + kpg - 1) // kpg
        steps += [(i, 0, g, nv, g, 0) for g in range(ng)] + [(i, 1, g, nv, ng - 1, g) for g in range(ng)]
    return list(zip(*steps))


def _dsa_prompt_kernel(tab_ref, qb_ref, qi_ref, wt_ref, ki4_ref, kb_ref, vt_ref, o_ref,
                       lhs_s, key_s, top_s, thr_s, qm_s, m_s, acc_s, *, qb_rows, kb_rows, sk, topk, n_steps, kpg):
    step = pl.program_id(0)
    i, ph, g, nv = (tab_ref[r * n_steps + step] for r in range(4))
    n_pairs = SA_HEADS // 2

    @pl.when((ph == 0) & (g == 0))
    def _init():
        qi, qb = qi_ref[...], qb_ref[...]
        lane = lax.broadcasted_iota(I32, (qb_rows, 128), 1)
        for h in range(SA_HEADS):
            hi, lo = _split_bf16(qi[:, h * 64:(h + 1) * 64])
            lhs_s[h * qb_rows:(h + 1) * qb_rows, :] = jnp.concatenate([hi, hi, lo, lo], axis=1)
        for p in range(n_pairs):
            qp = qb[:, p * 128:(p + 1) * 128]
            zero = jnp.zeros_like(qp)
            qm_s[p] = jnp.concatenate([jnp.where(lane < 64, qp, zero), jnp.where(lane < 64, zero, qp)], axis=0)
        m_s[...] = jnp.full(m_s.shape, NEG_BIG, F32)
        acc_s[...] = jnp.zeros(acc_s.shape, F32)

    def index_block(j, base):
        wt = wt_ref[...]
        qpos = i * qb_rows + lax.broadcasted_iota(I32, (sk, qb_rows), 1)
        for sub in range(kb_rows // sk):
            z = _dot_nt(ki4_ref[base + sub * sk:base + (sub + 1) * sk, :], lhs_s[...])
            sc = wt[0:1] * jnp.maximum(z[:, 0:qb_rows], 0.0)
            for h in range(1, SA_HEADS):
                sc = sc + wt[h:h + 1] * jnp.maximum(z[:, h * qb_rows:(h + 1) * qb_rows], 0.0)
            kpos = j * kb_rows + sub * sk + lax.broadcasted_iota(I32, (sk, qb_rows), 0)
            causal = kpos <= qpos
            sc = jnp.where(jnp.abs(sc) < F32_MIN_NORMAL, 0.0, sc)
            bits = pltpu.bitcast(sc, I32)
            key_s[j, sub * sk:(sub + 1) * sk, :] = jnp.where(causal, jnp.where(bits < 0, bits ^ 0x7FFFFFFF, bits),
                                                             INT_MIN)
            top = pltpu.bitcast(bits & jnp.int32(-65536), F32)
            top_s[j, sub * sk:(sub + 1) * sk, :] = jnp.where(causal, top, -jnp.inf).astype(BF16)

    for n in range(kpg):
        pl.when((ph == 0) & (g * kpg + n < nv))(functools.partial(index_block, g * kpg + n, n * kb_rows))

    @pl.when((ph == 1) & (g == 0))
    def _threshold():
        one, zero = jnp.ones((), BF16), jnp.zeros((), BF16)

        def count_top(cand):
            cand = jnp.maximum(cand, -32513)
            cand = jnp.where((cand < 0) & (cand > -129), -129, cand)
            pat = jnp.where(cand >= 0, cand, cand ^ 0x7FFF) & 0xFFFF
            cf = pltpu.bitcast(lax.shift_left(pat, 16), F32).astype(BF16)

            def body(jb, acc):
                x = jnp.where(top_s[jb] > cf, one, zero)
                parts = [x[r * 16:(r + 1) * 16] for r in range(kb_rows // 16)]
                while len(parts) > 1:
                    parts = [parts[a] + parts[a + 1] for a in range(0, len(parts), 2)]
                return acc + parts[0].astype(F32)

            acc = lax.fori_loop(0, nv, body, jnp.zeros((16, qb_rows), F32))
            return jnp.sum(acc, axis=0, keepdims=True).astype(I32)

        def top_step(it, c):
            t, cnt = c
            cand = t + lax.shift_left(jnp.int32(1), 15 - it)
            cnt_cand = count_top(cand)
            ok = cnt_cand >= topk
            return jnp.where(ok, cand, t), jnp.where(ok, cnt_cand, cnt)

        t_top0 = jnp.full((1, qb_rows), -32768, I32)
        t_top, cnt = lax.fori_loop(0, 16, top_step, (t_top0, count_top(t_top0)))

        def count_gt(t):
            def body(jb, acc):
                x = jnp.where(key_s[jb] > t, 1, 0)
                return acc + jnp.sum(x.reshape(kb_rows // 8, 8, qb_rows), axis=0)
            acc = lax.fori_loop(0, nv, body, jnp.zeros((8, qb_rows), I32))
            return jnp.sum(acc, axis=0, keepdims=True)

        t, cnt = _bisect_threshold(count_gt, lax.shift_left(t_top, 16) + 0xFFFF, cnt, topk, bits=16)
        thr_s[...] = t
        tied = cnt > topk

        @pl.when(jnp.max(jnp.where(tied, 1, 0)) > 0)
        def _drop_late_ties():
            need = jnp.where(tied, topk - count_gt(t + 1), NO_TIE).astype(F32)
            ri = lax.broadcasted_iota(I32, (kb_rows, kb_rows), 0)
            ci = lax.broadcasted_iota(I32, (kb_rows, kb_rows), 1)
            upto = (ri >= ci).astype(BF16)

            def body(jb, seen):
                k = key_s[jb]
                eq = jnp.where(k == t + 1, 1.0, 0.0)
                rank = seen + _dot(upto, eq.astype(BF16))
                key_s[jb] = jnp.where(eq * rank > need, INT_MIN, k)
                return seen + jnp.sum(eq, axis=0, keepdims=True)

            lax.fori_loop(0, nv, body, jnp.zeros((1, qb_rows), F32))

    def attend_block(j, base):
        win = slice(base, base + kb_rows)
        bias = jnp.where(key_s[j] > thr_s[...], 0.0, NEG_BIG)
        bias = jnp.concatenate([bias, bias], axis=1)
        for p in range(n_pairs):
            s = _dot_nt(kb_ref[win, p * 128:(p + 1) * 128], qm_s[p]) + bias
            m_old = m_s[p]
            m_new = jnp.maximum(m_old, jnp.max(s, axis=0, keepdims=True))
            alpha = jnp.exp2(m_old - m_new)
            prb = jnp.exp2(s - m_new).astype(BF16)
            m_s[p] = m_new
            for hh in range(2):
                h = 2 * p + hh
                cols = slice(hh * qb_rows, (hh + 1) * qb_rows)
                span = slice(hh * V_ROWS, (hh + 1) * V_ROWS)
                pv = _dot(vt_ref[h * V_ROWS:(h + 1) * V_ROWS, win], prb[:, cols])
                acc_s[p, span, :] = alpha[:, cols] * acc_s[p, span, :] + pv

    for n in range(kpg):
        pl.when((ph == 1) & (g * kpg + n < nv))(functools.partial(attend_block, g * kpg + n, n * kb_rows))

    @pl.when((ph == 1) & ((g + 1) * kpg >= nv))
    def _finish():
        for p in range(n_pairs):
            a = acc_s[p]
            a = jnp.concatenate([a[hh * V_ROWS:hh * V_ROWS + 64] / a[hh * V_ROWS + 64:hh * V_ROWS + 65]
                                 for hh in range(2)], axis=0)
            o_ref[:, p * 128:(p + 1) * 128] = a.T


def _dsa_prompt(qb, qi, wt, ki4, kb, vt, qb_rows, kb_rows):
    s = qb.shape[0]
    nq, nk = s // qb_rows, s // kb_rows
    sk = min(256, kb_rows)
    kpg = min(KEY_BLOCKS_PER_STEP, nk)
    tab = _dsa_prompt_steps(nq, qb_rows, kb_rows, kpg)
    n = len(tab[0])
    win = kpg * kb_rows
    qrow = lambda w: pl.BlockSpec((qb_rows, w), lambda st, t: (t[st], 0))
    kern = functools.partial(_dsa_prompt_kernel, qb_rows=qb_rows, kb_rows=kb_rows, sk=sk, topk=min(TOPK, s // 4),
                             n_steps=n, kpg=kpg)
    grid_spec = pltpu.PrefetchScalarGridSpec(
        num_scalar_prefetch=1,
        grid=(n,),
        in_specs=[qrow(512), qrow(512),
                  pl.BlockSpec((SA_HEADS, qb_rows), lambda st, t: (0, t[st])),
                  pl.BlockSpec((win, 256), lambda st, t: (t[4 * n + st], 0)),
                  pl.BlockSpec((win, 512), lambda st, t: (t[5 * n + st], 0)),
                  pl.BlockSpec((SA_HEADS * V_ROWS, win), lambda st, t: (0, t[5 * n + st]))],
        out_specs=qrow(512),
        scratch_shapes=[pltpu.VMEM((SA_HEADS * qb_rows, 256), BF16),
                        pltpu.VMEM((nk, kb_rows, qb_rows), I32),
                        pltpu.VMEM((nk, kb_rows, qb_rows), BF16),
                        pltpu.VMEM((1, qb_rows), I32),
                        pltpu.VMEM((SA_HEADS // 2, 2 * qb_rows, 128), BF16),
                        pltpu.VMEM((SA_HEADS // 2, 1, 2 * qb_rows), F32),
                        pltpu.VMEM((SA_HEADS // 2, 2 * V_ROWS, qb_rows), F32)],
    )
    return pl.pallas_call(
        kern,
        grid_spec=grid_spec,
        out_shape=jax.ShapeDtypeStruct((s, 512), F32),
        compiler_params=_params("arbitrary"),
        name="dsa_prompt",
    )(jnp.asarray(tab, I32).reshape(-1), qb, qi, wt, ki4, kb, vt)


PAGES_PER_STEP = 16
INDEX_PAGES_PER_STEP = 32


def _dsa_sample_index_kernel(pt_ref, qi_ref, kw_ref, ki4n_ref, *rest, n_pages, t_len, topk, pps):
    pages = rest[:pps]
    key_ref, thr_ref, lhs_s, wcol_s = rest[pps:]
    j = pl.program_id(1)
    n_steps = n_pages // pps

    @pl.when(j == 0)
    def _init():
        qi, kw = qi_ref[0], kw_ref[0]
        hi, lo = _split_bf16(jnp.concatenate([qi[:, h * 64:(h + 1) * 64] for h in range(SA_HEADS)], axis=0))
        lhs_s[...] = jnp.concatenate([hi, hi, lo, lo], axis=1)
        for h in range(SA_HEADS):
            wcol_s[h] = kw[:, 64 + h:65 + h]

    def head_sum(z):
        sc = wcol_s[0] * jnp.maximum(z[0:t_len], 0.0)
        for h in range(1, SA_HEADS):
            sc = sc + wcol_s[h] * jnp.maximum(z[h * t_len:(h + 1) * t_len], 0.0)
        return sc

    hi, lo = _split_bf16(jnp.concatenate([p[...] for p in pages], axis=1))
    keys = _sortable(head_sum(_dot(lhs_s[...], jnp.concatenate([hi, lo, hi, lo], axis=0))))
    for r in range(pps):
        key_ref[0, j * pps + r] = keys[:, r * PAGE:(r + 1) * PAGE]

    @pl.when(j == n_steps - 1)
    def _finish():
        sc = head_sum(_dot_nt(lhs_s[...], ki4n_ref[0]))
        tq = lax.broadcasted_iota(I32, (t_len, 128), 0)
        ts = lax.broadcasted_iota(I32, (t_len, 128), 1)
        key_ref[0, n_pages] = jnp.where(ts <= tq, _sortable(sc), INT_MIN)

        def count_gt(t):
            x = jnp.where(key_ref[0] > t[None], 1, 0)
            return jnp.sum(jnp.sum(x, axis=0), axis=1, keepdims=True)

        t0 = jnp.full((t_len, 1), INT_MIN, I32)
        thr, cnt = _bisect_threshold(count_gt, t0, count_gt(t0), topk)
        thr_ref[0] = jnp.broadcast_to(thr, (t_len, 128))
        tied = cnt > topk

        @pl.when(jnp.max(jnp.where(tied, 1, 0)) > 0)
        def _drop_late_ties():
            need = jnp.where(tied, topk - count_gt(thr + 1), NO_TIE).astype(F32)
            upto = (lax.broadcasted_iota(I32, (PAGE, PAGE), 0) <= lax.broadcasted_iota(I32, (PAGE, PAGE), 1))

            def body(p, seen):
                k = key_ref[0, p]
                eq = jnp.where(k == thr + 1, 1.0, 0.0)
                rank = seen + _dot(eq.astype(BF16), upto.astype(BF16))
                key_ref[0, p] = jnp.where(eq * rank > need, INT_MIN, k)
                return seen + jnp.sum(eq, axis=1, keepdims=True)

            lax.fori_loop(0, n_pages + 1, body, jnp.zeros((t_len, 1), F32))


def _dsa_sample_index(page_table, qi, kw, ki4_new, cache_kidx_t, layer, t_len):
    b = qi.shape[0]
    n_pages = page_table.shape[1]
    pps = math.gcd(n_pages, INDEX_PAGES_PER_STEP)
    n_steps = n_pages // pps

    def page_spec(r):
        return pl.BlockSpec((None, None, 64, PAGE),
                            lambda i, j, pt: (layer, pt[i * n_pages + j * pps + r], 0, 0))

    kern = functools.partial(_dsa_sample_index_kernel, n_pages=n_pages, t_len=t_len, pps=pps,
                             topk=min(TOPK, (n_pages * PAGE + t_len) // 4))
    grid_spec = pltpu.PrefetchScalarGridSpec(
        num_scalar_prefetch=1,
        grid=(b, n_steps),
        in_specs=[pl.BlockSpec((1, t_len, 512), lambda i, j, pt: (i, 0, 0)),
                  pl.BlockSpec((1, t_len, 128), lambda i, j, pt: (i, 0, 0)),
                  pl.BlockSpec((1, PAGE, 256), lambda i, j, pt: (i, 0, 0))]
        + [page_spec(r) for r in range(pps)],
        out_specs=[pl.BlockSpec((1, n_pages + 1, t_len, 128), lambda i, j, pt: (i, 0, 0, 0)),
                   pl.BlockSpec((1, t_len, 128), lambda i, j, pt: (i, 0, 0))],
        scratch_shapes=[pltpu.VMEM((SA_HEADS * t_len, 256), BF16), pltpu.VMEM((SA_HEADS, t_len, 1), F32)],
    )
    return pl.pallas_call(
        kern,
        grid_spec=grid_spec,
        out_shape=[jax.ShapeDtypeStruct((b, n_pages + 1, t_len, 128), I32),
                   jax.ShapeDtypeStruct((b, t_len, 128), I32)],
        compiler_params=_params("parallel", "arbitrary"),
        name="dsa_sample_index",
    )(page_table.reshape(-1), qi, kw, ki4_new, *([cache_kidx_t] * pps))


def _dsa_sample_attend_kernel(pt_ref, qb_ref, key_ref, keyn_ref, thr_ref, kn_ref, vn_ref, *rest, n_pages, t_len):
    kpages = rest[:PAGES_PER_STEP]
    vpages = rest[PAGES_PER_STEP:2 * PAGES_PER_STEP]
    o_ref, qbd_s, m_s, l_s, acc_s = rest[2 * PAGES_PER_STEP:]
    j = pl.program_id(1)
    n_steps = n_pages // PAGES_PER_STEP
    thr = thr_ref[0][:, 0:1]

    @pl.when(j == 0)
    def _init():
        qb = qb_ref[0].astype(F32)
        lane = lax.broadcasted_iota(I32, (t_len, 512), 1) // 64
        qbd = jnp.concatenate([jnp.where(lane == h, qb, 0.0) for h in range(SA_HEADS)], axis=0)
        qbd_s[...] = qbd.astype(BF16)
        m_s[...] = jnp.full(m_s.shape, NEG_BIG, F32)
        l_s[...] = jnp.zeros(l_s.shape, F32)
        acc_s[...] = jnp.zeros(acc_s.shape, F32)

    def update(keys, s, pv):
        bias = jnp.where(keys > thr, 0.0, NEG_BIG)
        s = s + jnp.concatenate([bias] * SA_HEADS, axis=0)
        m_old = m_s[...]
        m_new = jnp.maximum(m_old, jnp.max(s, axis=1, keepdims=True))
        alpha = jnp.exp2(m_old - m_new)
        pr = jnp.exp2(s - m_new)
        l_s[...] = alpha * l_s[...] + jnp.sum(pr, axis=1, keepdims=True)
        acc_s[...] = alpha * acc_s[...] + pv(pr.astype(BF16))
        m_s[...] = m_new

    kcat = jnp.concatenate([p[...].astype(BF16) for p in kpages], axis=1)
    vcat = jnp.concatenate([p[...].astype(BF16) for p in vpages], axis=1)
    keys = jnp.concatenate([key_ref[0, r] for r in range(PAGES_PER_STEP)], axis=1)
    update(keys, _dot(qbd_s[...], kcat), lambda pr: _dot_nt(pr, vcat))

    @pl.when(j == n_steps - 1)
    def _finish():
        kn, vn = kn_ref[0].astype(BF16), vn_ref[0].astype(BF16)
        update(keyn_ref[0, 0], _dot_nt(qbd_s[...], kn), lambda pr: _dot(pr, vn))
        o = acc_s[...] / l_s[...]
        lane = lax.broadcasted_iota(I32, (t_len, 512), 1) // 64
        out = jnp.zeros((t_len, 512), F32)
        for h in range(SA_HEADS):
            out = out + jnp.where(lane == h, o[h * t_len:(h + 1) * t_len], 0.0)
        o_ref[0] = out


def _dsa_sample_attend(page_table, qb, keys, thr, k_new, v_new, cache_k_t, cache_v_t, layer, t_len):
    b = qb.shape[0]
    n_pages = page_table.shape[1]
    n_steps = n_pages // PAGES_PER_STEP
    rows = SA_HEADS * t_len

    def page_spec(r):
        return pl.BlockSpec((None, None, 512, PAGE),
                            lambda i, j, pt: (layer, pt[i * n_pages + j * PAGES_PER_STEP + r], 0, 0))

    per_b = lambda n: pl.BlockSpec((1, t_len, n), lambda i, j, pt: (i, 0, 0))
    kern = functools.partial(_dsa_sample_attend_kernel, n_pages=n_pages, t_len=t_len)
    grid_spec = pltpu.PrefetchScalarGridSpec(
        num_scalar_prefetch=1,
        grid=(b, n_steps),
        in_specs=[per_b(512),
                  pl.BlockSpec((1, PAGES_PER_STEP, t_len, 128), lambda i, j, pt: (i, j, 0, 0)),
                  pl.BlockSpec((1, 1, t_len, 128), lambda i, j, pt: (i, n_pages, 0, 0)),
                  per_b(128),
                  pl.BlockSpec((1, PAGE, 512), lambda i, j, pt: (i, 0, 0)),
                  pl.BlockSpec((1, PAGE, 512), lambda i, j, pt: (i, 0, 0))]
        + [page_spec(r) for r in range(PAGES_PER_STEP)] * 2,
        out_specs=per_b(512),
        scratch_shapes=[pltpu.VMEM((rows, 512), BF16), pltpu.VMEM((rows, 1), F32), pltpu.VMEM((rows, 1), F32),
                        pltpu.VMEM((rows, 512), F32)],
    )
    return pl.pallas_call(
        kern,
        grid_spec=grid_spec,
        out_shape=jax.ShapeDtypeStruct((b, t_len, 512), F32),
        compiler_params=_params("parallel", "arbitrary"),
        name="dsa_sample_attend",
    )(page_table.reshape(-1), qb, keys, keys, thr, k_new, v_new,
      *([cache_k_t] * PAGES_PER_STEP), *([cache_v_t] * PAGES_PER_STEP))


def _matmul_kernel(x_ref, w_ref, o_ref):
    o_ref[...] = _dot(x_ref[...].astype(BF16), w_ref[...])


def _matmul(x, w):
    m, n = x.shape[0], w.shape[1]
    return pl.pallas_call(
        _matmul_kernel,
        grid=(1,),
        in_specs=[_full(x.shape), _full(w.shape)],
        out_specs=_full((m, n)),
        out_shape=jax.ShapeDtypeStruct((m, n), F32),
        compiler_params=_params("arbitrary"),
        name="mem_kv_proj",
    )(x, w)


def _post1_kernel(x_ref, yrw_ref, ysa_ref, ylru_ref, wo1_ref, wo2_ref, wo3_ref, g_ref, wq_ref, x1_ref, qm_ref):
    x1 = (x_ref[...] + _dot(yrw_ref[...].astype(BF16), wo1_ref[...])
          + _dot(ysa_ref[...].astype(BF16), wo2_ref[...]) + _dot(ylru_ref[...].astype(BF16), wo3_ref[...]))
    x1_ref[...] = x1
    qm = _dot(_rms(x1, g_ref[...]).astype(BF16), wq_ref[...])
    qm_ref[...] = (qm * (MEM_HEAD_DIM ** -0.5)).astype(BF16)


def _post1(x, yrw, ysa, ylru, wo1, wo2, wo3, g, wq, tm):
    m = x.shape[0]
    row = lambda n: pl.BlockSpec((tm, n), lambda i: (i, 0))
    return pl.pallas_call(
        _post1_kernel,
        grid=(m // tm,),
        in_specs=[row(1024), row(256), row(512), row(256), _full(wo1.shape), _full(wo2.shape), _full(wo3.shape),
                  _full((1, 1024)), _full(wq.shape)],
        out_specs=[row(1024), row(1024)],
        out_shape=[jax.ShapeDtypeStruct((m, 1024), F32), jax.ShapeDtypeStruct((m, 1024), BF16)],
        compiler_params=_params("parallel"),
        name="out_proj_memq",
    )(x, yrw, ysa, ylru, wo1, wo2, wo3, g, wq)


def _mem_attn_kernel(q_ref, mk_ref, mv_ref, o_ref):
    q = q_ref[0]
    mk = mk_ref[0].astype(BF16)
    mv = mv_ref[0].astype(BF16)
    for h in range(MEM_HEADS):
        sl = slice(h * MEM_HEAD_DIM, (h + 1) * MEM_HEAD_DIM)
        s = _dot_nt(q[:, sl], mk[:, sl])
        p = jnp.exp(s - jnp.max(s, axis=1, keepdims=True))
        o = _dot(p.astype(BF16), mv[:, sl]) / jnp.sum(p, axis=1, keepdims=True)
        o_ref[0, :, sl] = o.astype(BF16)


def _mem_attn(q, mk, mv, tm):
    b, t, _ = q.shape
    return pl.pallas_call(
        _mem_attn_kernel,
        grid=(b, t // tm),
        in_specs=[pl.BlockSpec((1, tm, 1024), lambda i, j: (i, j, 0)),
                  pl.BlockSpec((1, 256, 1024), lambda i, j: (i, 0, 0)),
                  pl.BlockSpec((1, 256, 1024), lambda i, j: (i, 0, 0))],
        out_specs=pl.BlockSpec((1, tm, 1024), lambda i, j: (i, j, 0)),
        out_shape=jax.ShapeDtypeStruct((b, t, 1024), BF16),
        compiler_params=_params("parallel", "parallel"),
        name="mem_attn",
    )(q, mk, mv)


def _post2_kernel(x1_ref, o_ref, wo_ref, g_ref, wg_ref, wu_ref, wd_ref, gf_ref, out_ref, *, last):
    x2 = x1_ref[...] + _dot(o_ref[...], wo_ref[...])
    hb = _rms(x2, g_ref[...]).astype(BF16)
    gt = _dot(hb, wg_ref[...])
    up = _dot(hb, wu_ref[...])
    act = (gt * jax.nn.sigmoid(gt) * up).astype(BF16)
    x3 = x2 + _dot(act, wd_ref[...])
    out_ref[...] = _rms(x3, gf_ref[...]) if last else x3


def _post2(x1, o, wo, g, wg, wu, wd, gf, tm, last):
    m = x1.shape[0]
    row = lambda n: pl.BlockSpec((tm, n), lambda i: (i, 0))
    once = lambda w: pl.BlockSpec(w.shape, lambda i: (0, 0), pipeline_mode=pl.Buffered(1))
    return pl.pallas_call(
        functools.partial(_post2_kernel, last=last),
        grid=(m // tm,),
        in_specs=[row(1024), row(1024), once(wo), _full((1, 1024)), once(wg), once(wu), once(wd), _full((1, 1024))],
        out_specs=row(1024),
        out_shape=jax.ShapeDtypeStruct((m, 1024), F32),
        compiler_params=_params("parallel"),
        name="memo_swiglu",
    )(x1, o, wo, g, wg, wu, wd, gf)


def _blockdiag4(w):
    n = w.shape[-1]
    return jnp.einsum("gij,gh->gihj", w, jnp.eye(4, dtype=w.dtype)).reshape(4 * n, 4 * n)


def _state_to_blockdiag(s):
    b = s.shape[0]
    return jnp.einsum("bhvk,hg->bhvgk", s, jnp.eye(4, dtype=s.dtype)).reshape(b, 256, 256)


def _blockdiag_to_state(sbd):
    b = sbd.shape[0]
    return jnp.einsum("bhvgk,hg->bhvk", sbd.reshape(b, 4, 64, 4, 64), jnp.eye(4, dtype=sbd.dtype))


def _layer_weights(l, w_in, w_out, rw, lru, g_mix, g_mem, w_mq, w_mk, w_mv, w_mo, g_ffn, w_gate, w_up, w_down):
    wi = w_in[l]
    o = RW_COLS
    cut = lambda a, n: wi[:, a:a + n].astype(BF16)
    wkw = jnp.pad(wi[:, o + 2048:o + 2120], ((0, 0), (0, 56))).astype(BF16)
    in_ws = (cut(0, 1024), cut(o, 512), cut(o + 512, 512), cut(o + 1024, 512), cut(o + 1536, 512), wkw,
             cut(o + 2120, 512))
    (rw_mu, rw_w0, rw_w2, rw_a0, rw_a2, rw_g2, rw_kk, rw_ka, rw_rk, rw_ln_w, rw_ln_b) = rw
    z = jnp.zeros((64, 256), F32)
    v256 = lambda a: a[l].reshape(1, 256)
    rw_prm = (rw_mu[l].reshape(1, 1024), v256(rw_w0), jnp.concatenate([rw_w2[l], z], axis=0), v256(rw_a0),
              jnp.concatenate([z, rw_a2[l]], axis=0), rw_g2[l], v256(rw_kk), v256(rw_ka), v256(rw_rk),
              v256(rw_ln_w), v256(rw_ln_b))
    (lru_conv_w, lru_conv_b, lru_wa, lru_ba, lru_wx, lru_bx, lru_lambda) = lru
    lru_prm = (lru_conv_w[l], v256(lru_conv_b), _blockdiag4(lru_wa[l]), v256(lru_ba), _blockdiag4(lru_wx[l]),
               v256(lru_bx), v256(lru_lambda))
    wo = w_out[l].astype(BF16)
    return dict(
        in_ws=in_ws, rw=rw_prm, lru=lru_prm, g_mix=g_mix[l].reshape(1, 1024),
        wo=(wo[:256], wo[256:768], wo[768:]), g_mem=g_mem[l].reshape(1, 1024), wq=w_mq[l].astype(BF16),
        wmk=w_mk[l].astype(BF16), wmv=w_mv[l].astype(BF16), wmo=w_mo[l].astype(BF16),
        g_ffn=g_ffn[l].reshape(1, 1024), wg=w_gate[l].astype(BF16), wu=w_up[l].astype(BF16),
        wd=w_down[l].astype(BF16))


def _pick_tile(n, pref):
    t = min(n, pref)
    while n % t:
        t //= 2
    return t


def _prompt_layer(x, mem, w, tabs, g_final, last):
    s = x.shape[0]
    tm = _pick_tile(s, 256)
    urw, kt, vt, qi, _, kwt, ulru, qb, kb, vtb, ki4 = _in_proj(x, w["g_mix"], w["in_ws"], tabs, tm)
    tb = _pick_tile(s, 512)
    y_rw, sbd = _rwkv(urw[None], jnp.zeros((1, 1, 1024), F32), jnp.zeros((1, 256, 256), F32), w["rw"], tb, tb)
    y_lru, h_last = _lru(ulru[None], jnp.zeros((1, 3, 256), F32), jnp.zeros((1, 1, 256), F32), w["lru"], tb, True)
    y_sa = _dsa_prompt(qb, qi, kwt[64:64 + SA_HEADS], ki4, kb, vtb, _pick_tile(s, 256), _pick_tile(s, 1024))
    mk = _matmul(mem, w["wmk"])
    mv = _matmul(mem, w["wmv"])
    x1, qm = _post1(x, y_rw[0], y_sa, y_lru[0], *w["wo"], w["g_mem"], w["wq"], tm)
    o = _mem_attn(qm[None], mk[None], mv[None], tm)
    x3 = _post2(x1, o[0], w["wmo"], w["g_ffn"], w["wg"], w["wu"], w["wd"], g_final, tm, last)
    rows_major = lambda a: jnp.transpose(a.reshape(SA_HEADS, HEAD_DIM, s), (2, 0, 1))[None]
    new = dict(k=rows_major(kt), v=rows_major(vt), kidx=kwt[:64].T[None],
               mk=mk.reshape(1, 256, 4, 256), mv=mv.reshape(1, 256, 4, 256), rwkv=_blockdiag_to_state(sbd),
               shift=urw[None, s - 1], h=h_last[:, 0], conv=ulru[None, s - 3:, :256])
    return x3, new


def _sample_layer(x, l, w, tabs, g_final, cache_k, cache_v, cache_kidx, cache_mem_k, cache_mem_v, state_rwkv,
                  state_rwkv_shift, state_lru_h, state_lru_conv, page_table, db, t, last):
    m = db * t
    urw, kt, vt, qi, kw, _, ulru, qb, _, _, ki4 = _in_proj(x, w["g_mix"], w["in_ws"], tabs, m)
    k, v = kt.T, vt.T
    urw3 = urw.reshape(db, t, 1024)
    u_pad = jnp.pad(urw3, ((0, 0), (0, RW_CHUNK - t), (0, 0)))
    y_rw, sbd = _rwkv(u_pad, state_rwkv_shift[l][:, None], _state_to_blockdiag(state_rwkv[l]), w["rw"], RW_CHUNK, t)
    ulru3 = ulru.reshape(db, t, 512)
    y_lru, h_last = _lru(ulru3, state_lru_conv[l], state_lru_h[l][:, None], w["lru"], t, False)
    pad_new = lambda a: jnp.pad(a.reshape(db, t, -1), ((0, 0), (0, PAGE - t), (0, 0)))
    keys, thr = _dsa_sample_index(page_table, qi.reshape(db, t, 512), kw.reshape(db, t, 128), pad_new(ki4),
                                  cache_kidx, l, t)
    y_sa = _dsa_sample_attend(page_table, qb.reshape(db, t, 512), keys, thr, pad_new(k), pad_new(v),
                              cache_k, cache_v, l, t)
    x1, qm = _post1(x, y_rw[:, :t].reshape(m, 256), y_sa.reshape(m, 512), y_lru.reshape(m, 256), *w["wo"],
                    w["g_mem"], w["wq"], m)
    o = _mem_attn(qm.reshape(db, t, 1024), cache_mem_k[l].reshape(db, 256, 1024),
                  cache_mem_v[l].reshape(db, 256, 1024), t)
    x3 = _post2(x1, o.reshape(m, 1024), w["wmo"], w["g_ffn"], w["wg"], w["wu"], w["wd"], g_final, m, last)
    conv = jnp.concatenate([state_lru_conv[l], ulru3[:, :, :256]], axis=1)[:, t:]
    new = dict(k=k.reshape(db, t, 8, 64), v=v.reshape(db, t, 8, 64), kidx=kw.reshape(db, t, 128)[:, :, :64],
               rwkv=_blockdiag_to_state(sbd), shift=urw3[:, t - 1], h=h_last[:, 0], conv=conv)
    return x3, new


def kernel(x_prompt, x_sample, mem_prompt, cache_k, cache_v, cache_kidx, cache_mem_k, cache_mem_v, state_rwkv, state_rwkv_shift, state_lru_h, state_lru_conv, page_table, g_mix, w_in, w_out, rw_mu, rw_w0, rw_w2, rw_a0, rw_a2, rw_g2, rw_kk, rw_ka, rw_rk, rw_ln_w, rw_ln_b, lru_conv_w, lru_conv_b, lru_wa, lru_ba, lru_wx, lru_bx, lru_lambda, g_mem, w_mq, w_mk, w_mv, w_mo, g_ffn, w_gate, w_up, w_down, g_final):
    depth = w_in.shape[0]
    _, s, _ = x_prompt.shape
    db, t, _ = x_sample.shape
    past_len = page_table.shape[1] * PAGE
    rw = (rw_mu, rw_w0, rw_w2, rw_a0, rw_a2, rw_g2, rw_kk, rw_ka, rw_rk, rw_ln_w, rw_ln_b)
    lru = (lru_conv_w, lru_conv_b, lru_wa, lru_ba, lru_wx, lru_bx, lru_lambda)
    tabs_p = _rope_tables(jnp.arange(s))
    tabs_s = _rope_tables(jnp.tile(past_len + jnp.arange(t), db))
    gf = g_final.reshape(1, 1024)
    n_pool = cache_k.shape[1]
    cache_kidx = jnp.transpose(cache_kidx, (0, 1, 3, 2))
    cache_k = jnp.transpose(cache_k, (0, 1, 3, 4, 2)).reshape(depth, n_pool, SA_WIDTH, PAGE)
    cache_v = jnp.transpose(cache_v, (0, 1, 3, 4, 2)).reshape(depth, n_pool, SA_WIDTH, PAGE)
    xp, xs = x_prompt[0], x_sample.reshape(db * t, 1024)
    mem = mem_prompt[0]
    news_p, news_s = [], []
    for l in range(depth):
        w = _layer_weights(l, w_in, w_out, rw, lru, g_mix, g_mem, w_mq, w_mk, w_mv, w_mo, g_ffn, w_gate, w_up,
                           w_down)
        last = l == depth - 1
        xp, new_p = _prompt_layer(xp, mem, w, tabs_p, gf, last)
        xs, new_s = _sample_layer(xs, l, w, tabs_s, gf, cache_k, cache_v, cache_kidx, cache_mem_k, cache_mem_v,
                                  state_rwkv, state_rwkv_shift, state_lru_h, state_lru_conv, page_table, db, t, last)
        news_p.append(new_p)
        news_s.append(new_s)
    stk = lambda news, name: jnp.stack([n[name] for n in news])
    return (xp[None], xs.reshape(db, t, 1024),
            stk(news_p, "k"), stk(news_p, "v"), stk(news_p, "kidx"), stk(news_p, "mk"), stk(news_p, "mv"),
            stk(news_p, "rwkv"), stk(news_p, "shift"), stk(news_p, "h"), stk(news_p, "conv"),
            stk(news_s, "k"), stk(news_s, "v"), stk(news_s, "kidx"), stk(news_s, "rwkv"), stk(news_s, "shift"),
            stk(news_s, "h"), stk(news_s, "conv"))
```

```python
import functools
import math

import jax
import jax.numpy as jnp
from jax import lax
from jax.experimental import pallas as pl
from jax.experimental.pallas import tpu as pltpu

F32 = jnp.float32
BF16 = jnp.bfloat16
I32 = jnp.int32

HEAD_DIM = 64
RW_HEADS = 4
RW_COLS = 1024
RW_GN_EPS = 64e-5
SA_HEADS = 8
SA_WIDTH = 512
IDX_SCALE = 512.0 ** -0.5
TOPK = 256
LRU_C = 8.0
MEM_HEADS = 4
MEM_HEAD_DIM = 256
ROPE_THETA = 10000.0
NORM_EPS = 1e-6
PAGE = 128
LOG2E = 1.4426950408889634
INT_MIN = -2 ** 31
F32_MIN_NORMAL = 2.0 ** -126
NO_TIE = 2 ** 30
V_ROWS = HEAD_DIM + 16
NEG_BIG = -1e30
VMEM_LIMIT = 56 * 1024 * 1024
RW_CHUNK = 64
KEY_BLOCKS_PER_STEP = 2


def _params(*sem):
    return pltpu.CompilerParams(dimension_semantics=sem, vmem_limit_bytes=VMEM_LIMIT)


def _full(shape):
    n = len(shape)
    return pl.BlockSpec(shape, lambda *_: (0,) * n)


def _dot(a, b):
    return jnp.dot(a, b, preferred_element_type=F32)


def _dot_nt(a, b):
    return lax.dot_general(a, b, (((1,), (1,)), ((), ())), preferred_element_type=F32)


def _rms(x, g):
    return x * lax.rsqrt(jnp.mean(x * x, axis=-1, keepdims=True) + NORM_EPS) * g


def _split_bf16(x):
    hi = x.astype(BF16)
    lo = (x - hi.astype(F32)).astype(BF16)
    return hi, lo


_NN = (((1,), (0,)), ((), ()))
_NT = (((1,), (1,)), ((), ()))
_TN = (((0,), (0,)), ((), ()))


def _dot1(a, b, dims=_NN):
    return lax.dot_general(a.astype(BF16), b.astype(BF16), dims, preferred_element_type=F32)


def _dot3(a, b, dims=_NN):
    ah, al = _split_bf16(a)
    bh, bl = _split_bf16(b)
    d = lambda x, y: lax.dot_general(x, y, dims, preferred_element_type=F32)
    return d(ah, bh) + (d(ah, bl) + d(al, bh))


def _dot_exact_rhs(a, m):
    hi = a.astype(BF16)
    r = a - hi.astype(F32)
    mid = r.astype(BF16)
    lo = (r - mid.astype(F32)).astype(BF16)
    return _dot(hi, m) + (_dot(mid, m) + _dot(lo, m))


def _dot_exact_lhs(m, b):
    hi = b.astype(BF16)
    r = b - hi.astype(F32)
    mid = r.astype(BF16)
    lo = (r - mid.astype(F32)).astype(BF16)
    return _dot(m, hi) + (_dot(m, mid) + _dot(m, lo))


def _rope(u, cos, sin):
    w = u.shape[-1]
    lane = lax.broadcasted_iota(I32, u.shape, 1)
    rot = jnp.where((lane & 63) < 32, pltpu.roll(u, w - 32, 1), pltpu.roll(u, 32, 1))
    return u * cos + rot * sin


def _in_proj_kernel(x_ref, g_ref, wrw_ref, wq_ref, wk_ref, wv_ref, wqi_ref, wkw_ref, wlru_ref,
                    cos_ref, sin_ref, ckw_ref, skw_ref,
                    urw_ref, kt_ref, vt_ref, qi_ref, kw_ref, kwt_ref, ulru_ref, qb_ref, kb_ref, vtb_ref, ki4_ref):
    hb = _rms(x_ref[...], g_ref[...]).astype(BF16)
    urw_ref[...] = _dot(hb, wrw_ref[...])
    ulru_ref[...] = _dot(hb, wlru_ref[...])
    cos = jnp.concatenate([cos_ref[...]] * 4, axis=1)
    sin = jnp.concatenate([sin_ref[...]] * 4, axis=1)
    q = _rope(_dot(hb, wq_ref[...]), cos, sin)
    k = _rope(_dot(hb, wk_ref[...]), cos, sin)
    v = _dot(hb, wv_ref[...])
    qi_ref[...] = _rope(_dot(hb, wqi_ref[...]), cos, sin)
    kw = _rope(_dot(hb, wkw_ref[...]), ckw_ref[...], skw_ref[...])
    vt = v.T
    kt_ref[...] = k.T
    vt_ref[...] = vt
    kw_ref[...] = kw
    kwt_ref[...] = kw.T
    qb_ref[...] = (q * (HEAD_DIM ** -0.5 * LOG2E)).astype(BF16)
    kb_ref[...] = k.astype(BF16)
    ones = jnp.ones((V_ROWS - HEAD_DIM, vt.shape[1]), F32)
    vtb_ref[...] = jnp.concatenate(
        [blk for h in range(SA_HEADS) for blk in (vt[h * HEAD_DIM:(h + 1) * HEAD_DIM], ones)], axis=0).astype(BF16)
    hi, lo = _split_bf16(kw[:, :64])
    ki4_ref[...] = jnp.concatenate([hi, lo, hi, lo], axis=1)


def _in_proj(x, g, ws, tabs, tm):
    m = x.shape[0]
    wrw, wq, wk, wv, wqi, wkw, wlru = ws
    row = lambda n: pl.BlockSpec((tm, n), lambda i: (i, 0))
    col = lambda n: pl.BlockSpec((n, tm), lambda i: (0, i))
    sds = jax.ShapeDtypeStruct
    out_shape = [sds((m, 1024), F32), sds((512, m), F32), sds((512, m), F32), sds((m, 512), F32),
                 sds((m, 128), F32), sds((128, m), F32), sds((m, 512), F32), sds((m, 512), BF16),
                 sds((m, 512), BF16), sds((SA_HEADS * V_ROWS, m), BF16), sds((m, 256), BF16)]
    return pl.pallas_call(
        _in_proj_kernel,
        grid=(m // tm,),
        in_specs=[row(1024), _full((1, 1024))] + [_full(w.shape) for w in ws] + [row(128)] * 4,
        out_specs=[row(1024), col(512), col(512), row(512), row(128), col(128), row(512), row(512), row(512),
                   col(SA_HEADS * V_ROWS), row(256)],
        out_shape=out_shape,
        compiler_params=_params("parallel"),
        name="in_proj",
    )(x, g, wrw, wq, wk, wv, wqi, wkw, wlru, *tabs)


def _rope_tables(pos):
    half = HEAD_DIM // 2
    inv = ROPE_THETA ** (-jnp.arange(half, dtype=F32) / half)
    ang = pos.astype(F32)[:, None] * inv[None, :]
    c, s = jnp.cos(ang), jnp.sin(ang)
    m = pos.shape[0]
    cos = jnp.concatenate([c, c, c, c], axis=1)
    sin = jnp.concatenate([-s, s, -s, s], axis=1)
    ckw = jnp.concatenate([c, c, jnp.full((m, SA_HEADS), IDX_SCALE, F32), jnp.zeros((m, 56), F32)], axis=1)
    skw = jnp.concatenate([-s, s, jnp.zeros((m, 64), F32)], axis=1)
    return cos, sin, ckw, skw


def _rwkv_kernel(u_ref, prev_ref, s0_ref, mu_ref, w0_ref, w2_ref, a0_ref, a2_ref, g2_ref, kk_ref, ka_ref,
                 rk_ref, lnw_ref, lnb_ref, y_ref, sout_ref,
                 sbd, carry, r_s, kn_s, a_s, k2_s, v_s, lw_s, y_s, *, tb_rows, n_valid):
    c_len = RW_CHUNK
    tb = pl.program_id(1)

    @pl.when(tb == 0)
    def _():
        sbd[...] = s0_ref[0]
        carry[...] = prev_ref[0]

    u = u_ref[0]
    row = lax.broadcasted_iota(I32, u.shape, 0)
    u_prev = jnp.where(row == 0, carry[...], pltpu.roll(u, 1, 0))
    last = min(n_valid, tb_rows) - 1
    carry[...] = u[last:last + 1, :]
    us = u + mu_ref[...] * (u_prev - u)
    r, k, v = us[:, 0:256], us[:, 256:512], us[:, 512:768]
    wa, gd = us[:, 768:896], us[:, 896:1024]

    li = lax.broadcasted_iota(I32, (256, 256), 0) // 64
    lj = lax.broadcasted_iota(I32, (256, 256), 1) // 64
    blockdiag = (li == lj).astype(F32)
    head_sum = functools.partial(_dot_exact_rhs, m=blockdiag.astype(BF16))

    xw = w0_ref[...] + _dot3(jnp.tanh(wa), w2_ref[...])
    logw = -math.exp(-0.5) * jax.nn.sigmoid(xw)
    a = jax.nn.sigmoid(a0_ref[...] + _dot3(wa, a2_ref[...]))
    g = _dot3(jax.nn.sigmoid(gd), g2_ref[...])
    kkv = k * kk_ref[...]
    kn = kkv / jnp.maximum(jnp.sqrt(head_sum(kkv * kkv)), 1e-12)
    k2 = k * (1.0 + (a - 1.0) * ka_ref[...])
    bonus = head_sum(r * k2 * rk_ref[...]) * v
    if n_valid < tb_rows:
        ok = lax.broadcasted_iota(I32, (tb_rows, 256), 0) < n_valid
        logw = jnp.where(ok, logw, 0.0)
        kn = jnp.where(ok, kn, 0.0)
        k2 = jnp.where(ok, k2, 0.0)
        v = jnp.where(ok, v, 0.0)
    r_s[...] = r
    kn_s[...] = kn
    a_s[...] = a
    k2_s[...] = k2
    v_s[...] = v
    lw_s[...] = logw

    lane = lax.broadcasted_iota(I32, (1, 256), 1) // 64
    hmask = [(lane == h).astype(F32) for h in range(RW_HEADS)]
    ci = lax.broadcasted_iota(I32, (c_len, c_len), 0)
    cj = lax.broadcasted_iota(I32, (c_len, c_len), 1)
    tri_incl = (ci >= cj).astype(BF16)
    eye = (ci == cj).astype(F32)
    mi = lax.broadcasted_iota(I32, (8 * c_len, 2 * c_len), 0)
    mj = lax.broadcasted_iota(I32, (8 * c_len, 2 * c_len), 1)
    mt, ms = mi & (c_len - 1), mj & (c_len - 1)
    keep = ms < mt + jnp.where(mi < 4 * c_len, 0, 1)
    pick = lambda x: sum(x[h * c_len:(h + 1) * c_len] * hmask[h] for h in range(RW_HEADS))
    n_sq = int(math.log2(c_len)) - 1

    def prepare(c):
        sl = pl.ds(pl.multiple_of(c * c_len, c_len), c_len)
        lw, rc, knc, ac, k2c, vc = lw_s[sl, :], r_s[sl, :], kn_s[sl, :], a_s[sl, :], k2_s[sl, :], v_s[sl, :]
        lcum = _dot_exact_lhs(tri_incl, lw)
        gam, gex, gin = jnp.exp(lcum), jnp.exp(lcum - lw), jnp.exp(-lcum)
        rt, at, bt, kt = rc * gam, knc * gex, -(knc * ac) * gin, k2c * gin
        g_end = gam[c_len - 1:c_len, :]
        lhs = jnp.concatenate([at * hm for hm in hmask] + [rt * hm for hm in hmask], axis=0)
        m = jnp.where(keep, _dot1(lhs, jnp.concatenate([bt, kt], axis=0), _NT), 0.0)
        w1 = pick(_dot1(m[:4 * c_len], jnp.concatenate([jnp.zeros_like(vc), vc], axis=0)))
        return dict(sl=sl, vc=vc, m=m, w1=w1, ar=jnp.concatenate([at, rt], axis=0), g_end=g_end,
                    bk=jnp.concatenate([bt * g_end, kt * g_end], axis=0))

    def invert(ms):
        qs = [m[h * c_len:(h + 1) * c_len, 0:c_len] for m in ms for h in range(RW_HEADS)]
        xs = [eye + q for q in qs]
        for _ in range(n_sq):
            qs = [_dot1(q, q) for q in qs]
            xs = [x + _dot1(x, q) for x, q in zip(xs, qs)]
        return [jnp.concatenate(xs[n * RW_HEADS:(n + 1) * RW_HEADS], axis=0) for n in range(len(ms))]

    def advance(d, inv):
        s = sbd[...]
        asrs = _dot1(d["ar"], s, _NT)
        p = pick(_dot1(inv, asrs[:c_len] + d["w1"]))
        pv = jnp.concatenate([p, d["vc"]], axis=0)
        y_s[d["sl"], :] = asrs[c_len:] + pick(_dot1(d["m"][4 * c_len:], pv))
        sbd[...] = s * d["g_end"] + _dot1(pv, d["bk"], _TN) * blockdiag

    n_chunks = tb_rows // c_len
    group = 8 if n_chunks % 8 == 0 else 1

    def chunk_group(gi, _):
        ds = [prepare(gi * group + n) for n in range(group)]
        for d, inv in zip(ds, invert([d["m"] for d in ds])):
            advance(d, inv)
        return 0

    lax.fori_loop(0, n_chunks // group, chunk_group, 0)

    y = y_s[...]
    mean = head_sum(y) * (1.0 / HEAD_DIM)
    yc = y - mean
    var = head_sum(yc * yc) * (1.0 / HEAD_DIM)
    yn = yc * lax.rsqrt(var + RW_GN_EPS) * lnw_ref[...] + lnb_ref[...]
    y_ref[0] = (yn + bonus) * g
    sout_ref[0] = sbd[...]


def _rwkv(u, prev, s0bd, prm, tb_rows, n_valid):
    b, t, _ = u.shape
    vec = lambda n: _full((1, n))
    kern = functools.partial(_rwkv_kernel, tb_rows=tb_rows, n_valid=n_valid)
    return pl.pallas_call(
        kern,
        grid=(b, t // tb_rows),
        in_specs=[pl.BlockSpec((1, tb_rows, 1024), lambda i, j: (i, j, 0)),
                  pl.BlockSpec((1, 1, 1024), lambda i, j: (i, 0, 0)),
                  pl.BlockSpec((1, 256, 256), lambda i, j: (i, 0, 0)),
                  vec(1024), vec(256), _full((128, 256)), vec(256), _full((128, 256)), _full((128, 256)),
                  vec(256), vec(256), vec(256), vec(256), vec(256)],
        out_specs=[pl.BlockSpec((1, tb_rows, 256), lambda i, j: (i, j, 0)),
                   pl.BlockSpec((1, 256, 256), lambda i, j: (i, 0, 0))],
        out_shape=[jax.ShapeDtypeStruct((b, t, 256), F32), jax.ShapeDtypeStruct((b, 256, 256), F32)],
        scratch_shapes=[pltpu.VMEM((256, 256), F32), pltpu.VMEM((1, 1024), F32)]
        + [pltpu.VMEM((tb_rows, 256), F32)] * 7,
        compiler_params=_params("parallel", "arbitrary"),
        name="rwkv7",
    )(u, prev, s0bd, *prm)


def _lru_kernel(u_ref, cb_ref, h0_ref, cw_ref, cbias_ref, wa_ref, ba_ref, wx_ref, bx_ref, lam_ref,
                out_ref, hl_ref, ext, hc, *, tb_rows, reset_first):
    tb = pl.program_id(1)

    @pl.when(tb == 0)
    def _():
        ext[0:8, :] = jnp.zeros((8, 256), F32)
        ext[5:8, :] = cb_ref[0]
        hc[...] = h0_ref[0]

    u = u_ref[0]
    xb, gate = u[:, :256], u[:, 256:]
    ext[8:8 + tb_rows, :] = xb
    cw = cw_ref[...]
    y = (cbias_ref[...] + cw[0:1] * ext[5:5 + tb_rows, :] + cw[1:2] * ext[6:6 + tb_rows, :]
         + cw[2:3] * ext[7:7 + tb_rows, :] + cw[3:4] * xb)
    ext[0:8, :] = ext[tb_rows:tb_rows + 8, :]
    gate_r = jax.nn.sigmoid(_dot3(y, wa_ref[...]) + ba_ref[...])
    gate_i = jax.nn.sigmoid(_dot3(y, wx_ref[...]) + bx_ref[...])
    lam = lam_ref[...]
    softplus_neg = jnp.maximum(-lam, 0.0) + jnp.log(1.0 + jnp.exp(-jnp.abs(lam)))
    log_a = (-LRU_C * softplus_neg) * gate_r
    a = jnp.exp(log_a)
    mult = jnp.sqrt(1.0 - jnp.exp(2.0 * log_a))
    row = lax.broadcasted_iota(I32, (tb_rows, 256), 0)
    if reset_first:
        mult = jnp.where((row == 0) & (tb == 0), 1.0, mult)
    bv = mult * gate_i * y
    d = 1
    while d < tb_rows:
        live = row >= d
        a_sh = jnp.where(live, pltpu.roll(a, d, 0), 1.0)
        b_sh = jnp.where(live, pltpu.roll(bv, d, 0), 0.0)
        bv = a * b_sh + bv
        a = a * a_sh
        d *= 2
    h = a * hc[...] + bv
    hc[...] = h[tb_rows - 1:tb_rows, :]
    gelu = 0.5 * gate * (1.0 + jnp.tanh(math.sqrt(2.0 / math.pi) * (gate + 0.044715 * gate * gate * gate)))
    out_ref[0] = h * gelu
    hl_ref[0] = h[tb_rows - 1:tb_rows, :]


def _lru(u, conv_buf, h0, prm, tb_rows, reset_first):
    b, t, _ = u.shape
    vec = _full((1, 256))
    kern = functools.partial(_lru_kernel, tb_rows=tb_rows, reset_first=reset_first)
    return pl.pallas_call(
        kern,
        grid=(b, t // tb_rows),
        in_specs=[pl.BlockSpec((1, tb_rows, 512), lambda i, j: (i, j, 0)),
                  pl.BlockSpec((1, 3, 256), lambda i, j: (i, 0, 0)),
                  pl.BlockSpec((1, 1, 256), lambda i, j: (i, 0, 0)),
                  _full((4, 256)), vec, _full((256, 256)), vec, _full((256, 256)), vec, vec],
        out_specs=[pl.BlockSpec((1, tb_rows, 256), lambda i, j: (i, j, 0)),
                   pl.BlockSpec((1, 1, 256), lambda i, j: (i, 0, 0))],
        out_shape=[jax.ShapeDtypeStruct((b, t, 256), F32), jax.ShapeDtypeStruct((b, 1, 256), F32)],
        scratch_shapes=[pltpu.VMEM((tb_rows + 8, 256), F32), pltpu.VMEM((1, 256), F32)],
        compiler_params=_params("parallel", "arbitrary"),
        name="rglru",
    )(u, conv_buf, h0, *prm)


def _sortable(x):
    bits = pltpu.bitcast(jnp.where(jnp.abs(x) < F32_MIN_NORMAL, 0.0, x), I32)
    return jnp.where(bits < 0, bits ^ 0x7FFFFFFF, bits)


def _bisect_threshold(count_gt, t0, cnt0, topk, bits=32):
    def cond(c):
        it, _, cnt = c
        return (it < bits) & (jnp.max(cnt) > topk)

    def body(c):
        it, t, cnt = c
        cand = t + lax.shift_left(jnp.int32(1), bits - 1 - it)
        cnt_cand = count_gt(cand)
        ok = cnt_cand >= topk
        return it + 1, jnp.where(ok, cand, t), jnp.where(ok, cnt_cand, cnt)

    _, t, cnt = lax.while_loop(cond, body, (jnp.int32(0), t0, cnt0))
    return t, cnt


def _dsa_prompt_steps(nq, qb_rows, kb_rows, kpg):
    steps = []
    for i in range(nq):
        nv = ((i + 1) * qb_rows + kb_rows - 1) // kb_rows
        ng = (nv + kpg - 1) // kpg
        steps += [(i, 0, g, nv, g, 0) for g in range(ng)] + [(i, 1, g, nv, ng - 1, g) for g in range(ng)]
    return list(zip(*steps))


def _dsa_prompt_kernel(tab_ref, qb_ref, qi_ref, wt_ref, ki4_ref, kb_ref, vt_ref, o_ref,
                       lhs_s, key_s, top_s, thr_s, qm_s, m_s, acc_s, *, qb_rows, kb_rows, sk, topk, n_steps, kpg):
    step = pl.program_id(0)
    i, ph, g, nv = (tab_ref[r * n_steps + step] for r in range(4))
    n_pairs = SA_HEADS // 2

    @pl.when((ph == 0) & (g == 0))
    def _init():
        qi, qb = qi_ref[...], qb_ref[...]
        lane = lax.broadcasted_iota(I32, (qb_rows, 128), 1)
        for h in range(SA_HEADS):
            hi, lo = _split_bf16(qi[:, h * 64:(h + 1) * 64])
            lhs_s[h * qb_rows:(h + 1) * qb_rows, :] = jnp.concatenate([hi, hi, lo, lo], axis=1)
        for p in range(n_pairs):
            qp = qb[:, p * 128:(p + 1) * 128]
            zero = jnp.zeros_like(qp)
            qm_s[p] = jnp.concatenate([jnp.where(lane < 64, qp, zero), jnp.where(lane < 64, zero, qp)], axis=0)
        m_s[...] = jnp.full(m_s.shape, NEG_BIG, F32)
        acc_s[...] = jnp.zeros(acc_s.shape, F32)

    def index_block(j, base):
        wt = wt_ref[...]
        qpos = i * qb_rows + lax.broadcasted_iota(I32, (sk, qb_rows), 1)
        for sub in range(kb_rows // sk):
            z = _dot_nt(ki4_ref[base + sub * sk:base + (sub + 1) * sk, :], lhs_s[...])
            sc = wt[0:1] * jnp.maximum(z[:, 0:qb_rows], 0.0)
            for h in range(1, SA_HEADS):
                sc = sc + wt[h:h + 1] * jnp.maximum(z[:, h * qb_rows:(h + 1) * qb_rows], 0.0)
            kpos = j * kb_rows + sub * sk + lax.broadcasted_iota(I32, (sk, qb_rows), 0)
            causal = kpos <= qpos
            sc = jnp.where(jnp.abs(sc) < F32_MIN_NORMAL, 0.0, sc)
            bits = pltpu.bitcast(sc, I32)
            key_s[j, sub * sk:(sub + 1) * sk, :] = jnp.where(causal, jnp.where(bits < 0, bits ^ 0x7FFFFFFF, bits),
                                                             INT_MIN)
            top = pltpu.bitcast(bits & jnp.int32(-65536), F32)
            top_s[j, sub * sk:(sub + 1) * sk, :] = jnp.where(causal, top, -jnp.inf).astype(BF16)

    for n in range(kpg):
        pl.when((ph == 0) & (g * kpg + n < nv))(functools.partial(index_block, g * kpg + n, n * kb_rows))

    @pl.when((ph == 1) & (g == 0))
    def _threshold():
        one, zero = jnp.ones((), BF16), jnp.zeros((), BF16)

        def count_top(cand):
            cand = jnp.maximum(cand, -32513)
            cand = jnp.where((cand < 0) & (cand > -129), -129, cand)
            pat = jnp.where(cand >= 0, cand, cand ^ 0x7FFF) & 0xFFFF
            cf = pltpu.bitcast(lax.shift_left(pat, 16), F32).astype(BF16)

            def body(jb, acc):
                x = jnp.where(top_s[jb] > cf, one, zero)
                parts = [x[r * 16:(r + 1) * 16] for r in range(kb_rows // 16)]
                while len(parts) > 1:
                    parts = [parts[a] + parts[a + 1] for a in range(0, len(parts), 2)]
                return acc + parts[0].astype(F32)

            acc = lax.fori_loop(0, nv, body, jnp.zeros((16, qb_rows), F32))
            return jnp.sum(acc, axis=0, keepdims=True).astype(I32)

        def top_step(it, c):
            t, cnt = c
            cand = t + lax.shift_left(jnp.int32(1), 15 - it)
            cnt_cand = count_top(cand)
            ok = cnt_cand >= topk
            return jnp.where(ok, cand, t), jnp.where(ok, cnt_cand, cnt)

        t_top0 = jnp.full((1, qb_rows), -32768, I32)
        t_top, cnt = lax.fori_loop(0, 16, top_step, (t_top0, count_top(t_top0)))

        def count_gt(t):
            def body(jb, acc):
                x = jnp.where(key_s[jb] > t, 1, 0)
                return acc + jnp.sum(x.reshape(kb_rows // 8, 8, qb_rows), axis=0)
            acc = lax.fori_loop(0, nv, body, jnp.zeros((8, qb_rows), I32))
            return jnp.sum(acc, axis=0, keepdims=True)

        t, cnt = _bisect_threshold(count_gt, lax.shift_left(t_top, 16) + 0xFFFF, cnt, topk, bits=16)
        thr_s[...] = t
        tied = cnt > topk

        @pl.when(jnp.max(jnp.where(tied, 1, 0)) > 0)
        def _drop_late_ties():
            need = jnp.where(tied, topk - count_gt(t + 1), NO_TIE).astype(F32)
            ri = lax.broadcasted_iota(I32, (kb_rows, kb_rows), 0)
            ci = lax.broadcasted_iota(I32, (kb_rows, kb_rows), 1)
            upto = (ri >= ci).astype(BF16)

            def body(jb, seen):
                k = key_s[jb]
                eq = jnp.where(k == t + 1, 1.0, 0.0)
                rank = seen + _dot(upto, eq.astype(BF16))
                key_s[jb] = jnp.where(eq * rank > need, INT_MIN, k)
                return seen + jnp.sum(eq, axis=0, keepdims=True)

            lax.fori_loop(0, nv, body, jnp.zeros((1, qb_rows), F32))

    def attend_block(j, base):
        win = slice(base, base + kb_rows)
        bias = jnp.where(key_s[j] > thr_s[...], 0.0, NEG_BIG)
        bias = jnp.concatenate([bias, bias], axis=1)
        for p in range(n_pairs):
            s = _dot_nt(kb_ref[win, p * 128:(p + 1) * 128], qm_s[p]) + bias
            m_old = m_s[p]
            m_new = jnp.maximum(m_old, jnp.max(s, axis=0, keepdims=True))
            alpha = jnp.exp2(m_old - m_new)
            prb = jnp.exp2(s - m_new).astype(BF16)
            m_s[p] = m_new
            for hh in range(2):
                h = 2 * p + hh
                cols = slice(hh * qb_rows, (hh + 1) * qb_rows)
                span = slice(hh * V_ROWS, (hh + 1) * V_ROWS)
                pv = _dot(vt_ref[h * V_ROWS:(h + 1) * V_ROWS, win], prb[:, cols])
                acc_s[p, span, :] = alpha[:, cols] * acc_s[p, span, :] + pv

    for n in range(kpg):
        pl.when((ph == 1) & (g * kpg + n < nv))(functools.partial(attend_block, g * kpg + n, n * kb_rows))

    @pl.when((ph == 1) & ((g + 1) * kpg >= nv))
    def _finish():
        for p in range(n_pairs):
            a = acc_s[p]
            a = jnp.concatenate([a[hh * V_ROWS:hh * V_ROWS + 64] / a[hh * V_ROWS + 64:hh * V_ROWS + 65]
                                 for hh in range(2)], axis=0)
            o_ref[:, p * 128:(p + 1) * 128] = a.T


def _dsa_prompt(qb, qi, wt, ki4, kb, vt, qb_rows, kb_rows):
    s = qb.shape[0]
    nq, nk = s // qb_rows, s // kb_rows
    sk = min(256, kb_rows)
    kpg = min(KEY_BLOCKS_PER_STEP, nk)
    tab = _dsa_prompt_steps(nq, qb_rows, kb_rows, kpg)
    n = len(tab[0])
    win = kpg * kb_rows
    qrow = lambda w: pl.BlockSpec((qb_rows, w), lambda st, t: (t[st], 0))
    kern = functools.partial(_dsa_prompt_kernel, qb_rows=qb_rows, kb_rows=kb_rows, sk=sk, topk=min(TOPK, s // 4),
                             n_steps=n, kpg=kpg)
    grid_spec = pltpu.PrefetchScalarGridSpec(
        num_scalar_prefetch=1,
        grid=(n,),
        in_specs=[qrow(512), qrow(512),
                  pl.BlockSpec((SA_HEADS, qb_rows), lambda st, t: (0, t[st])),
                  pl.BlockSpec((win, 256), lambda st, t: (t[4 * n + st], 0)),
                  pl.BlockSpec((win, 512), lambda st, t: (t[5 * n + st], 0)),
                  pl.BlockSpec((SA_HEADS * V_ROWS, win), lambda st, t: (0, t[5 * n + st]))],
        out_specs=qrow(512),
        scratch_shapes=[pltpu.VMEM((SA_HEADS * qb_rows, 256), BF16),
                        pltpu.VMEM((nk, kb_rows, qb_rows), I32),
                        pltpu.VMEM((nk, kb_rows, qb_rows), BF16),
                        pltpu.VMEM((1, qb_rows), I32),
                        pltpu.VMEM((SA_HEADS // 2, 2 * qb_rows, 128), BF16),
                        pltpu.VMEM((SA_HEADS // 2, 1, 2 * qb_rows), F32),
                        pltpu.VMEM((SA_HEADS // 2, 2 * V_ROWS, qb_rows), F32)],
    )
    return pl.pallas_call(
        kern,
        grid_spec=grid_spec,
        out_shape=jax.ShapeDtypeStruct((s, 512), F32),
        compiler_params=_params("arbitrary"),
        name="dsa_prompt",
    )(jnp.asarray(tab, I32).reshape(-1), qb, qi, wt, ki4, kb, vt)


PAGES_PER_STEP = 32
INDEX_PAGES_PER_STEP = 32


def _dsa_sample_index_kernel(pt_ref, qi_ref, kw_ref, ki4n_ref, *rest, n_pages, t_len, topk, pps):
    pages = rest[:pps]
    key_ref, thr_ref, lhs_s, wcol_s = rest[pps:]
    j = pl.program_id(1)
    n_steps = n_pages // pps

    @pl.when(j == 0)
    def _init():
        qi, kw = qi_ref[0], kw_ref[0]
        hi, lo = _split_bf16(jnp.concatenate([qi[:, h * 64:(h + 1) * 64] for h in range(SA_HEADS)], axis=0))
        lhs_s[...] = jnp.concatenate([hi, hi, lo, lo], axis=1)
        for h in range(SA_HEADS):
            wcol_s[h] = kw[:, 64 + h:65 + h]

    def head_sum(z):
        sc = wcol_s[0] * jnp.maximum(z[0:t_len], 0.0)
        for h in range(1, SA_HEADS):
            sc = sc + wcol_s[h] * jnp.maximum(z[h * t_len:(h + 1) * t_len], 0.0)
        return sc

    hi, lo = _split_bf16(jnp.concatenate([p[...] for p in pages], axis=1))
    keys = _sortable(head_sum(_dot(lhs_s[...], jnp.concatenate([hi, lo, hi, lo], axis=0))))
    for r in range(pps):
        key_ref[0, j * pps + r] = keys[:, r * PAGE:(r + 1) * PAGE]

    @pl.when(j == n_steps - 1)
    def _finish():
        sc = head_sum(_dot_nt(lhs_s[...], ki4n_ref[0]))
        tq = lax.broadcasted_iota(I32, (t_len, 128), 0)
        ts = lax.broadcasted_iota(I32, (t_len, 128), 1)
        key_ref[0, n_pages] = jnp.where(ts <= tq, _sortable(sc), INT_MIN)

        def count_gt(t):
            x = jnp.where(key_ref[0] > t[None], 1, 0)
            return jnp.sum(jnp.sum(x, axis=0), axis=1, keepdims=True)

        t0 = jnp.full((t_len, 1), INT_MIN, I32)
        thr, cnt = _bisect_threshold(count_gt, t0, count_gt(t0), topk)
        thr_ref[0] = jnp.broadcast_to(thr, (t_len, 128))
        tied = cnt > topk

        @pl.when(jnp.max(jnp.where(tied, 1, 0)) > 0)
        def _drop_late_ties():
            need = jnp.where(tied, topk - count_gt(thr + 1), NO_TIE).astype(F32)
            upto = (lax.broadcasted_iota(I32, (PAGE, PAGE), 0) <= lax.broadcasted_iota(I32, (PAGE, PAGE), 1))

            def body(p, seen):
                k = key_ref[0, p]
                eq = jnp.where(k == thr + 1, 1.0, 0.0)
                rank = seen + _dot(eq.astype(BF16), upto.astype(BF16))
                key_ref[0, p] = jnp.where(eq * rank > need, INT_MIN, k)
                return seen + jnp.sum(eq, axis=1, keepdims=True)

            lax.fori_loop(0, n_pages + 1, body, jnp.zeros((t_len, 1), F32))


def _dsa_sample_index(page_table, qi, kw, ki4_new, cache_kidx_t, layer, t_len):
    b = qi.shape[0]
    n_pages = page_table.shape[1]
    pps = math.gcd(n_pages, INDEX_PAGES_PER_STEP)
    n_steps = n_pages // pps

    def page_spec(r):
        return pl.BlockSpec((None, None, 64, PAGE),
                            lambda i, j, pt: (layer, pt[i * n_pages + j * pps + r], 0, 0))

    kern = functools.partial(_dsa_sample_index_kernel, n_pages=n_pages, t_len=t_len, pps=pps,
                             topk=min(TOPK, (n_pages * PAGE + t_len) // 4))
    grid_spec = pltpu.PrefetchScalarGridSpec(
        num_scalar_prefetch=1,
        grid=(b, n_steps),
        in_specs=[pl.BlockSpec((1, t_len, 512), lambda i, j, pt: (i, 0, 0)),
                  pl.BlockSpec((1, t_len, 128), lambda i, j, pt: (i, 0, 0)),
                  pl.BlockSpec((1, PAGE, 256), lambda i, j, pt: (i, 0, 0))]
        + [page_spec(r) for r in range(pps)],
        out_specs=[pl.BlockSpec((1, n_pages + 1, t_len, 128), lambda i, j, pt: (i, 0, 0, 0)),
                   pl.BlockSpec((1, t_len, 128), lambda i, j, pt: (i, 0, 0))],
        scratch_shapes=[pltpu.VMEM((SA_HEADS * t_len, 256), BF16), pltpu.VMEM((SA_HEADS, t_len, 1), F32)],
    )
    return pl.pallas_call(
        kern,
        grid_spec=grid_spec,
        out_shape=[jax.ShapeDtypeStruct((b, n_pages + 1, t_len, 128), I32),
                   jax.ShapeDtypeStruct((b, t_len, 128), I32)],
        compiler_params=_params("parallel", "arbitrary"),
        name="dsa_sample_index",
    )(page_table.reshape(-1), qi, kw, ki4_new, *([cache_kidx_t] * pps))


def _dsa_sample_attend_kernel(pt_ref, qb_ref, key_ref, keyn_ref, thr_ref, kn_ref, vn_ref, *rest, n_pages, t_len,
                              pps):
    kpages = rest[:pps]
    vpages = rest[pps:2 * pps]
    o_ref, qbd_s, m_s, l_s, acc_s = rest[2 * pps:]
    j = pl.program_id(1)
    n_steps = n_pages // pps
    thr = thr_ref[0][:, 0:1]

    @pl.when(j == 0)
    def _init():
        qb = qb_ref[0].astype(F32)
        lane = lax.broadcasted_iota(I32, (t_len, 512), 1) // 64
        qbd = jnp.concatenate([jnp.where(lane == h, qb, 0.0) for h in range(SA_HEADS)], axis=0)
        qbd_s[...] = qbd.astype(BF16)
        m_s[...] = jnp.full(m_s.shape, NEG_BIG, F32)
        l_s[...] = jnp.zeros(l_s.shape, F32)
        acc_s[...] = jnp.zeros(acc_s.shape, F32)

    def update(keys, s, pv):
        bias = jnp.where(keys > thr, 0.0, NEG_BIG)
        s = s + jnp.concatenate([bias] * SA_HEADS, axis=0)
        m_old = m_s[...]
        m_new = jnp.maximum(m_old, jnp.max(s, axis=1, keepdims=True))
        alpha = jnp.exp2(m_old - m_new)
        pr = jnp.exp2(s - m_new)
        l_s[...] = alpha * l_s[...] + jnp.sum(pr, axis=1, keepdims=True)
        acc_s[...] = alpha * acc_s[...] + pv(pr.astype(BF16))
        m_s[...] = m_new

    kcat = jnp.concatenate([p[...].astype(BF16) for p in kpages], axis=1)
    vcat = jnp.concatenate([p[...].astype(BF16) for p in vpages], axis=1)
    keys = jnp.concatenate([key_ref[0, r] for r in range(pps)], axis=1)
    update(keys, _dot(qbd_s[...], kcat), lambda pr: _dot_nt(pr, vcat))

    @pl.when(j == n_steps - 1)
    def _finish():
        kn, vn = kn_ref[0].astype(BF16), vn_ref[0].astype(BF16)
        update(keyn_ref[0, 0], _dot_nt(qbd_s[...], kn), lambda pr: _dot(pr, vn))
        o = acc_s[...] / l_s[...]
        lane = lax.broadcasted_iota(I32, (t_len, 512), 1) // 64
        out = jnp.zeros((t_len, 512), F32)
        for h in range(SA_HEADS):
            out = out + jnp.where(lane == h, o[h * t_len:(h + 1) * t_len], 0.0)
        o_ref[0] = out


def _dsa_sample_attend(page_table, qb, keys, thr, k_new, v_new, cache_k_t, cache_v_t, layer, t_len):
    b = qb.shape[0]
    n_pages = page_table.shape[1]
    pps = math.gcd(n_pages, PAGES_PER_STEP)
    n_steps = n_pages // pps
    rows = SA_HEADS * t_len

    def page_spec(r):
        return pl.BlockSpec((None, None, 512, PAGE),
                            lambda i, j, pt: (layer, pt[i * n_pages + j * pps + r], 0, 0))

    per_b = lambda n: pl.BlockSpec((1, t_len, n), lambda i, j, pt: (i, 0, 0))
    kern = functools.partial(_dsa_sample_attend_kernel, n_pages=n_pages, t_len=t_len, pps=pps)
    grid_spec = pltpu.PrefetchScalarGridSpec(
        num_scalar_prefetch=1,
        grid=(b, n_steps),
        in_specs=[per_b(512),
                  pl.BlockSpec((1, pps, t_len, 128), lambda i, j, pt: (i, j, 0, 0)),
                  pl.BlockSpec((1, 1, t_len, 128), lambda i, j, pt: (i, n_pages, 0, 0)),
                  per_b(128),
                  pl.BlockSpec((1, PAGE, 512), lambda i, j, pt: (i, 0, 0)),
                  pl.BlockSpec((1, PAGE, 512), lambda i, j, pt: (i, 0, 0))]
        + [page_spec(r) for r in range(pps)] * 2,
        out_specs=per_b(512),
        scratch_shapes=[pltpu.VMEM((rows, 512), BF16), pltpu.VMEM((rows, 1), F32), pltpu.VMEM((rows, 1), F32),
                        pltpu.VMEM((rows, 512), F32)],
    )
    return pl.pallas_call(
        kern,
        grid_spec=grid_spec,
        out_shape=jax.ShapeDtypeStruct((b, t_len, 512), F32),
        compiler_params=_params("parallel", "arbitrary"),
        name="dsa_sample_attend",
    )(page_table.reshape(-1), qb, keys, keys, thr, k_new, v_new,
      *([cache_k_t] * pps), *([cache_v_t] * pps))


def _matmul_kernel(x_ref, w_ref, o_ref):
    o_ref[...] = _dot(x_ref[...].astype(BF16), w_ref[...])


def _matmul(x, w):
    m, n = x.shape[0], w.shape[1]
    return pl.pallas_call(
        _matmul_kernel,
        grid=(1,),
        in_specs=[_full(x.shape), _full(w.shape)],
        out_specs=_full((m, n)),
        out_shape=jax.ShapeDtypeStruct((m, n), F32),
        compiler_params=_params("arbitrary"),
        name="mem_kv_proj",
    )(x, w)


def _post1_kernel(x_ref, yrw_ref, ysa_ref, ylru_ref, wo1_ref, wo2_ref, wo3_ref, g_ref, wq_ref, x1_ref, qm_ref):
    x1 = (x_ref[...] + _dot(yrw_ref[...].astype(BF16), wo1_ref[...])
          + _dot(ysa_ref[...].astype(BF16), wo2_ref[...]) + _dot(ylru_ref[...].astype(BF16), wo3_ref[...]))
    x1_ref[...] = x1
    qm = _dot(_rms(x1, g_ref[...]).astype(BF16), wq_ref[...])
    qm_ref[...] = (qm * (MEM_HEAD_DIM ** -0.5)).astype(BF16)


def _post1(x, yrw, ysa, ylru, wo1, wo2, wo3, g, wq, tm):
    m = x.shape[0]
    row = lambda n: pl.BlockSpec((tm, n), lambda i: (i, 0))
    return pl.pallas_call(
        _post1_kernel,
        grid=(m // tm,),
        in_specs=[row(1024), row(256), row(512), row(256), _full(wo1.shape), _full(wo2.shape), _full(wo3.shape),
                  _full((1, 1024)), _full(wq.shape)],
        out_specs=[row(1024), row(1024)],
        out_shape=[jax.ShapeDtypeStruct((m, 1024), F32), jax.ShapeDtypeStruct((m, 1024), BF16)],
        compiler_params=_params("parallel"),
        name="out_proj_memq",
    )(x, yrw, ysa, ylru, wo1, wo2, wo3, g, wq)


def _mem_attn_kernel(q_ref, mk_ref, mv_ref, o_ref):
    q = q_ref[0]
    mk = mk_ref[0].astype(BF16)
    mv = mv_ref[0].astype(BF16)
    for h in range(MEM_HEADS):
        sl = slice(h * MEM_HEAD_DIM, (h + 1) * MEM_HEAD_DIM)
        s = _dot_nt(q[:, sl], mk[:, sl])
        p = jnp.exp(s - jnp.max(s, axis=1, keepdims=True))
        o = _dot(p.astype(BF16), mv[:, sl]) / jnp.sum(p, axis=1, keepdims=True)
        o_ref[0, :, sl] = o.astype(BF16)


def _mem_attn(q, mk, mv, tm):
    b, t, _ = q.shape
    return pl.pallas_call(
        _mem_attn_kernel,
        grid=(b, t // tm),
        in_specs=[pl.BlockSpec((1, tm, 1024), lambda i, j: (i, j, 0)),
                  pl.BlockSpec((1, 256, 1024), lambda i, j: (i, 0, 0)),
                  pl.BlockSpec((1, 256, 1024), lambda i, j: (i, 0, 0))],
        out_specs=pl.BlockSpec((1, tm, 1024), lambda i, j: (i, j, 0)),
        out_shape=jax.ShapeDtypeStruct((b, t, 1024), BF16),
        compiler_params=_params("parallel", "parallel"),
        name="mem_attn",
    )(q, mk, mv)


def _post2_kernel(x1_ref, o_ref, wo_ref, g_ref, wg_ref, wu_ref, wd_ref, gf_ref, out_ref, *, last):
    x2 = x1_ref[...] + _dot(o_ref[...], wo_ref[...])
    hb = _rms(x2, g_ref[...]).astype(BF16)
    gt = _dot(hb, wg_ref[...])
    up = _dot(hb, wu_ref[...])
    act = (gt * jax.nn.sigmoid(gt) * up).astype(BF16)
    x3 = x2 + _dot(act, wd_ref[...])
    out_ref[...] = _rms(x3, gf_ref[...]) if last else x3


def _post2(x1, o, wo, g, wg, wu, wd, gf, tm, last):
    m = x1.shape[0]
    row = lambda n: pl.BlockSpec((tm, n), lambda i: (i, 0))
    once = lambda w: pl.BlockSpec(w.shape, lambda i: (0, 0), pipeline_mode=pl.Buffered(1))
    return pl.pallas_call(
        functools.partial(_post2_kernel, last=last),
        grid=(m // tm,),
        in_specs=[row(1024), row(1024), once(wo), _full((1, 1024)), once(wg), once(wu), once(wd), _full((1, 1024))],
        out_specs=row(1024),
        out_shape=jax.ShapeDtypeStruct((m, 1024), F32),
        compiler_params=_params("parallel"),
        name="memo_swiglu",
    )(x1, o, wo, g, wg, wu, wd, gf)


def _blockdiag4(w):
    n = w.shape[-1]
    return jnp.einsum("gij,gh->gihj", w, jnp.eye(4, dtype=w.dtype)).reshape(4 * n, 4 * n)


def _state_to_blockdiag(s):
    b = s.shape[0]
    return jnp.einsum("bhvk,hg->bhvgk", s, jnp.eye(4, dtype=s.dtype)).reshape(b, 256, 256)


def _blockdiag_to_state(sbd):
    b = sbd.shape[0]
    return jnp.einsum("bhvgk,hg->bhvk", sbd.reshape(b, 4, 64, 4, 64), jnp.eye(4, dtype=sbd.dtype))


def _layer_weights(l, w_in, w_out, rw, lru, g_mix, g_mem, w_mq, w_mk, w_mv, w_mo, g_ffn, w_gate, w_up, w_down):
    wi = w_in[l]
    o = RW_COLS
    cut = lambda a, n: wi[:, a:a + n].astype(BF16)
    wkw = jnp.pad(wi[:, o + 2048:o + 2120], ((0, 0), (0, 56))).astype(BF16)
    in_ws = (cut(0, 1024), cut(o, 512), cut(o + 512, 512), cut(o + 1024, 512), cut(o + 1536, 512), wkw,
             cut(o + 2120, 512))
    (rw_mu, rw_w0, rw_w2, rw_a0, rw_a2, rw_g2, rw_kk, rw_ka, rw_rk, rw_ln_w, rw_ln_b) = rw
    z = jnp.zeros((64, 256), F32)
    v256 = lambda a: a[l].reshape(1, 256)
    rw_prm = (rw_mu[l].reshape(1, 1024), v256(rw_w0), jnp.concatenate([rw_w2[l], z], axis=0), v256(rw_a0),
              jnp.concatenate([z, rw_a2[l]], axis=0), rw_g2[l], v256(rw_kk), v256(rw_ka), v256(rw_rk),
              v256(rw_ln_w), v256(rw_ln_b))
    (lru_conv_w, lru_conv_b, lru_wa, lru_ba, lru_wx, lru_bx, lru_lambda) = lru
    lru_prm = (lru_conv_w[l], v256(lru_conv_b), _blockdiag4(lru_wa[l]), v256(lru_ba), _blockdiag4(lru_wx[l]),
               v256(lru_bx), v256(lru_lambda))
    wo = w_out[l].astype(BF16)
    return dict(
        in_ws=in_ws, rw=rw_prm, lru=lru_prm, g_mix=g_mix[l].reshape(1, 1024),
        wo=(wo[:256], wo[256:768], wo[768:]), g_mem=g_mem[l].reshape(1, 1024), wq=w_mq[l].astype(BF16),
        wmk=w_mk[l].astype(BF16), wmv=w_mv[l].astype(BF16), wmo=w_mo[l].astype(BF16),
        g_ffn=g_ffn[l].reshape(1, 1024), wg=w_gate[l].astype(BF16), wu=w_up[l].astype(BF16),
        wd=w_down[l].astype(BF16))


def _pick_tile(n, pref):
    t = min(n, pref)
    while n % t:
        t //= 2
    return t


def _prompt_layer(x, mem, w, tabs, g_final, last):
    s = x.shape[0]
    tm = _pick_tile(s, 256)
    urw, kt, vt, qi, _, kwt, ulru, qb, kb, vtb, ki4 = _in_proj(x, w["g_mix"], w["in_ws"], tabs, tm)
    tb = _pick_tile(s, 512)
    y_rw, sbd = _rwkv(urw[None], jnp.zeros((1, 1, 1024), F32), jnp.zeros((1, 256, 256), F32), w["rw"], tb, tb)
    y_lru, h_last = _lru(ulru[None], jnp.zeros((1, 3, 256), F32), jnp.zeros((1, 1, 256), F32), w["lru"], tb, True)
    y_sa = _dsa_prompt(qb, qi, kwt[64:64 + SA_HEADS], ki4, kb, vtb, _pick_tile(s, 256), _pick_tile(s, 1024))
    mk = _matmul(mem, w["wmk"])
    mv = _matmul(mem, w["wmv"])
    x1, qm = _post1(x, y_rw[0], y_sa, y_lru[0], *w["wo"], w["g_mem"], w["wq"], tm)
    o = _mem_attn(qm[None], mk[None], mv[None], tm)
    x3 = _post2(x1, o[0], w["wmo"], w["g_ffn"], w["wg"], w["wu"], w["wd"], g_final, tm, last)
    rows_major = lambda a: jnp.transpose(a.reshape(SA_HEADS, HEAD_DIM, s), (2, 0, 1))[None]
    new = dict(k=rows_major(kt), v=rows_major(vt), kidx=kwt[:64].T[None],
               mk=mk.reshape(1, 256, 4, 256), mv=mv.reshape(1, 256, 4, 256), rwkv=_blockdiag_to_state(sbd),
               shift=urw[None, s - 1], h=h_last[:, 0], conv=ulru[None, s - 3:, :256])
    return x3, new


def _sample_layer(x, l, w, tabs, g_final, cache_k, cache_v, cache_kidx, cache_mem_k, cache_mem_v, state_rwkv,
                  state_rwkv_shift, state_lru_h, state_lru_conv, page_table, db, t, last):
    m = db * t
    urw, kt, vt, qi, kw, _, ulru, qb, _, _, ki4 = _in_proj(x, w["g_mix"], w["in_ws"], tabs, m)
    k, v = kt.T, vt.T
    urw3 = urw.reshape(db, t, 1024)
    u_pad = jnp.pad(urw3, ((0, 0), (0, RW_CHUNK - t), (0, 0)))
    y_rw, sbd = _rwkv(u_pad, state_rwkv_shift[l][:, None], _state_to_blockdiag(state_rwkv[l]), w["rw"], RW_CHUNK, t)
    ulru3 = ulru.reshape(db, t, 512)
    y_lru, h_last = _lru(ulru3, state_lru_conv[l], state_lru_h[l][:, None], w["lru"], t, False)
    pad_new = lambda a: jnp.pad(a.reshape(db, t, -1), ((0, 0), (0, PAGE - t), (0, 0)))
    keys, thr = _dsa_sample_index(page_table, qi.reshape(db, t, 512), kw.reshape(db, t, 128), pad_new(ki4),
                                  cache_kidx, l, t)
    y_sa = _dsa_sample_attend(page_table, qb.reshape(db, t, 512), keys, thr, pad_new(k), pad_new(v),
                              cache_k, cache_v, l, t)
    x1, qm = _post1(x, y_rw[:, :t].reshape(m, 256), y_sa.reshape(m, 512), y_lru.reshape(m, 256), *w["wo"],
                    w["g_mem"], w["wq"], m)
    o = _mem_attn(qm.reshape(db, t, 1024), cache_mem_k[l].reshape(db, 256, 1024),
                  cache_mem_v[l].reshape(db, 256, 1024), t)
    x3 = _post2(x1, o.reshape(m, 1024), w["wmo"], w["g_ffn"], w["wg"], w["wu"], w["wd"], g_final, m, last)
    conv = jnp.concatenate([state_lru_conv[l], ulru3[:, :, :256]], axis=1)[:, t:]
    new = dict(k=k.reshape(db, t, 8, 64), v=v.reshape(db, t, 8, 64), kidx=kw.reshape(db, t, 128)[:, :, :64],
               rwkv=_blockdiag_to_state(sbd), shift=urw3[:, t - 1], h=h_last[:, 0], conv=conv)
    return x3, new


def kernel(x_prompt, x_sample, mem_prompt, cache_k, cache_v, cache_kidx, cache_mem_k, cache_mem_v, state_rwkv, state_rwkv_shift, state_lru_h, state_lru_conv, page_table, g_mix, w_in, w_out, rw_mu, rw_w0, rw_w2, rw_a0, rw_a2, rw_g2, rw_kk, rw_ka, rw_rk, rw_ln_w, rw_ln_b, lru_conv_w, lru_conv_b, lru_wa, lru_ba, lru_wx, lru_bx, lru_lambda, g_mem, w_mq, w_mk, w_mv, w_mo, g_ffn, w_gate, w_up, w_down, g_final):
    depth = w_in.shape[0]
    _, s, _ = x_prompt.shape
    db, t, _ = x_sample.shape
    past_len = page_table.shape[1] * PAGE
    rw = (rw_mu, rw_w0, rw_w2, rw_a0, rw_a2, rw_g2, rw_kk, rw_ka, rw_rk, rw_ln_w, rw_ln_b)
    lru = (lru_conv_w, lru_conv_b, lru_wa, lru_ba, lru_wx, lru_bx, lru_lambda)
    tabs_p = _rope_tables(jnp.arange(s))
    tabs_s = _rope_tables(jnp.tile(past_len + jnp.arange(t), db))
    gf = g_final.reshape(1, 1024)
    n_pool = cache_k.shape[1]
    cache_kidx = jnp.transpose(cache_kidx, (0, 1, 3, 2))
    cache_k = jnp.transpose(cache_k, (0, 1, 3, 4, 2)).reshape(depth, n_pool, SA_WIDTH, PAGE)
    cache_v = jnp.transpose(cache_v, (0, 1, 3, 4, 2)).reshape(depth, n_pool, SA_WIDTH, PAGE)
    xp, xs = x_prompt[0], x_sample.reshape(db * t, 1024)
    mem = mem_prompt[0]
    news_p, news_s = [], []
    for l in range(depth):
        w = _layer_weights(l, w_in, w_out, rw, lru, g_mix, g_mem, w_mq, w_mk, w_mv, w_mo, g_ffn, w_gate, w_up,
                           w_down)
        last = l == depth - 1
        xp, new_p = _prompt_layer(xp, mem, w, tabs_p, gf, last)
        xs, new_s = _sample_layer(xs, l, w, tabs_s, gf, cache_k, cache_v, cache_kidx, cache_mem_k, cache_mem_v,
                                  state_rwkv, state_rwkv_shift, state_lru_h, state_lru_conv, page_table, db, t, last)
        news_p.append(new_p)
        news_s.append(new_s)
    stk = lambda news, name: jnp.stack([n[name] for n in news])
    return (xp[None], xs.reshape(db, t, 1024),
            stk(news_p, "k"), stk(news_p, "v"), stk(news_p, "kidx"), stk(news_p, "mk"), stk(news_p, "mv"),
            stk(news_p, "rwkv"), stk(news_p, "shift"), stk(news_p, "h"), stk(news_p, "conv"),
            stk(news_s, "k"), stk(news_s, "v"), stk(news_s, "kidx"), stk(news_s, "rwkv"), stk(news_s, "shift"),
            stk(news_s, "h"), stk(news_s, "conv"))
```

```python
import functools
import math

import jax
import jax.numpy as jnp
from jax import lax
from jax.experimental import pallas as pl
from jax.experimental.pallas import tpu as pltpu

F32 = jnp.float32
BF16 = jnp.bfloat16
I32 = jnp.int32

HEAD_DIM = 64
RW_HEADS = 4
RW_COLS = 1024
RW_GN_EPS = 64e-5
SA_HEADS = 8
SA_WIDTH = 512
IDX_SCALE = 512.0 ** -0.5
TOPK = 256
LRU_C = 8.0
MEM_HEADS = 4
MEM_HEAD_DIM = 256
ROPE_THETA = 10000.0
NORM_EPS = 1e-6
PAGE = 128
LOG2E = 1.4426950408889634
INT_MIN = -2 ** 31
F32_MIN_NORMAL = 2.0 ** -126
NO_TIE = 2 ** 30
V_ROWS = HEAD_DIM + 16
NEG_BIG = -1e30
VMEM_LIMIT = 56 * 1024 * 1024
RW_CHUNK = 64
KEY_BLOCKS_PER_STEP = 2


def _params(*sem):
    return pltpu.CompilerParams(dimension_semantics=sem, vmem_limit_bytes=VMEM_LIMIT)


def _full(shape):
    n = len(shape)
    return pl.BlockSpec(shape, lambda *_: (0,) * n)


def _dot(a, b):
    return jnp.dot(a, b, preferred_element_type=F32)


def _dot_nt(a, b):
    return lax.dot_general(a, b, (((1,), (1,)), ((), ())), preferred_element_type=F32)


def _rms(x, g):
    return x * lax.rsqrt(jnp.mean(x * x, axis=-1, keepdims=True) + NORM_EPS) * g


def _split_bf16(x):
    hi = x.astype(BF16)
    lo = (x - hi.astype(F32)).astype(BF16)
    return hi, lo


_NN = (((1,), (0,)), ((), ()))
_NT = (((1,), (1,)), ((), ()))
_TN = (((0,), (0,)), ((), ()))


def _dot1(a, b, dims=_NN):
    return lax.dot_general(a.astype(BF16), b.astype(BF16), dims, preferred_element_type=F32)


def _dot3(a, b, dims=_NN):
    ah, al = _split_bf16(a)
    bh, bl = _split_bf16(b)
    d = lambda x, y: lax.dot_general(x, y, dims, preferred_element_type=F32)
    return d(ah, bh) + (d(ah, bl) + d(al, bh))


def _dot_exact_rhs(a, m):
    hi = a.astype(BF16)
    r = a - hi.astype(F32)
    mid = r.astype(BF16)
    lo = (r - mid.astype(F32)).astype(BF16)
    return _dot(hi, m) + (_dot(mid, m) + _dot(lo, m))


def _dot_exact_lhs(m, b):
    hi = b.astype(BF16)
    r = b - hi.astype(F32)
    mid = r.astype(BF16)
    lo = (r - mid.astype(F32)).astype(BF16)
    return _dot(m, hi) + (_dot(m, mid) + _dot(m, lo))


def _rope(u, cos, sin):
    w = u.shape[-1]
    lane = lax.broadcasted_iota(I32, u.shape, 1)
    rot = jnp.where((lane & 63) < 32, pltpu.roll(u, w - 32, 1), pltpu.roll(u, 32, 1))
    return u * cos + rot * sin


def _in_proj_kernel(x_ref, g_ref, wrw_ref, wq_ref, wk_ref, wv_ref, wqi_ref, wkw_ref, wlru_ref,
                    cos_ref, sin_ref, ckw_ref, skw_ref,
                    urw_ref, kt_ref, vt_ref, qi_ref, kw_ref, kwt_ref, ulru_ref, qb_ref, kb_ref, vtb_ref, ki4_ref):
    hb = _rms(x_ref[...], g_ref[...]).astype(BF16)
    urw_ref[...] = _dot(hb, wrw_ref[...])
    ulru_ref[...] = _dot(hb, wlru_ref[...])
    cos = jnp.concatenate([cos_ref[...]] * 4, axis=1)
    sin = jnp.concatenate([sin_ref[...]] * 4, axis=1)
    q = _rope(_dot(hb, wq_ref[...]), cos, sin)
    k = _rope(_dot(hb, wk_ref[...]), cos, sin)
    v = _dot(hb, wv_ref[...])
    qi_ref[...] = _rope(_dot(hb, wqi_ref[...]), cos, sin)
    kw = _rope(_dot(hb, wkw_ref[...]), ckw_ref[...], skw_ref[...])
    vt = v.T
    kt_ref[...] = k.T
    vt_ref[...] = vt
    kw_ref[...] = kw
    kwt_ref[...] = kw.T
    qb_ref[...] = (q * (HEAD_DIM ** -0.5 * LOG2E)).astype(BF16)
    kb_ref[...] = k.astype(BF16)
    ones = jnp.ones((V_ROWS - HEAD_DIM, vt.shape[1]), F32)
    vtb_ref[...] = jnp.concatenate(
        [blk for h in range(SA_HEADS) for blk in (vt[h * HEAD_DIM:(h + 1) * HEAD_DIM], ones)], axis=0).astype(BF16)
    hi, lo = _split_bf16(kw[:, :64])
    ki4_ref[...] = jnp.concatenate([hi, lo, hi, lo], axis=1)


def _in_proj(x, g, ws, tabs, tm):
    m = x.shape[0]
    wrw, wq, wk, wv, wqi, wkw, wlru = ws
    row = lambda n: pl.BlockSpec((tm, n), lambda i: (i, 0))
    col = lambda n: pl.BlockSpec((n, tm), lambda i: (0, i))
    sds = jax.ShapeDtypeStruct
    out_shape = [sds((m, 1024), F32), sds((512, m), F32), sds((512, m), F32), sds((m, 512), F32),
                 sds((m, 128), F32), sds((128, m), F32), sds((m, 512), F32), sds((m, 512), BF16),
                 sds((m, 512), BF16), sds((SA_HEADS * V_ROWS, m), BF16), sds((m, 256), BF16)]
    return pl.pallas_call(
        _in_proj_kernel,
        grid=(m // tm,),
        in_specs=[row(1024), _full((1, 1024))] + [_full(w.shape) for w in ws] + [row(128)] * 4,
        out_specs=[row(1024), col(512), col(512), row(512), row(128), col(128), row(512), row(512), row(512),
                   col(SA_HEADS * V_ROWS), row(256)],
        out_shape=out_shape,
        compiler_params=_params("parallel"),
        name="in_proj",
    )(x, g, wrw, wq, wk, wv, wqi, wkw, wlru, *tabs)


def _rope_tables(pos):
    half = HEAD_DIM // 2
    inv = ROPE_THETA ** (-jnp.arange(half, dtype=F32) / half)
    ang = pos.astype(F32)[:, None] * inv[None, :]
    c, s = jnp.cos(ang), jnp.sin(ang)
    m = pos.shape[0]
    cos = jnp.concatenate([c, c, c, c], axis=1)
    sin = jnp.concatenate([-s, s, -s, s], axis=1)
    ckw = jnp.concatenate([c, c, jnp.full((m, SA_HEADS), IDX_SCALE, F32), jnp.zeros((m, 56), F32)], axis=1)
    skw = jnp.concatenate([-s, s, jnp.zeros((m, 64), F32)], axis=1)
    return cos, sin, ckw, skw


def _rwkv_kernel(u_ref, prev_ref, s0_ref, mu_ref, w0_ref, w2_ref, a0_ref, a2_ref, g2_ref, kk_ref, ka_ref,
                 rk_ref, lnw_ref, lnb_ref, y_ref, sout_ref,
                 sbd, carry, r_s, kn_s, a_s, k2_s, v_s, lw_s, y_s, *, tb_rows, n_valid):
    c_len = RW_CHUNK
    tb = pl.program_id(1)

    @pl.when(tb == 0)
    def _():
        sbd[...] = s0_ref[0]
        carry[...] = prev_ref[0]

    u = u_ref[0]
    row = lax.broadcasted_iota(I32, u.shape, 0)
    u_prev = jnp.where(row == 0, carry[...], pltpu.roll(u, 1, 0))
    last = min(n_valid, tb_rows) - 1
    carry[...] = u[last:last + 1, :]
    us = u + mu_ref[...] * (u_prev - u)
    r, k, v = us[:, 0:256], us[:, 256:512], us[:, 512:768]
    wa, gd = us[:, 768:896], us[:, 896:1024]

    li = lax.broadcasted_iota(I32, (256, 256), 0) // 64
    lj = lax.broadcasted_iota(I32, (256, 256), 1) // 64
    blockdiag = (li == lj).astype(F32)
    head_sum = functools.partial(_dot_exact_rhs, m=blockdiag.astype(BF16))

    xw = w0_ref[...] + _dot3(jnp.tanh(wa), w2_ref[...])
    logw = -math.exp(-0.5) * jax.nn.sigmoid(xw)
    a = jax.nn.sigmoid(a0_ref[...] + _dot3(wa, a2_ref[...]))
    g = _dot3(jax.nn.sigmoid(gd), g2_ref[...])
    kkv = k * kk_ref[...]
    kn = kkv / jnp.maximum(jnp.sqrt(head_sum(kkv * kkv)), 1e-12)
    k2 = k * (1.0 + (a - 1.0) * ka_ref[...])
    bonus = head_sum(r * k2 * rk_ref[...]) * v
    if n_valid < tb_rows:
        ok = lax.broadcasted_iota(I32, (tb_rows, 256), 0) < n_valid
        logw = jnp.where(ok, logw, 0.0)
        kn = jnp.where(ok, kn, 0.0)
        k2 = jnp.where(ok, k2, 0.0)
        v = jnp.where(ok, v, 0.0)
    r_s[...] = r
    kn_s[...] = kn
    a_s[...] = a
    k2_s[...] = k2
    v_s[...] = v
    lw_s[...] = logw

    lane = lax.broadcasted_iota(I32, (1, 256), 1) // 64
    hmask = [(lane == h).astype(F32) for h in range(RW_HEADS)]
    ci = lax.broadcasted_iota(I32, (c_len, c_len), 0)
    cj = lax.broadcasted_iota(I32, (c_len, c_len), 1)
    tri_incl = (ci >= cj).astype(BF16)
    eye = (ci == cj).astype(F32)
    mi = lax.broadcasted_iota(I32, (8 * c_len, 2 * c_len), 0)
    mj = lax.broadcasted_iota(I32, (8 * c_len, 2 * c_len), 1)
    mt, ms = mi & (c_len - 1), mj & (c_len - 1)
    keep = ms < mt + jnp.where(mi < 4 * c_len, 0, 1)
    pick = lambda x: sum(x[h * c_len:(h + 1) * c_len] * hmask[h] for h in range(RW_HEADS))
    n_sq = int(math.log2(c_len)) - 1

    def prepare(c):
        sl = pl.ds(pl.multiple_of(c * c_len, c_len), c_len)
        lw, rc, knc, ac, k2c, vc = lw_s[sl, :], r_s[sl, :], kn_s[sl, :], a_s[sl, :], k2_s[sl, :], v_s[sl, :]
        lcum = _dot_exact_lhs(tri_incl, lw)
        gam, gex, gin = jnp.exp(lcum), jnp.exp(lcum - lw), jnp.exp(-lcum)
        rt, at, bt, kt = rc * gam, knc * gex, -(knc * ac) * gin, k2c * gin
        g_end = gam[c_len - 1:c_len, :]
        lhs = jnp.concatenate([at * hm for hm in hmask] + [rt * hm for hm in hmask], axis=0)
        m = jnp.where(keep, _dot1(lhs, jnp.concatenate([bt, kt], axis=0), _NT), 0.0)
        w1 = pick(_dot1(m[:4 * c_len], jnp.concatenate([jnp.zeros_like(vc), vc], axis=0)))
        return dict(sl=sl, vc=vc, m=m, w1=w1, ar=jnp.concatenate([at, rt], axis=0), g_end=g_end,
                    bk=jnp.concatenate([bt * g_end, kt * g_end], axis=0))

    def invert(ms):
        qs = [m[h * c_len:(h + 1) * c_len, 0:c_len] for m in ms for h in range(RW_HEADS)]
        xs = [eye + q for q in qs]
        for _ in range(n_sq):
            qs = [_dot1(q, q) for q in qs]
            xs = [x + _dot1(x, q) for x, q in zip(xs, qs)]
        return [jnp.concatenate(xs[n * RW_HEADS:(n + 1) * RW_HEADS], axis=0) for n in range(len(ms))]

    def advance(d, inv):
        s = sbd[...]
        asrs = _dot1(d["ar"], s, _NT)
        p = pick(_dot1(inv, asrs[:c_len] + d["w1"]))
        pv = jnp.concatenate([p, d["vc"]], axis=0)
        y_s[d["sl"], :] = asrs[c_len:] + pick(_dot1(d["m"][4 * c_len:], pv))
        sbd[...] = s * d["g_end"] + _dot1(pv, d["bk"], _TN) * blockdiag

    n_chunks = tb_rows // c_len
    group = 8 if n_chunks % 8 == 0 else 1

    def chunk_group(gi, _):
        ds = [prepare(gi * group + n) for n in range(group)]
        for d, inv in zip(ds, invert([d["m"] for d in ds])):
            advance(d, inv)
        return 0

    lax.fori_loop(0, n_chunks // group, chunk_group, 0)

    y = y_s[...]
    mean = head_sum(y) * (1.0 / HEAD_DIM)
    yc = y - mean
    var = head_sum(yc * yc) * (1.0 / HEAD_DIM)
    yn = yc * lax.rsqrt(var + RW_GN_EPS) * lnw_ref[...] + lnb_ref[...]
    y_ref[0] = (yn + bonus) * g
    sout_ref[0] = sbd[...]


def _rwkv(u, prev, s0bd, prm, tb_rows, n_valid):
    b, t, _ = u.shape
    vec = lambda n: _full((1, n))
    kern = functools.partial(_rwkv_kernel, tb_rows=tb_rows, n_valid=n_valid)
    return pl.pallas_call(
        kern,
        grid=(b, t // tb_rows),
        in_specs=[pl.BlockSpec((1, tb_rows, 1024), lambda i, j: (i, j, 0)),
                  pl.BlockSpec((1, 1, 1024), lambda i, j: (i, 0, 0)),
                  pl.BlockSpec((1, 256, 256), lambda i, j: (i, 0, 0)),
                  vec(1024), vec(256), _full((128, 256)), vec(256), _full((128, 256)), _full((128, 256)),
                  vec(256), vec(256), vec(256), vec(256), vec(256)],
        out_specs=[pl.BlockSpec((1, tb_rows, 256), lambda i, j: (i, j, 0)),
                   pl.BlockSpec((1, 256, 256), lambda i, j: (i, 0, 0))],
        out_shape=[jax.ShapeDtypeStruct((b, t, 256), F32), jax.ShapeDtypeStruct((b, 256, 256), F32)],
        scratch_shapes=[pltpu.VMEM((256, 256), F32), pltpu.VMEM((1, 1024), F32)]
        + [pltpu.VMEM((tb_rows, 256), F32)] * 7,
        compiler_params=_params("parallel", "arbitrary"),
        name="rwkv7",
    )(u, prev, s0bd, *prm)


def _lru_kernel(u_ref, cb_ref, h0_ref, cw_ref, cbias_ref, wa_ref, ba_ref, wx_ref, bx_ref, lam_ref,
                out_ref, hl_ref, ext, hc, *, tb_rows, reset_first):
    tb = pl.program_id(1)

    @pl.when(tb == 0)
    def _():
        ext[0:8, :] = jnp.zeros((8, 256), F32)
        ext[5:8, :] = cb_ref[0]
        hc[...] = h0_ref[0]

    u = u_ref[0]
    xb, gate = u[:, :256], u[:, 256:]
    ext[8:8 + tb_rows, :] = xb
    cw = cw_ref[...]
    y = (cbias_ref[...] + cw[0:1] * ext[5:5 + tb_rows, :] + cw[1:2] * ext[6:6 + tb_rows, :]
         + cw[2:3] * ext[7:7 + tb_rows, :] + cw[3:4] * xb)
    ext[0:8, :] = ext[tb_rows:tb_rows + 8, :]
    gate_r = jax.nn.sigmoid(_dot3(y, wa_ref[...]) + ba_ref[...])
    gate_i = jax.nn.sigmoid(_dot3(y, wx_ref[...]) + bx_ref[...])
    lam = lam_ref[...]
    softplus_neg = jnp.maximum(-lam, 0.0) + jnp.log(1.0 + jnp.exp(-jnp.abs(lam)))
    log_a = (-LRU_C * softplus_neg) * gate_r
    a = jnp.exp(log_a)
    mult = jnp.sqrt(1.0 - jnp.exp(2.0 * log_a))
    row = lax.broadcasted_iota(I32, (tb_rows, 256), 0)
    if reset_first:
        mult = jnp.where((row == 0) & (tb == 0), 1.0, mult)
    bv = mult * gate_i * y
    d = 1
    while d < tb_rows:
        live = row >= d
        a_sh = jnp.where(live, pltpu.roll(a, d, 0), 1.0)
        b_sh = jnp.where(live, pltpu.roll(bv, d, 0), 0.0)
        bv = a * b_sh + bv
        a = a * a_sh
        d *= 2
    h = a * hc[...] + bv
    hc[...] = h[tb_rows - 1:tb_rows, :]
    gelu = 0.5 * gate * (1.0 + jnp.tanh(math.sqrt(2.0 / math.pi) * (gate + 0.044715 * gate * gate * gate)))
    out_ref[0] = h * gelu
    hl_ref[0] = h[tb_rows - 1:tb_rows, :]


def _lru(u, conv_buf, h0, prm, tb_rows, reset_first):
    b, t, _ = u.shape
    vec = _full((1, 256))
    kern = functools.partial(_lru_kernel, tb_rows=tb_rows, reset_first=reset_first)
    return pl.pallas_call(
        kern,
        grid=(b, t // tb_rows),
        in_specs=[pl.BlockSpec((1, tb_rows, 512), lambda i, j: (i, j, 0)),
                  pl.BlockSpec((1, 3, 256), lambda i, j: (i, 0, 0)),
                  pl.BlockSpec((1, 1, 256), lambda i, j: (i, 0, 0)),
                  _full((4, 256)), vec, _full((256, 256)), vec, _full((256, 256)), vec, vec],
        out_specs=[pl.BlockSpec((1, tb_rows, 256), lambda i, j: (i, j, 0)),
                   pl.BlockSpec((1, 1, 256), lambda i, j: (i, 0, 0))],
        out_shape=[jax.ShapeDtypeStruct((b, t, 256), F32), jax.ShapeDtypeStruct((b, 1, 256), F32)],
        scratch_shapes=[pltpu.VMEM((tb_rows + 8, 256), F32), pltpu.VMEM((1, 256), F32)],
        compiler_params=_params("parallel", "arbitrary"),
        name="rglru",
    )(u, conv_buf, h0, *prm)


def _sortable(x):
    bits = pltpu.bitcast(jnp.where(jnp.abs(x) < F32_MIN_NORMAL, 0.0, x), I32)
    return jnp.where(bits < 0, bits ^ 0x7FFFFFFF, bits)


def _bisect_threshold(count_gt, t0, cnt0, topk, bits=32):
    def cond(c):
        it, _, cnt = c
        return (it < bits) & (jnp.max(cnt) > topk)

    def body(c):
        it, t, cnt = c
        cand = t + lax.shift_left(jnp.int32(1), bits - 1 - it)
        cnt_cand = count_gt(cand)
        ok = cnt_cand >= topk
        return it + 1, jnp.where(ok, cand, t), jnp.where(ok, cnt_cand, cnt)

    _, t, cnt = lax.while_loop(cond, body, (jnp.int32(0), t0, cnt0))
    return t, cnt


def _dsa_prompt_steps(nq, qb_rows, kb_rows, kpg):
    steps = []
    for i in range(nq):
        nv = ((i + 1) * qb_rows + kb_rows - 1) // kb_rows
        ng = (nv + kpg - 1) // kpg
        steps += [(i, 0, g, nv, g, 0) for g in range(ng)] + [(i, 1, g, nv, ng - 1, g) for g in range(ng)]
    return list(zip(*steps))


def _dsa_prompt_kernel(tab_ref, qb_ref, qi_ref, wt_ref, ki4_ref, kb_ref, vt_ref, o_ref,
                       lhs_s, key_s, top_s, thr_s, qm_s, m_s, acc_s, *, qb_rows, kb_rows, sk, topk, n_steps, kpg):
    step = pl.program_id(0)
    i, ph, g, nv = (tab_ref[r * n_steps + step] for r in range(4))
    n_pairs = SA_HEADS // 2

    @pl.when((ph == 0) & (g == 0))
    def _init():
        qi, qb = qi_ref[...], qb_ref[...]
        lane = lax.broadcasted_iota(I32, (qb_rows, 128), 1)
        for h in range(SA_HEADS):
            hi, lo = _split_bf16(qi[:, h * 64:(h + 1) * 64])
            lhs_s[h * qb_rows:(h + 1) * qb_rows, :] = jnp.concatenate([hi, hi, lo, lo], axis=1)
        for p in range(n_pairs):
            qp = qb[:, p * 128:(p + 1) * 128]
            zero = jnp.zeros_like(qp)
            qm_s[p] = jnp.concatenate([jnp.where(lane < 64, qp, zero), jnp.where(lane < 64, zero, qp)], axis=0)
        m_s[...] = jnp.full(m_s.shape, NEG_BIG, F32)
        acc_s[...] = jnp.zeros(acc_s.shape, F32)

    def index_block(j, base):
        wt = wt_ref[...]
        qpos = i * qb_rows + lax.broadcasted_iota(I32, (sk, qb_rows), 1)
        for sub in range(kb_rows // sk):
            z = _dot_nt(ki4_ref[base + sub * sk:base + (sub + 1) * sk, :], lhs_s[...])
            sc = wt[0:1] * jnp.maximum(z[:, 0:qb_rows], 0.0)
            for h in range(1, SA_HEADS):
                sc = sc + wt[h:h + 1] * jnp.maximum(z[:, h * qb_rows:(h + 1) * qb_rows], 0.0)
            kpos = j * kb_rows + sub * sk + lax.broadcasted_iota(I32, (sk, qb_rows), 0)
            causal = kpos <= qpos
            sc = jnp.where(jnp.abs(sc) < F32_MIN_NORMAL, 0.0, sc)
            bits = pltpu.bitcast(sc, I32)
            key_s[j, sub * sk:(sub + 1) * sk, :] = jnp.where(causal, jnp.where(bits < 0, bits ^ 0x7FFFFFFF, bits),
                                                             INT_MIN)
            top = pltpu.bitcast(bits & jnp.int32(-65536), F32)
            top_s[j, sub * sk:(sub + 1) * sk, :] = jnp.where(causal, top, -jnp.inf).astype(BF16)

    for n in range(kpg):
        pl.when((ph == 0) & (g * kpg + n < nv))(functools.partial(index_block, g * kpg + n, n * kb_rows))

    @pl.when((ph == 1) & (g == 0))
    def _threshold():
        one, zero = jnp.ones((), BF16), jnp.zeros((), BF16)

        def count_top(cand):
            cand = jnp.maximum(cand, -32513)
            cand = jnp.where((cand < 0) & (cand > -129), -129, cand)
            pat = jnp.where(cand >= 0, cand, cand ^ 0x7FFF) & 0xFFFF
            cf = pltpu.bitcast(lax.shift_left(pat, 16), F32).astype(BF16)

            def body(jb, acc):
                x = jnp.where(top_s[jb] > cf, one, zero)
                parts = [x[r * 16:(r + 1) * 16] for r in range(kb_rows // 16)]
                while len(parts) > 1:
                    parts = [parts[a] + parts[a + 1] for a in range(0, len(parts), 2)]
                return acc + parts[0].astype(F32)

            acc = lax.fori_loop(0, nv, body, jnp.zeros((16, qb_rows), F32))
            return jnp.sum(acc, axis=0, keepdims=True).astype(I32)

        def top_step(it, c):
            t, cnt = c
            cand = t + lax.shift_left(jnp.int32(1), 15 - it)
            cnt_cand = count_top(cand)
            ok = cnt_cand >= topk
            return jnp.where(ok, cand, t), jnp.where(ok, cnt_cand, cnt)

        t_top0 = jnp.full((1, qb_rows), -32768, I32)
        t_top, cnt = lax.fori_loop(0, 16, top_step, (t_top0, count_top(t_top0)))

        def count_gt(t):
            def body(jb, acc):
                x = jnp.where(key_s[jb] > t, 1, 0)
                return acc + jnp.sum(x.reshape(kb_rows // 8, 8, qb_rows), axis=0)
            acc = lax.fori_loop(0, nv, body, jnp.zeros((8, qb_rows), I32))
            return jnp.sum(acc, axis=0, keepdims=True)

        t, cnt = _bisect_threshold(count_gt, lax.shift_left(t_top, 16) + 0xFFFF, cnt, topk, bits=16)
        thr_s[...] = t
        tied = cnt > topk

        @pl.when(jnp.max(jnp.where(tied, 1, 0)) > 0)
        def _drop_late_ties():
            need = jnp.where(tied, topk - count_gt(t + 1), NO_TIE).astype(F32)
            ri = lax.broadcasted_iota(I32, (kb_rows, kb_rows), 0)
            ci = lax.broadcasted_iota(I32, (kb_rows, kb_rows), 1)
            upto = (ri >= ci).astype(BF16)

            def body(jb, seen):
                k = key_s[jb]
                eq = jnp.where(k == t + 1, 1.0, 0.0)
                rank = seen + _dot(upto, eq.astype(BF16))
                key_s[jb] = jnp.where(eq * rank > need, INT_MIN, k)
                return seen + jnp.sum(eq, axis=0, keepdims=True)

            lax.fori_loop(0, nv, body, jnp.zeros((1, qb_rows), F32))

    def attend_block(j, base):
        win = slice(base, base + kb_rows)
        bias = jnp.where(key_s[j] > thr_s[...], 0.0, NEG_BIG)
        bias = jnp.concatenate([bias, bias], axis=1)
        for p in range(n_pairs):
            s = _dot_nt(kb_ref[win, p * 128:(p + 1) * 128], qm_s[p]) + bias
            m_old = m_s[p]
            m_new = jnp.maximum(m_old, jnp.max(s, axis=0, keepdims=True))
            alpha = jnp.exp2(m_old - m_new)
            prb = jnp.exp2((s - m_new).astype(BF16))
            m_s[p] = m_new
            for hh in range(2):
                h = 2 * p + hh
                cols = slice(hh * qb_rows, (hh + 1) * qb_rows)
                span = slice(hh * V_ROWS, (hh + 1) * V_ROWS)
                pv = _dot(vt_ref[h * V_ROWS:(h + 1) * V_ROWS, win], prb[:, cols])
                acc_s[p, span, :] = alpha[:, cols] * acc_s[p, span, :] + pv

    for n in range(kpg):
        pl.when((ph == 1) & (g * kpg + n < nv))(functools.partial(attend_block, g * kpg + n, n * kb_rows))

    @pl.when((ph == 1) & ((g + 1) * kpg >= nv))
    def _finish():
        for p in range(n_pairs):
            a = acc_s[p]
            a = jnp.concatenate([a[hh * V_ROWS:hh * V_ROWS + 64] / a[hh * V_ROWS + 64:hh * V_ROWS + 65]
                                 for hh in range(2)], axis=0)
            o_ref[:, p * 128:(p + 1) * 128] = a.T


def _dsa_prompt(qb, qi, wt, ki4, kb, vt, qb_rows, kb_rows):
    s = qb.shape[0]
    nq, nk = s // qb_rows, s // kb_rows
    sk = min(256, kb_rows)
    kpg = min(KEY_BLOCKS_PER_STEP, nk)
    tab = _dsa_prompt_steps(nq, qb_rows, kb_rows, kpg)
    n = len(tab[0])
    win = kpg * kb_rows
    qrow = lambda w: pl.BlockSpec((qb_rows, w), lambda st, t: (t[st], 0))
    kern = functools.partial(_dsa_prompt_kernel, qb_rows=qb_rows, kb_rows=kb_rows, sk=sk, topk=min(TOPK, s // 4),
                             n_steps=n, kpg=kpg)
    grid_spec = pltpu.PrefetchScalarGridSpec(
        num_scalar_prefetch=1,
        grid=(n,),
        in_specs=[qrow(512), qrow(512),
                  pl.BlockSpec((SA_HEADS, qb_rows), lambda st, t: (0, t[st])),
                  pl.BlockSpec((win, 256), lambda st, t: (t[4 * n + st], 0)),
                  pl.BlockSpec((win, 512), lambda st, t: (t[5 * n + st], 0)),
                  pl.BlockSpec((SA_HEADS * V_ROWS, win), lambda st, t: (0, t[5 * n + st]))],
        out_specs=qrow(512),
        scratch_shapes=[pltpu.VMEM((SA_HEADS * qb_rows, 256), BF16),
                        pltpu.VMEM((nk, kb_rows, qb_rows), I32),
                        pltpu.VMEM((nk, kb_rows, qb_rows), BF16),
                        pltpu.VMEM((1, qb_rows), I32),
                        pltpu.VMEM((SA_HEADS // 2, 2 * qb_rows, 128), BF16),
                        pltpu.VMEM((SA_HEADS // 2, 1, 2 * qb_rows), F32),
                        pltpu.VMEM((SA_HEADS // 2, 2 * V_ROWS, qb_rows), F32)],
    )
    return pl.pallas_call(
        kern,
        grid_spec=grid_spec,
        out_shape=jax.ShapeDtypeStruct((s, 512), F32),
        compiler_params=_params("arbitrary"),
        name="dsa_prompt",
    )(jnp.asarray(tab, I32).reshape(-1), qb, qi, wt, ki4, kb, vt)


PAGES_PER_STEP = 32
INDEX_PAGES_PER_STEP = 32


def _dsa_sample_index_kernel(pt_ref, qi_ref, kw_ref, ki4n_ref, *rest, n_pages, t_len, topk, pps):
    pages = rest[:pps]
    key_ref, thr_ref, lhs_s, wcol_s = rest[pps:]
    j = pl.program_id(1)
    n_steps = n_pages // pps

    @pl.when(j == 0)
    def _init():
        qi, kw = qi_ref[0], kw_ref[0]
        hi, lo = _split_bf16(jnp.concatenate([qi[:, h * 64:(h + 1) * 64] for h in range(SA_HEADS)], axis=0))
        lhs_s[...] = jnp.concatenate([hi, hi, lo, lo], axis=1)
        for h in range(SA_HEADS):
            wcol_s[h] = kw[:, 64 + h:65 + h]

    def head_sum(z):
        sc = wcol_s[0] * jnp.maximum(z[0:t_len], 0.0)
        for h in range(1, SA_HEADS):
            sc = sc + wcol_s[h] * jnp.maximum(z[h * t_len:(h + 1) * t_len], 0.0)
        return sc

    hi, lo = _split_bf16(jnp.concatenate([p[...] for p in pages], axis=1))
    keys = _sortable(head_sum(_dot(lhs_s[...], jnp.concatenate([hi, lo, hi, lo], axis=0))))
    for r in range(pps):
        key_ref[0, j * pps + r] = keys[:, r * PAGE:(r + 1) * PAGE]

    @pl.when(j == n_steps - 1)
    def _finish():
        sc = head_sum(_dot_nt(lhs_s[...], ki4n_ref[0]))
        tq = lax.broadcasted_iota(I32, (t_len, 128), 0)
        ts = lax.broadcasted_iota(I32, (t_len, 128), 1)
        key_ref[0, n_pages] = jnp.where(ts <= tq, _sortable(sc), INT_MIN)

        def count_gt(t):
            x = jnp.where(key_ref[0] > t[None], 1, 0)
            return jnp.sum(jnp.sum(x, axis=0), axis=1, keepdims=True)

        t0 = jnp.full((t_len, 1), INT_MIN, I32)
        thr, cnt = _bisect_threshold(count_gt, t0, count_gt(t0), topk)
        thr_ref[0] = jnp.broadcast_to(thr, (t_len, 128))
        tied = cnt > topk

        @pl.when(jnp.max(jnp.where(tied, 1, 0)) > 0)
        def _drop_late_ties():
            need = jnp.where(tied, topk - count_gt(thr + 1), NO_TIE).astype(F32)
            upto = (lax.broadcasted_iota(I32, (PAGE, PAGE), 0) <= lax.broadcasted_iota(I32, (PAGE, PAGE), 1))

            def body(p, seen):
                k = key_ref[0, p]
                eq = jnp.where(k == thr + 1, 1.0, 0.0)
                rank = seen + _dot(eq.astype(BF16), upto.astype(BF16))
                key_ref[0, p] = jnp.where(eq * rank > need, INT_MIN, k)
                return seen + jnp.sum(eq, axis=1, keepdims=True)

            lax.fori_loop(0, n_pages + 1, body, jnp.zeros((t_len, 1), F32))


def _dsa_sample_index(page_table, qi, kw, ki4_new, cache_kidx_t, layer, t_len):
    b = qi.shape[0]
    n_pages = page_table.shape[1]
    pps = math.gcd(n_pages, INDEX_PAGES_PER_STEP)
    n_steps = n_pages // pps

    def page_spec(r):
        return pl.BlockSpec((None, None, 64, PAGE),
                            lambda i, j, pt: (layer, pt[i * n_pages + j * pps + r], 0, 0))

    kern = functools.partial(_dsa_sample_index_kernel, n_pages=n_pages, t_len=t_len, pps=pps,
                             topk=min(TOPK, (n_pages * PAGE + t_len) // 4))
    grid_spec = pltpu.PrefetchScalarGridSpec(
        num_scalar_prefetch=1,
        grid=(b, n_steps),
        in_specs=[pl.BlockSpec((1, t_len, 512), lambda i, j, pt: (i, 0, 0)),
                  pl.BlockSpec((1, t_len, 128), lambda i, j, pt: (i, 0, 0)),
                  pl.BlockSpec((1, PAGE, 256), lambda i, j, pt: (i, 0, 0))]
        + [page_spec(r) for r in range(pps)],
        out_specs=[pl.BlockSpec((1, n_pages + 1, t_len, 128), lambda i, j, pt: (i, 0, 0, 0)),
                   pl.BlockSpec((1, t_len, 128), lambda i, j, pt: (i, 0, 0))],
        scratch_shapes=[pltpu.VMEM((SA_HEADS * t_len, 256), BF16), pltpu.VMEM((SA_HEADS, t_len, 1), F32)],
    )
    return pl.pallas_call(
        kern,
        grid_spec=grid_spec,
        out_shape=[jax.ShapeDtypeStruct((b, n_pages + 1, t_len, 128), I32),
                   jax.ShapeDtypeStruct((b, t_len, 128), I32)],
        compiler_params=_params("parallel", "arbitrary"),
        name="dsa_sample_index",
    )(page_table.reshape(-1), qi, kw, ki4_new, *([cache_kidx_t] * pps))


def _dsa_sample_attend_kernel(pt_ref, qb_ref, key_ref, keyn_ref, thr_ref, kn_ref, vn_ref, *rest, n_pages, t_len,
                              pps):
    kpages = rest[:pps]
    vpages = rest[pps:2 * pps]
    o_ref, qbd_s, m_s, l_s, acc_s = rest[2 * pps:]
    j = pl.program_id(1)
    n_steps = n_pages // pps
    thr = thr_ref[0][:, 0:1]

    @pl.when(j == 0)
    def _init():
        qb = qb_ref[0].astype(F32)
        lane = lax.broadcasted_iota(I32, (t_len, 512), 1) // 64
        qbd = jnp.concatenate([jnp.where(lane == h, qb, 0.0) for h in range(SA_HEADS)], axis=0)
        qbd_s[...] = qbd.astype(BF16)
        m_s[...] = jnp.full(m_s.shape, NEG_BIG, F32)
        l_s[...] = jnp.zeros(l_s.shape, F32)
        acc_s[...] = jnp.zeros(acc_s.shape, F32)

    def update(keys, s, pv):
        bias = jnp.where(keys > thr, 0.0, NEG_BIG)
        s = s + jnp.concatenate([bias] * SA_HEADS, axis=0)
        m_old = m_s[...]
        m_new = jnp.maximum(m_old, jnp.max(s, axis=1, keepdims=True))
        alpha = jnp.exp2(m_old - m_new)
        pr = jnp.exp2(s - m_new)
        l_s[...] = alpha * l_s[...] + jnp.sum(pr, axis=1, keepdims=True)
        acc_s[...] = alpha * acc_s[...] + pv(pr.astype(BF16))
        m_s[...] = m_new

    kcat = jnp.concatenate([p[...].astype(BF16) for p in kpages], axis=1)
    vcat = jnp.concatenate([p[...].astype(BF16) for p in vpages], axis=1)
    keys = jnp.concatenate([key_ref[0, r] for r in range(pps)], axis=1)
    update(keys, _dot(qbd_s[...], kcat), lambda pr: _dot_nt(pr, vcat))

    @pl.when(j == n_steps - 1)
    def _finish():
        kn, vn = kn_ref[0].astype(BF16), vn_ref[0].astype(BF16)
        update(keyn_ref[0, 0], _dot_nt(qbd_s[...], kn), lambda pr: _dot(pr, vn))
        o = acc_s[...] / l_s[...]
        lane = lax.broadcasted_iota(I32, (t_len, 512), 1) // 64
        out = jnp.zeros((t_len, 512), F32)
        for h in range(SA_HEADS):
            out = out + jnp.where(lane == h, o[h * t_len:(h + 1) * t_len], 0.0)
        o_ref[0] = out


def _dsa_sample_attend(page_table, qb, keys, thr, k_new, v_new, cache_k_t, cache_v_t, layer, t_len):
    b = qb.shape[0]
    n_pages = page_table.shape[1]
    pps = math.gcd(n_pages, PAGES_PER_STEP)
    n_steps = n_pages // pps
    rows = SA_HEADS * t_len

    def page_spec(r):
        return pl.BlockSpec((None, None, 512, PAGE),
                            lambda i, j, pt: (layer, pt[i * n_pages + j * pps + r], 0, 0))

    per_b = lambda n: pl.BlockSpec((1, t_len, n), lambda i, j, pt: (i, 0, 0))
    kern = functools.partial(_dsa_sample_attend_kernel, n_pages=n_pages, t_len=t_len, pps=pps)
    grid_spec = pltpu.PrefetchScalarGridSpec(
        num_scalar_prefetch=1,
        grid=(b, n_steps),
        in_specs=[per_b(512),
                  pl.BlockSpec((1, pps, t_len, 128), lambda i, j, pt: (i, j, 0, 0)),
                  pl.BlockSpec((1, 1, t_len, 128), lambda i, j, pt: (i, n_pages, 0, 0)),
                  per_b(128),
                  pl.BlockSpec((1, PAGE, 512), lambda i, j, pt: (i, 0, 0)),
                  pl.BlockSpec((1, PAGE, 512), lambda i, j, pt: (i, 0, 0))]
        + [page_spec(r) for r in range(pps)] * 2,
        out_specs=per_b(512),
        scratch_shapes=[pltpu.VMEM((rows, 512), BF16), pltpu.VMEM((rows, 1), F32), pltpu.VMEM((rows, 1), F32),
                        pltpu.VMEM((rows, 512), F32)],
    )
    return pl.pallas_call(
        kern,
        grid_spec=grid_spec,
        out_shape=jax.ShapeDtypeStruct((b, t_len, 512), F32),
        compiler_params=_params("parallel", "arbitrary"),
        name="dsa_sample_attend",
    )(page_table.reshape(-1), qb, keys, keys, thr, k_new, v_new,
      *([cache_k_t] * pps), *([cache_v_t] * pps))


def _matmul_kernel(x_ref, w_ref, o_ref):
    o_ref[...] = _dot(x_ref[...].astype(BF16), w_ref[...])


def _matmul(x, w):
    m, n = x.shape[0], w.shape[1]
    return pl.pallas_call(
        _matmul_kernel,
        grid=(1,),
        in_specs=[_full(x.shape), _full(w.shape)],
        out_specs=_full((m, n)),
        out_shape=jax.ShapeDtypeStruct((m, n), F32),
        compiler_params=_params("arbitrary"),
        name="mem_kv_proj",
    )(x, w)


def _post1_kernel(x_ref, yrw_ref, ysa_ref, ylru_ref, wo1_ref, wo2_ref, wo3_ref, g_ref, wq_ref, x1_ref, qm_ref):
    x1 = (x_ref[...] + _dot(yrw_ref[...].astype(BF16), wo1_ref[...])
          + _dot(ysa_ref[...].astype(BF16), wo2_ref[...]) + _dot(ylru_ref[...].astype(BF16), wo3_ref[...]))
    x1_ref[...] = x1
    qm = _dot(_rms(x1, g_ref[...]).astype(BF16), wq_ref[...])
    qm_ref[...] = (qm * (MEM_HEAD_DIM ** -0.5)).astype(BF16)


def _post1(x, yrw, ysa, ylru, wo1, wo2, wo3, g, wq, tm):
    m = x.shape[0]
    row = lambda n: pl.BlockSpec((tm, n), lambda i: (i, 0))
    return pl.pallas_call(
        _post1_kernel,
        grid=(m // tm,),
        in_specs=[row(1024), row(256), row(512), row(256), _full(wo1.shape), _full(wo2.shape), _full(wo3.shape),
                  _full((1, 1024)), _full(wq.shape)],
        out_specs=[row(1024), row(1024)],
        out_shape=[jax.ShapeDtypeStruct((m, 1024), F32), jax.ShapeDtypeStruct((m, 1024), BF16)],
        compiler_params=_params("parallel"),
        name="out_proj_memq",
    )(x, yrw, ysa, ylru, wo1, wo2, wo3, g, wq)


def _mem_attn_kernel(q_ref, mk_ref, mv_ref, o_ref):
    q = q_ref[0]
    mk = mk_ref[0].astype(BF16)
    mv = mv_ref[0].astype(BF16)
    for h in range(MEM_HEADS):
        sl = slice(h * MEM_HEAD_DIM, (h + 1) * MEM_HEAD_DIM)
        s = _dot_nt(q[:, sl], mk[:, sl])
        p = jnp.exp(s - jnp.max(s, axis=1, keepdims=True))
        o = _dot(p.astype(BF16), mv[:, sl]) / jnp.sum(p, axis=1, keepdims=True)
        o_ref[0, :, sl] = o.astype(BF16)


def _mem_attn(q, mk, mv, tm):
    b, t, _ = q.shape
    return pl.pallas_call(
        _mem_attn_kernel,
        grid=(b, t // tm),
        in_specs=[pl.BlockSpec((1, tm, 1024), lambda i, j: (i, j, 0)),
                  pl.BlockSpec((1, 256, 1024), lambda i, j: (i, 0, 0)),
                  pl.BlockSpec((1, 256, 1024), lambda i, j: (i, 0, 0))],
        out_specs=pl.BlockSpec((1, tm, 1024), lambda i, j: (i, j, 0)),
        out_shape=jax.ShapeDtypeStruct((b, t, 1024), BF16),
        compiler_params=_params("parallel", "parallel"),
        name="mem_attn",
    )(q, mk, mv)


def _post2_kernel(x1_ref, o_ref, wo_ref, g_ref, wg_ref, wu_ref, wd_ref, gf_ref, out_ref, *, last):
    x2 = x1_ref[...] + _dot(o_ref[...], wo_ref[...])
    hb = _rms(x2, g_ref[...]).astype(BF16)
    gt = _dot(hb, wg_ref[...])
    up = _dot(hb, wu_ref[...])
    act = (gt * jax.nn.sigmoid(gt) * up).astype(BF16)
    x3 = x2 + _dot(act, wd_ref[...])
    out_ref[...] = _rms(x3, gf_ref[...]) if last else x3


def _post2(x1, o, wo, g, wg, wu, wd, gf, tm, last):
    m = x1.shape[0]
    row = lambda n: pl.BlockSpec((tm, n), lambda i: (i, 0))
    once = lambda w: pl.BlockSpec(w.shape, lambda i: (0, 0), pipeline_mode=pl.Buffered(1))
    return pl.pallas_call(
        functools.partial(_post2_kernel, last=last),
        grid=(m // tm,),
        in_specs=[row(1024), row(1024), once(wo), _full((1, 1024)), once(wg), once(wu), once(wd), _full((1, 1024))],
        out_specs=row(1024),
        out_shape=jax.ShapeDtypeStruct((m, 1024), F32),
        compiler_params=_params("parallel"),
        name="memo_swiglu",
    )(x1, o, wo, g, wg, wu, wd, gf)


def _blockdiag4(w):
    n = w.shape[-1]
    return jnp.einsum("gij,gh->gihj", w, jnp.eye(4, dtype=w.dtype)).reshape(4 * n, 4 * n)


def _state_to_blockdiag(s):
    b = s.shape[0]
    return jnp.einsum("bhvk,hg->bhvgk", s, jnp.eye(4, dtype=s.dtype)).reshape(b, 256, 256)


def _blockdiag_to_state(sbd):
    b = sbd.shape[0]
    return jnp.einsum("bhvgk,hg->bhvk", sbd.reshape(b, 4, 64, 4, 64), jnp.eye(4, dtype=sbd.dtype))


def _layer_weights(l, w_in, w_out, rw, lru, g_mix, g_mem, w_mq, w_mk, w_mv, w_mo, g_ffn, w_gate, w_up, w_down):
    wi = w_in[l]
    o = RW_COLS
    cut = lambda a, n: wi[:, a:a + n].astype(BF16)
    wkw = jnp.pad(wi[:, o + 2048:o + 2120], ((0, 0), (0, 56))).astype(BF16)
    in_ws = (cut(0, 1024), cut(o, 512), cut(o + 512, 512), cut(o + 1024, 512), cut(o + 1536, 512), wkw,
             cut(o + 2120, 512))
    (rw_mu, rw_w0, rw_w2, rw_a0, rw_a2, rw_g2, rw_kk, rw_ka, rw_rk, rw_ln_w, rw_ln_b) = rw
    z = jnp.zeros((64, 256), F32)
    v256 = lambda a: a[l].reshape(1, 256)
    rw_prm = (rw_mu[l].reshape(1, 1024), v256(rw_w0), jnp.concatenate([rw_w2[l], z], axis=0), v256(rw_a0),
              jnp.concatenate([z, rw_a2[l]], axis=0), rw_g2[l], v256(rw_kk), v256(rw_ka), v256(rw_rk),
              v256(rw_ln_w), v256(rw_ln_b))
    (lru_conv_w, lru_conv_b, lru_wa, lru_ba, lru_wx, lru_bx, lru_lambda) = lru
    lru_prm = (lru_conv_w[l], v256(lru_conv_b), _blockdiag4(lru_wa[l]), v256(lru_ba), _blockdiag4(lru_wx[l]),
               v256(lru_bx), v256(lru_lambda))
    wo = w_out[l].astype(BF16)
    return dict(
        in_ws=in_ws, rw=rw_prm, lru=lru_prm, g_mix=g_mix[l].reshape(1, 1024),
        wo=(wo[:256], wo[256:768], wo[768:]), g_mem=g_mem[l].reshape(1, 1024), wq=w_mq[l].astype(BF16),
        wmk=w_mk[l].astype(BF16), wmv=w_mv[l].astype(BF16), wmo=w_mo[l].astype(BF16),
        g_ffn=g_ffn[l].reshape(1, 1024), wg=w_gate[l].astype(BF16), wu=w_up[l].astype(BF16),
        wd=w_down[l].astype(BF16))


def _pick_tile(n, pref):
    t = min(n, pref)
    while n % t:
        t //= 2
    return t


def _prompt_layer(x, mem, w, tabs, g_final, last):
    s = x.shape[0]
    tm = _pick_tile(s, 256)
    urw, kt, vt, qi, _, kwt, ulru, qb, kb, vtb, ki4 = _in_proj(x, w["g_mix"], w["in_ws"], tabs, tm)
    tb = _pick_tile(s, 512)
    y_rw, sbd = _rwkv(urw[None], jnp.zeros((1, 1, 1024), F32), jnp.zeros((1, 256, 256), F32), w["rw"], tb, tb)
    y_lru, h_last = _lru(ulru[None], jnp.zeros((1, 3, 256), F32), jnp.zeros((1, 1, 256), F32), w["lru"], tb, True)
    y_sa = _dsa_prompt(qb, qi, kwt[64:64 + SA_HEADS], ki4, kb, vtb, _pick_tile(s, 256), _pick_tile(s, 1024))
    mk = _matmul(mem, w["wmk"])
    mv = _matmul(mem, w["wmv"])
    x1, qm = _post1(x, y_rw[0], y_sa, y_lru[0], *w["wo"], w["g_mem"], w["wq"], tm)
    o = _mem_attn(qm[None], mk[None], mv[None], tm)
    x3 = _post2(x1, o[0], w["wmo"], w["g_ffn"], w["wg"], w["wu"], w["wd"], g_final, tm, last)
    rows_major = lambda a: jnp.transpose(a.reshape(SA_HEADS, HEAD_DIM, s), (2, 0, 1))[None]
    new = dict(k=rows_major(kt), v=rows_major(vt), kidx=kwt[:64].T[None],
               mk=mk.reshape(1, 256, 4, 256), mv=mv.reshape(1, 256, 4, 256), rwkv=_blockdiag_to_state(sbd),
               shift=urw[None, s - 1], h=h_last[:, 0], conv=ulru[None, s - 3:, :256])
    return x3, new


def _sample_layer(x, l, w, tabs, g_final, cache_k, cache_v, cache_kidx, cache_mem_k, cache_mem_v, state_rwkv,
                  state_rwkv_shift, state_lru_h, state_lru_conv, page_table, db, t, last):
    m = db * t
    urw, kt, vt, qi, kw, _, ulru, qb, _, _, ki4 = _in_proj(x, w["g_mix"], w["in_ws"], tabs, m)
    k, v = kt.T, vt.T
    urw3 = urw.reshape(db, t, 1024)
    u_pad = jnp.pad(urw3, ((0, 0), (0, RW_CHUNK - t), (0, 0)))
    y_rw, sbd = _rwkv(u_pad, state_rwkv_shift[l][:, None], _state_to_blockdiag(state_rwkv[l]), w["rw"], RW_CHUNK, t)
    ulru3 = ulru.reshape(db, t, 512)
    y_lru, h_last = _lru(ulru3, state_lru_conv[l], state_lru_h[l][:, None], w["lru"], t, False)
    pad_new = lambda a: jnp.pad(a.reshape(db, t, -1), ((0, 0), (0, PAGE - t), (0, 0)))
    keys, thr = _dsa_sample_index(page_table, qi.reshape(db, t, 512), kw.reshape(db, t, 128), pad_new(ki4),
                                  cache_kidx, l, t)
    y_sa = _dsa_sample_attend(page_table, qb.reshape(db, t, 512), keys, thr, pad_new(k), pad_new(v),
                              cache_k, cache_v, l, t)
    x1, qm = _post1(x, y_rw[:, :t].reshape(m, 256), y_sa.reshape(m, 512), y_lru.reshape(m, 256), *w["wo"],
                    w["g_mem"], w["wq"], m)
    o = _mem_attn(qm.reshape(db, t, 1024), cache_mem_k[l].reshape(db, 256, 1024),
                  cache_mem_v[l].reshape(db, 256, 1024), t)
    x3 = _post2(x1, o.reshape(m, 1024), w["wmo"], w["g_ffn"], w["wg"], w["wu"], w["wd"], g_final, m, last)
    conv = jnp.concatenate([state_lru_conv[l], ulru3[:, :, :256]], axis=1)[:, t:]
    new = dict(k=k.reshape(db, t, 8, 64), v=v.reshape(db, t, 8, 64), kidx=kw.reshape(db, t, 128)[:, :, :64],
               rwkv=_blockdiag_to_state(sbd), shift=urw3[:, t - 1], h=h_last[:, 0], conv=conv)
    return x3, new


def kernel(x_prompt, x_sample, mem_prompt, cache_k, cache_v, cache_kidx, cache_mem_k, cache_mem_v, state_rwkv, state_rwkv_shift, state_lru_h, state_lru_conv, page_table, g_mix, w_in, w_out, rw_mu, rw_w0, rw_w2, rw_a0, rw_a2, rw_g2, rw_kk, rw_ka, rw_rk, rw_ln_w, rw_ln_b, lru_conv_w, lru_conv_b, lru_wa, lru_ba, lru_wx, lru_bx, lru_lambda, g_mem, w_mq, w_mk, w_mv, w_mo, g_ffn, w_gate, w_up, w_down, g_final):
    depth = w_in.shape[0]
    _, s, _ = x_prompt.shape
    db, t, _ = x_sample.shape
    past_len = page_table.shape[1] * PAGE
    rw = (rw_mu, rw_w0, rw_w2, rw_a0, rw_a2, rw_g2, rw_kk, rw_ka, rw_rk, rw_ln_w, rw_ln_b)
    lru = (lru_conv_w, lru_conv_b, lru_wa, lru_ba, lru_wx, lru_bx, lru_lambda)
    tabs_p = _rope_tables(jnp.arange(s))
    tabs_s = _rope_tables(jnp.tile(past_len + jnp.arange(t), db))
    gf = g_final.reshape(1, 1024)
    n_pool = cache_k.shape[1]
    cache_kidx = jnp.transpose(cache_kidx, (0, 1, 3, 2))
    cache_k = jnp.transpose(cache_k, (0, 1, 3, 4, 2)).reshape(depth, n_pool, SA_WIDTH, PAGE)
    cache_v = jnp.transpose(cache_v, (0, 1, 3, 4, 2)).reshape(depth, n_pool, SA_WIDTH, PAGE)
    xp, xs = x_prompt[0], x_sample.reshape(db * t, 1024)
    mem = mem_prompt[0]
    news_p, news_s = [], []
    for l in range(depth):
        w = _layer_weights(l, w_in, w_out, rw, lru, g_mix, g_mem, w_mq, w_mk, w_mv, w_mo, g_ffn, w_gate, w_up,
                           w_down)
        last = l == depth - 1
        xp, new_p = _prompt_layer(xp, mem, w, tabs_p, gf, last)
        xs, new_s = _sample_layer(xs, l, w, tabs_s, gf, cache_k, cache_v, cache_kidx, cache_mem_k, cache_mem_v,
                                  state_rwkv, state_rwkv_shift, state_lru_h, state_lru_conv, page_table, db, t, last)
        news_p.append(new_p)
        news_s.append(new_s)
    stk = lambda news, name: jnp.stack([n[name] for n in news])
    return (xp[None], xs.reshape(db, t, 1024),
            stk(news_p, "k"), stk(news_p, "v"), stk(news_p, "kidx"), stk(news_p, "mk"), stk(news_p, "mv"),
            stk(news_p, "rwkv"), stk(news_p, "shift"), stk(news_p, "h"), stk(news_p, "conv"),
            stk(news_s, "k"), stk(news_s, "v"), stk(news_s, "kidx"), stk(news_s, "rwkv"), stk(news_s, "shift"),
            stk(news_s, "h"), stk(news_s, "conv"))
```
